```python
import jax, jax.numpy as jnp
from jax import lax
import numpy as np

D_MODEL = 1024
BATCH = 16
SEQ = 256
DEPTH = 4
DEC_BATCH = 4
DEC_SEQ = 1024
PAST_LEN = 256

GRID_W = 64
N_MIXERS = 3
N_MLA = (DEPTH + 2) // N_MIXERS
N_CONV = (DEPTH + 1) // N_MIXERS
N_SSD = DEPTH // N_MIXERS

MLA_HEADS = 16
QK_NOPE = 64
QK_ROPE = 32
QK_DIM = QK_NOPE + QK_ROPE
V_HEAD = 64
Q_LORA = 384
KV_LORA = 256
ROPE_AXIS = QK_ROPE // 2
ROPE_BASE = 10000.0
Q_BLOCK = 128

CONV_WIDTH = 31

SSD_INNER = 2 * D_MODEL
SSD_HEADDIM = 64
SSD_HEADS = SSD_INNER // SSD_HEADDIM
SSD_GROUPS = 4
SSD_STATE = 128
SSD_CONV = 5
SSD_CONV_DIM = SSD_INNER + 2 * SSD_GROUPS * SSD_STATE
SSD_IN_DIM = SSD_INNER + SSD_CONV_DIM + 2 * SSD_HEADS
CHUNK = 128

FFN_HIDDEN = ((8 * D_MODEL + 3 * 256 - 1) // (3 * 256)) * 256
EPS = 1e-6

kernel_name = 'hybrid_mla_conformer_ssd_diffusion_step'


def rmsnorm(x, g):
    x32 = x.astype(jnp.float32)
    y = x32 * lax.rsqrt(jnp.mean(x32 * x32, axis=-1, keepdims=True) + EPS)
    return (y * g.astype(jnp.float32)).astype(x.dtype)


def layernorm(x, g, b):
    x32 = x.astype(jnp.float32)
    xc = x32 - jnp.mean(x32, axis=-1, keepdims=True)
    y = xc * lax.rsqrt(jnp.mean(xc * xc, axis=-1, keepdims=True) + EPS)
    return (y * g.astype(jnp.float32) + b.astype(jnp.float32)).astype(x.dtype)


def adaln(cond, w, b):
    return jnp.split(jax.nn.silu(cond) @ w + b, 6, axis=-1)


def modulate(h, shift, scale):
    return h * (1 + scale) + shift


def swiglu(h, w_in, w_out):
    a, u = jnp.split(h @ w_in, 2, axis=-1)
    return (jax.nn.silu(a) * u) @ w_out


def axial_angles(n_tok):
    rows = n_tok // GRID_W
    row = jnp.repeat(jnp.arange(rows, dtype=jnp.float32), GRID_W)
    col = jnp.tile(jnp.arange(GRID_W, dtype=jnp.float32), rows)
    inv = ROPE_BASE ** (-jnp.arange(0, ROPE_AXIS, 2, dtype=jnp.float32) / ROPE_AXIS)
    return row[:, None] * inv, col[:, None] * inv


def rotate_half(x, ang):
    m = ang.shape[-1]
    cos = jnp.cos(ang)[:, None, :].astype(x.dtype)
    sin = jnp.sin(ang)[:, None, :].astype(x.dtype)
    x1, x2 = x[..., :m], x[..., m:]
    return jnp.concatenate([x1 * cos - x2 * sin, x2 * cos + x1 * sin], axis=-1)


def axial_rope(x, ang_r, ang_c):
    pe = x[..., QK_NOPE:]
    return jnp.concatenate([x[..., :QK_NOPE], rotate_half(pe[..., :ROPE_AXIS], ang_r),
                            rotate_half(pe[..., ROPE_AXIS:], ang_c)], axis=-1)


def attention(q, k, v):
    b, tq, h, dk = q.shape
    scale = dk ** -0.5
    qb = q.reshape(b, tq // Q_BLOCK, Q_BLOCK, h, dk).transpose(1, 0, 2, 3, 4)

    def block(qi):
        s = jnp.einsum('bqhd,bkhd->bhqk', qi, k).astype(jnp.float32) * scale
        p = jax.nn.softmax(s, axis=-1).astype(v.dtype)
        return jnp.einsum('bhqk,bkhd->bqhd', p, v)

    o = lax.map(block, qb)
    return o.transpose(1, 0, 2, 3, 4).reshape(b, tq, h, v.shape[-1])


def mla_queries(h, w_dq, g_q, w_uq, g_qn):
    b, t, _ = h.shape
    q = (rmsnorm(h @ w_dq, g_q) @ w_uq).reshape(b, t, MLA_HEADS, QK_DIM)
    return rmsnorm(q, g_qn)


def mla_kv_latent(h, w_dkv, g_kv):
    kv = h @ w_dkv
    return rmsnorm(kv[..., :KV_LORA], g_kv), kv[..., KV_LORA:]


def mla_keys_values(ckv, kpe, w_ukv, g_kn):
    b, s, _ = ckv.shape
    kv = (ckv @ w_ukv).reshape(b, s, MLA_HEADS, QK_NOPE + V_HEAD)
    k_pe = jnp.broadcast_to(kpe[:, :, None, :], (b, s, MLA_HEADS, QK_ROPE))
    k = jnp.concatenate([kv[..., :QK_NOPE], k_pe], axis=-1)
    return rmsnorm(k, g_kn), kv[..., QK_NOPE:]


def mla_output(q, k, v, w_o):
    b, t = q.shape[:2]
    return attention(q, k, v).reshape(b, t, MLA_HEADS * V_HEAD) @ w_o


def mla_context(h, w_dq, g_q, w_uq, w_dkv, g_kv, w_ukv, g_qn, g_kn, w_o):
    q = mla_queries(h, w_dq, g_q, w_uq, g_qn)
    ckv, kpe = mla_kv_latent(h, w_dkv, g_kv)
    k, v = mla_keys_values(ckv, kpe, w_ukv, g_kn)
    return mla_output(q, k, v, w_o), ckv, kpe


def mla_latent(h, ckv_ctx, kpe_ctx, ang_r, ang_c, w_dq, g_q, w_uq, w_dkv, g_kv, w_ukv, g_qn, g_kn, w_o):
    q = axial_rope(mla_queries(h, w_dq, g_q, w_uq, g_qn), ang_r, ang_c)
    ckv, kpe = mla_kv_latent(h, w_dkv, g_kv)
    k_l, v_l = mla_keys_values(ckv, kpe, w_ukv, g_kn)
    k_l = axial_rope(k_l, ang_r, ang_c)
    k_c, v_c = mla_keys_values(ckv_ctx, kpe_ctx, w_ukv, g_kn)
    k = jnp.concatenate([k_c, k_l], axis=1)
    v = jnp.concatenate([v_c, v_l], axis=1)
    return mla_output(q, k, v, w_o)


def depthwise_conv(x, w, b):
    pad = (w.shape[0] - 1) // 2
    y = lax.conv_general_dilated(x, w[:, None, :], window_strides=(1,), padding=[(pad, pad)],
                                 dimension_numbers=('NWC', 'WIO', 'NWC'),
                                 feature_group_count=x.shape[-1])
    return y + b


def conv_module(h, w_pw1, b_pw1, w_dw, b_dw, g_ln, b_ln, w_pw2, b_pw2):
    a, g = jnp.split(h @ w_pw1 + b_pw1, 2, axis=-1)
    u = a * jax.nn.sigmoid(g)
    u = jax.nn.silu(layernorm(depthwise_conv(u, w_dw, b_dw), g_ln, b_ln))
    return u @ w_pw2 + b_pw2


def ssd_scan(x, dt, a, bm, cm, d_skip, h0):
    f32 = jnp.float32
    b, L, H, P = x.shape
    G, N = bm.shape[-2:]
    R = H // G
    nc = L // CHUNK
    xr = x.astype(f32).reshape(b, nc, CHUNK, G, R, P)
    dtr = dt.reshape(b, nc, CHUNK, G, R)
    br = bm.astype(f32).reshape(b, nc, CHUNK, G, N)
    cr = cm.astype(f32).reshape(b, nc, CHUNK, G, N)
    a_cum = jnp.cumsum(dtr * a.reshape(G, R), axis=2)
    xdt = xr * dtr[..., None]
    ac = jnp.moveaxis(a_cum, 2, -1)
    seg = ac[..., :, None] - ac[..., None, :]
    lower = jnp.tril(jnp.ones((CHUNK, CHUNK), dtype=bool))
    lmat = jnp.exp(jnp.where(lower, seg, -jnp.inf))
    cb = jnp.einsum('bcign,bcjgn->bcgij', cr, br)
    y_diag = jnp.einsum('bcgrij,bcjgrp->bcigrp', cb[:, :, :, None] * lmat, xdt)
    decay_in = jnp.exp(a_cum[:, :, -1:] - a_cum)
    states = jnp.einsum('bcjgn,bcjgrp->bcgrpn', br, xdt * decay_in[..., None])
    chunk_decay = jnp.exp(a_cum[:, :, -1])

    def step(hc, inp):
        st, dec = inp
        return hc * dec[..., None, None] + st, hc

    h_fin, h_prev = lax.scan(step, h0.astype(f32).reshape(b, G, R, P, N),
                             (jnp.moveaxis(states, 1, 0), jnp.moveaxis(chunk_decay, 1, 0)))
    h_prev = jnp.moveaxis(h_prev, 0, 1)
    y_off = jnp.einsum('bcign,bcgrpn->bcigrp', cr, h_prev) * jnp.exp(a_cum)[..., None]
    y = y_diag + y_off + xr * d_skip.astype(f32).reshape(G, R)[..., None]
    return y.reshape(b, L, H, P).astype(x.dtype), h_fin.reshape(b, H, P, N).astype(x.dtype)


def ssd_mixer(h, h0, w_in, w_conv, b_conv, dt_bias, a_log, d_skip, g_norm, w_out):
    b, t, _ = h.shape
    zxbcdt = h @ w_in
    z = zxbcdt[..., :SSD_INNER]
    xbc = jax.nn.silu(depthwise_conv(zxbcdt[..., SSD_INNER:SSD_INNER + SSD_CONV_DIM], w_conv, b_conv))
    dt_raw = zxbcdt[..., SSD_INNER + SSD_CONV_DIM:].reshape(b, t, 2, SSD_HEADS).astype(jnp.float32)
    gn = SSD_GROUPS * SSD_STATE
    x = xbc[..., :SSD_INNER].reshape(b, t, SSD_HEADS, SSD_HEADDIM)
    bm = xbc[..., SSD_INNER:SSD_INNER + gn].reshape(b, t, SSD_GROUPS, SSD_STATE)
    cm = xbc[..., SSD_INNER + gn:].reshape(b, t, SSD_GROUPS, SSD_STATE)
    dt = jax.nn.softplus(dt_raw + dt_bias.astype(jnp.float32))
    a = -jnp.exp(a_log.astype(jnp.float32))
    flip = lambda u: jnp.flip(u, axis=1)
    y_f, s_f = ssd_scan(x, dt[:, :, 0], a[0], bm, cm, d_skip[0], h0[:, 0])
    y_b, s_b = ssd_scan(flip(x), flip(dt[:, :, 1]), a[1], flip(bm), flip(cm), d_skip[1], h0[:, 1])
    y = (y_f + flip(y_b)).reshape(b, t, SSD_INNER)
    yz = (y * jax.nn.silu(z)).reshape(b, t, SSD_GROUPS, SSD_INNER // SSD_GROUPS)
    y = rmsnorm(yz, g_norm.reshape(SSD_GROUPS, SSD_INNER // SSD_GROUPS)).reshape(b, t, SSD_INNER)
    return y @ w_out, jnp.stack([s_f, s_b], axis=1)


def setup_inputs(seed: int = 0) -> dict:
    key = jax.random.key(seed)
    ks = iter(jax.random.split(key, 48))
    f32 = jnp.float32
    D = D_MODEL

    def nrm(shape, scale=1.0):
        return jax.random.normal(next(ks), shape, f32) * scale

    def gain(shape):
        return 1.0 + nrm(shape, 0.05)

    dt0 = jnp.exp(jax.random.uniform(next(ks), (N_SSD, 2, SSD_HEADS), f32,
                                     minval=np.log(0.001), maxval=np.log(0.1)))
    return {
        'x_prompt': nrm((BATCH, SEQ, D)),
        'x_sample': nrm((DEC_BATCH, DEC_SEQ, D)),
        'cache_ckv': nrm((DEC_BATCH, N_MLA, PAST_LEN, KV_LORA)),
        'cache_kpe': nrm((DEC_BATCH, N_MLA, PAST_LEN, QK_ROPE)),
        'state_ssm': nrm((DEC_BATCH, N_SSD, 2, SSD_HEADS, SSD_HEADDIM, SSD_STATE), 0.1),
        'c': nrm((DEC_BATCH, D)),
        'c_ctx': nrm((D,)),
        'w_ada': nrm((DEPTH, D, 6 * D), 0.5 * D ** -0.5),
        'b_ada': nrm((DEPTH, 6 * D), 0.02),
        'g_norm1': gain((DEPTH, D)),
        'g_norm2': gain((DEPTH, D)),
        'mla_w_dq': nrm((N_MLA, D, Q_LORA), D ** -0.5),
        'mla_g_q': gain((N_MLA, Q_LORA)),
        'mla_w_uq': nrm((N_MLA, Q_LORA, MLA_HEADS * QK_DIM), Q_LORA ** -0.5),
        'mla_w_dkv': nrm((N_MLA, D, KV_LORA + QK_ROPE), D ** -0.5),
        'mla_g_kv': gain((N_MLA, KV_LORA)),
        'mla_w_ukv': nrm((N_MLA, KV_LORA, MLA_HEADS * (QK_NOPE + V_HEAD)), KV_LORA ** -0.5),
        'mla_g_qn': gain((N_MLA, QK_DIM)),
        'mla_g_kn': gain((N_MLA, QK_DIM)),
        'mla_w_o': nrm((N_MLA, MLA_HEADS * V_HEAD, D), (MLA_HEADS * V_HEAD) ** -0.5),
        'cv_w_pw1': nrm((N_CONV, D, 2 * D), D ** -0.5),
        'cv_b_pw1': nrm((N_CONV, 2 * D), 0.02),
        'cv_w_dw': nrm((N_CONV, CONV_WIDTH, D), CONV_WIDTH ** -0.5),
        'cv_b_dw': nrm((N_CONV, D), 0.02),
        'cv_g_ln': gain((N_CONV, D)),
        'cv_b_ln': nrm((N_CONV, D), 0.02),
        'cv_w_pw2': nrm((N_CONV, D, D), D ** -0.5),
        'cv_b_pw2': nrm((N_CONV, D), 0.02),
        'ssd_w_in': nrm((N_SSD, D, SSD_IN_DIM), D ** -0.5),
        'ssd_w_conv': nrm((N_SSD, SSD_CONV, SSD_CONV_DIM), SSD_CONV ** -0.5),
        'ssd_b_conv': nrm((N_SSD, SSD_CONV_DIM), 0.02),
        'ssd_dt_bias': dt0 + jnp.log(-jnp.expm1(-dt0)),
        'ssd_a_log': jnp.log(jax.random.uniform(next(ks), (N_SSD, 2, SSD_HEADS), f32, minval=1.0, maxval=16.0)),
        'ssd_d': gain((N_SSD, 2, SSD_HEADS)),
        'ssd_g_norm': gain((N_SSD, SSD_INNER)),
        'ssd_w_out': nrm((N_SSD, SSD_INNER, D), SSD_INNER ** -0.5),
        'ffn_w_in': nrm((DEPTH, D, 2 * FFN_HIDDEN), D ** -0.5),
        'ffn_w_out': nrm((DEPTH, FFN_HIDDEN, D), FFN_HIDDEN ** -0.5),
    }


def reference(x_prompt, x_sample, cache_ckv, cache_kpe, state_ssm, c, c_ctx,
              w_ada, b_ada, g_norm1, g_norm2,
              mla_w_dq, mla_g_q, mla_w_uq, mla_w_dkv, mla_g_kv, mla_w_ukv, mla_g_qn, mla_g_kn, mla_w_o,
              cv_w_pw1, cv_b_pw1, cv_w_dw, cv_b_dw, cv_g_ln, cv_b_ln, cv_w_pw2, cv_b_pw2,
              ssd_w_in, ssd_w_conv, ssd_b_conv, ssd_dt_bias, ssd_a_log, ssd_d, ssd_g_norm, ssd_w_out,
              ffn_w_in, ffn_w_out):
    ang_r, ang_c = axial_angles(x_sample.shape[1])
    cond_ctx = c_ctx[None, None, :]
    cond_lat = c[:, None, :]
    xp, xs = x_prompt, x_sample
    ckv_list, kpe_list, ssm_list = [], [], []
    for i in range(DEPTH):
        kind, j = i % N_MIXERS, i // N_MIXERS
        p_sh1, p_sc1, p_g1, p_sh2, p_sc2, p_g2 = adaln(cond_ctx, w_ada[i], b_ada[i])
        s_sh1, s_sc1, s_g1, s_sh2, s_sc2, s_g2 = adaln(cond_lat, w_ada[i], b_ada[i])
        hp = modulate(rmsnorm(xp, g_norm1[i]), p_sh1, p_sc1)
        hs = modulate(rmsnorm(xs, g_norm1[i]), s_sh1, s_sc1)
        if kind == 0:
            mw = (mla_w_dq[j], mla_g_q[j], mla_w_uq[j], mla_w_dkv[j], mla_g_kv[j],
                  mla_w_ukv[j], mla_g_qn[j], mla_g_kn[j], mla_w_o[j])
            op, ckv, kpe = mla_context(hp, *mw)
            os_ = mla_latent(hs, cache_ckv[:, j], cache_kpe[:, j], ang_r, ang_c, *mw)
            ckv_list.append(ckv)
            kpe_list.append(kpe)
        elif kind == 1:
            cw = (cv_w_pw1[j], cv_b_pw1[j], cv_w_dw[j], cv_b_dw[j], cv_g_ln[j], cv_b_ln[j],
                  cv_w_pw2[j], cv_b_pw2[j])
            op = conv_module(hp, *cw)
            os_ = conv_module(hs, *cw)
        else:
            sw = (ssd_w_in[j], ssd_w_conv[j], ssd_b_conv[j], ssd_dt_bias[j], ssd_a_log[j],
                  ssd_d[j], ssd_g_norm[j], ssd_w_out[j])
            h0 = jnp.zeros((xp.shape[0], 2, SSD_HEADS, SSD_HEADDIM, SSD_STATE), xp.dtype)
            op, st = ssd_mixer(hp, h0, *sw)
            os_, _ = ssd_mixer(hs, state_ssm[:, j], *sw)
            ssm_list.append(st)
        xp = xp + p_g1 * op
        xs = xs + s_g1 * os_
        hp = modulate(rmsnorm(xp, g_norm2[i]), p_sh2, p_sc2)
        hs = modulate(rmsnorm(xs, g_norm2[i]), s_sh2, s_sc2)
        xp = xp + p_g2 * swiglu(hp, ffn_w_in[i], ffn_w_out[i])
        xs = xs + s_g2 * swiglu(hs, ffn_w_in[i], ffn_w_out[i])
    new_ckv = jnp.stack(ckv_list, axis=1)
    new_kpe = jnp.stack(kpe_list, axis=1)
    new_ssm = jnp.stack(ssm_list, axis=1)
    return (xp, xs, new_ckv, new_kpe, new_ssm)
```

```python
import functools

import jax
import jax.numpy as jnp
from jax import lax
from jax.experimental import pallas as pl
from jax.experimental.pallas import tpu as pltpu

F32 = jnp.float32
BF16 = jnp.bfloat16

EPS = 1e-6
ROPE_BASE = 10000.0
GRID_W = 64
N_MIXERS = 3

LANES = 128
VMEM_LIMIT_BYTES = 56 * 1024 * 1024

MLA_HEADS = 16
QK_NOPE = 64
QK_ROPE = 32
QK_DIM = QK_NOPE + QK_ROPE
V_HEAD = 64
HEAD_PAD = LANES
CONV_WIDTH = 31
SSD_HEADDIM = 64
SSD_GROUPS = 4
SSD_STATE = 128
SSD_CONV = 5
CHUNK = 128


def _cparams(sem):
    return pltpu.CompilerParams(dimension_semantics=sem, vmem_limit_bytes=VMEM_LIMIT_BYTES)


def _silu(x):
    return x * jax.nn.sigmoid(x)


def _normmod(x, g, sh, sc):
    ms = jnp.mean(x * x, axis=-1, keepdims=True)
    return (x * lax.rsqrt(ms + EPS)) * (g * (1.0 + sc)) + sh


def _cond_index(tm, n_prompt_rows, sample_seq):
    def f(i):
        r = i * tm
        return jnp.where(r < n_prompt_rows, 0, 1 + (r - n_prompt_rows) // sample_seq)
    return f


def _mod_spec(layer, cidx, d6, ngrid):
    if ngrid == 1:
        return pl.BlockSpec((1, 1, 1, d6), lambda i: (layer, cidx(i), 0, 0))
    return pl.BlockSpec((1, 1, 1, d6), lambda i, j: (layer, cidx(i), 0, 0))


def _ada_kernel(c_ref, w_ref, b_ref, o_ref):
    s = _silu(c_ref[...]).astype(BF16)
    o_ref[0] = jnp.dot(s, w_ref[0].astype(BF16), preferred_element_type=F32) + b_ref[0]


def _ada(cond8, w_ada, b_ada):
    depth, d, n = w_ada.shape
    tn = n // 4
    return pl.pallas_call(
        _ada_kernel,
        grid=(depth, n // tn),
        in_specs=[pl.BlockSpec((8, d), lambda l, j: (0, 0)),
                  pl.BlockSpec((1, d, tn), lambda l, j: (l, 0, j)),
                  pl.BlockSpec((1, 1, tn), lambda l, j: (l, 0, j))],
        out_specs=pl.BlockSpec((1, 8, tn), lambda l, j: (l, 0, j)),
        out_shape=jax.ShapeDtypeStruct((depth, 8, n), F32),
        compiler_params=_cparams(("arbitrary", "arbitrary")),
        name="ada",
    )(cond8, w_ada, b_ada.reshape(depth, 1, n))


def _ffn_kernel(x_ref, mod_ref, g_ref, wa_ref, wu_ref, wo_ref, o_ref, h_sc, *, d):
    j = pl.program_id(1)

    @pl.when(j == 0)
    def _():
        sh = mod_ref[0, 0, :, 3 * d:4 * d]
        sc = mod_ref[0, 0, :, 4 * d:5 * d]
        h_sc[...] = _normmod(x_ref[...], g_ref[0], sh, sc).astype(BF16)
        o_ref[...] = jnp.zeros_like(o_ref)

    h = h_sc[...]
    a = jnp.dot(h, wa_ref[0].astype(BF16), preferred_element_type=F32)
    u = jnp.dot(h, wu_ref[0].astype(BF16), preferred_element_type=F32)
    t = (_silu(a) * u).astype(BF16)
    o_ref[...] += jnp.dot(t, wo_ref[0].astype(BF16), preferred_element_type=F32)

    @pl.when(j == pl.num_programs(1) - 1)
    def _():
        gate = mod_ref[0, 0, :, 5 * d:6 * d]
        o_ref[...] = x_ref[...] + gate * o_ref[...]


def _ffn(x, mod4, g2, w_in, w_out, layer, cidx, tm, th):
    m, d = x.shape
    hid = w_out.shape[1]
    nj = hid // th
    return pl.pallas_call(
        functools.partial(_ffn_kernel, d=d),
        grid=(m // tm, nj),
        in_specs=[pl.BlockSpec((tm, d), lambda i, j: (i, 0)),
                  _mod_spec(layer, cidx, 6 * d, 2),
                  pl.BlockSpec((1, 1, d), lambda i, j: (layer, 0, 0)),
                  pl.BlockSpec((1, d, th), lambda i, j: (layer, 0, j)),
                  pl.BlockSpec((1, d, th), lambda i, j: (layer, 0, j + nj)),
                  pl.BlockSpec((1, th, d), lambda i, j: (layer, j, 0))],
        out_specs=pl.BlockSpec((tm, d), lambda i, j: (i, 0)),
        out_shape=jax.ShapeDtypeStruct((m, d), F32),
        scratch_shapes=[pltpu.VMEM((tm, d), BF16)],
        compiler_params=_cparams(("arbitrary", "arbitrary")),
        name="ffn",
    )(x, mod4, g2.reshape(g2.shape[0], 1, d), w_in, w_in, w_out)


def _proj_resid_kernel(x_ref, y_ref, mod_ref, w_ref, b_ref, o_ref, w_sc, *, d):
    @pl.when(pl.program_id(0) == 0)
    def _():
        w_sc[...] = w_ref[0].astype(BF16)

    gate = mod_ref[0, 0, :, 2 * d:3 * d]
    r = jnp.dot(y_ref[...], w_sc[...], preferred_element_type=F32) + b_ref[0]
    o_ref[...] = x_ref[...] + gate * r


def _proj_resid(x, y, mod4, w, b, layer, widx, cidx, tm):
    m, d = x.shape
    k = y.shape[1]
    return pl.pallas_call(
        functools.partial(_proj_resid_kernel, d=d),
        grid=(m // tm,),
        in_specs=[pl.BlockSpec((tm, d), lambda i: (i, 0)),
                  pl.BlockSpec((tm, k), lambda i: (i, 0)),
                  _mod_spec(layer, cidx, 6 * d, 1),
                  pl.BlockSpec((1, k, d), lambda i: (widx, 0, 0)),
                  pl.BlockSpec((1, 1, d), lambda i: (widx, 0, 0))],
        out_specs=pl.BlockSpec((tm, d), lambda i: (i, 0)),
        out_shape=jax.ShapeDtypeStruct((m, d), F32),
        scratch_shapes=[pltpu.VMEM((k, d), BF16)],
        compiler_params=_cparams(("arbitrary",)),
        name="proj_resid",
    )(x, y, mod4, w, b.reshape(b.shape[0], 1, d))


def _conv_a_kernel(x_ref, mod_ref, g_ref, w_ref, b_ref, o_ref, w_sc, *, d):
    @pl.when(pl.program_id(0) == 0)
    def _():
        w_sc[...] = w_ref[0].astype(BF16)

    h = _normmod(x_ref[...], g_ref[0], mod_ref[0, 0, :, 0:d], mod_ref[0, 0, :, d:2 * d]).astype(BF16)
    y = jnp.dot(h, w_sc[...], preferred_element_type=F32) + b_ref[0]
    u = y[:, :d] * jax.nn.sigmoid(y[:, d:])
    for cb in range(d // LANES):
        o_ref[cb] = u[:, cb * LANES:(cb + 1) * LANES]


def _conv_b_kernel(x_ref, u_ref, mod_ref, wdw_ref, bdw_ref, gln_ref, bln_ref, w2_ref, b2_ref, o_ref,
                   upad, conv_sc, y_sc, w_sc, *, d, t, rc, top):
    nb = d // LANES
    half = (CONV_WIDTH - 1) // 2

    @pl.when(pl.program_id(0) == 0)
    def _():
        w_sc[...] = w2_ref[0].astype(BF16)
        upad[:, 0:top, :] = jnp.zeros((nb, top, LANES), F32)
        upad[:, top + t:2 * top + t, :] = jnp.zeros((nb, top, LANES), F32)

    upad[:, top:top + t, :] = u_ref[...]

    def conv_block(cb, carry):
        for r0 in range(0, t, rc):
            acc = jnp.broadcast_to(bdw_ref[cb], (rc, LANES))
            for k in range(CONV_WIDTH):
                acc = acc + wdw_ref[cb, pl.ds(k, 1), :] * upad[cb, pl.ds(r0 + top - half + k, rc), :]
            conv_sc[cb, pl.ds(r0, rc), :] = acc
        return carry

    lax.fori_loop(0, nb, conv_block, 0)

    rl = min(t, 256)
    for r0 in range(0, t, rl):
        s = jnp.zeros((rl, 1), F32)
        for cb in range(nb):
            s = s + jnp.sum(conv_sc[cb, r0:r0 + rl, :], axis=-1, keepdims=True)
        mean = s * (1.0 / d)
        v = jnp.zeros((rl, 1), F32)
        for cb in range(nb):
            c = conv_sc[cb, r0:r0 + rl, :] - mean
            v = v + jnp.sum(c * c, axis=-1, keepdims=True)
        rstd = lax.rsqrt(v * (1.0 / d) + EPS)
        for cb in range(nb):
            sl = slice(cb * LANES, (cb + 1) * LANES)
            y = (conv_sc[cb, r0:r0 + rl, :] - mean) * rstd * gln_ref[:, sl] + bln_ref[:, sl]
            y_sc[r0:r0 + rl, sl] = _silu(y).astype(BF16)

    gate = mod_ref[0, 0, :, 2 * d:3 * d]
    r = jnp.dot(y_sc[...], w_sc[...], preferred_element_type=F32) + b2_ref[0]
    o_ref[...] = x_ref[...] + gate * r


def _conv_b(x, u, mod4, wdw, bdw, gln, bln, w2, b2, layer, j, t, row0, nseq, cond0, cond_step):
    m, d = x.shape
    nb = d // LANES
    top = 16
    blk0 = row0 // t
    return pl.pallas_call(
        functools.partial(_conv_b_kernel, d=d, t=t, rc=64, top=top),
        grid=(nseq,),
        in_specs=[pl.BlockSpec((t, d), lambda i: (blk0 + i, 0)),
                  pl.BlockSpec((nb, t, LANES), lambda i: (0, blk0 + i, 0)),
                  pl.BlockSpec((1, 1, 1, 6 * d), lambda i: (layer, cond0 + cond_step * i, 0, 0)),
                  pl.BlockSpec((nb, 32, LANES), lambda i: (0, 0, 0)),
                  pl.BlockSpec((nb, 1, LANES), lambda i: (0, 0, 0)),
                  pl.BlockSpec((1, d), lambda i: (0, 0)),
                  pl.BlockSpec((1, d), lambda i: (0, 0)),
                  pl.BlockSpec((1, d, d), lambda i: (j, 0, 0)),
                  pl.BlockSpec((1, 1, d), lambda i: (j, 0, 0))],
        out_specs=pl.BlockSpec((t, d), lambda i: (i, 0)),
        out_shape=jax.ShapeDtypeStruct((nseq * t, d), F32),
        scratch_shapes=[pltpu.VMEM((nb, t + 2 * top, LANES), F32),
                        pltpu.VMEM((nb, t, LANES), F32),
                        pltpu.VMEM((t, d), BF16),
                        pltpu.VMEM((d, d), BF16)],
        compiler_params=_cparams(("arbitrary",)),
        name="conv_b",
    )(x, u, mod4, wdw, bdw, gln, bln, w2, b2)


def _conv_layer(x, mod4, P, layer, j, npr, seq_p, seq_s):
    m, d = x.shape
    nb = d // LANES
    tm = 256
    cidx = _cond_index(tm, npr, seq_s)
    u = pl.pallas_call(
        functools.partial(_conv_a_kernel, d=d),
        grid=(m // tm,),
        in_specs=[pl.BlockSpec((tm, d), lambda i: (i, 0)),
                  _mod_spec(layer, cidx, 6 * d, 1),
                  pl.BlockSpec((1, 1, d), lambda i: (layer, 0, 0)),
                  pl.BlockSpec((1, d, 2 * d), lambda i: (j, 0, 0)),
                  pl.BlockSpec((1, 1, 2 * d), lambda i: (j, 0, 0))],
        out_specs=pl.BlockSpec((nb, tm, LANES), lambda i: (0, i, 0)),
        out_shape=jax.ShapeDtypeStruct((nb, m, LANES), F32),
        scratch_shapes=[pltpu.VMEM((d, 2 * d), BF16)],
        compiler_params=_cparams(("arbitrary",)),
        name="conv_a",
    )(x, mod4, P["g_norm1"].reshape(-1, 1, d), P["cv_w_pw1"], P["cv_b_pw1"].reshape(-1, 1, 2 * d))
    wdw = jnp.pad(P["cv_w_dw"][j], ((0, 32 - CONV_WIDTH), (0, 0))).reshape(32, nb, LANES).transpose(1, 0, 2)
    bdw = P["cv_b_dw"][j].reshape(nb, 1, LANES)
    gln = P["cv_g_ln"][j].reshape(1, d)
    bln = P["cv_b_ln"][j].reshape(1, d)
    b2 = P["cv_b_pw2"].reshape(-1, 1, d)
    args = (x, u, mod4, wdw, bdw, gln, bln, P["cv_w_pw2"], b2, layer, j)
    op = _conv_b(*args, seq_p, 0, npr // seq_p, 0, 0)
    os_ = _conv_b(*args, seq_s, npr, (m - npr) // seq_s, 1, 1)
    return jnp.concatenate([op, os_], axis=0)


def _rope_tables(seq):
    rows = seq // GRID_W
    row = jnp.repeat(jnp.arange(rows, dtype=F32), GRID_W)
    col = jnp.tile(jnp.arange(GRID_W, dtype=F32), rows)
    axis = QK_ROPE // 2
    inv = ROPE_BASE ** (-jnp.arange(0, axis, 2, dtype=F32) / axis)
    ar, ac = row[:, None] * inv, col[:, None] * inv
    cr, sr, cc, sc = jnp.cos(ar), jnp.sin(ar), jnp.cos(ac), jnp.sin(ac)
    one = jnp.ones((seq, QK_NOPE), F32)
    z8 = jnp.zeros((seq, axis // 2), F32)
    zn = jnp.zeros((seq, QK_NOPE), F32)
    zp = jnp.zeros((seq, HEAD_PAD - QK_DIM), F32)
    c = jnp.concatenate([one, cr, cr, cc, cc, zp + 1.0], axis=1)
    sa = jnp.concatenate([zn, -sr, z8, -sc, z8, zp], axis=1)
    sb = jnp.concatenate([zn, z8, sr, z8, sc, zp], axis=1)
    return c, sa, sb


def _head_norm(xh, gain):
    ms = jnp.sum(xh * xh, axis=-1, keepdims=True) * (1.0 / QK_DIM)
    return xh * lax.rsqrt(ms + EPS) * gain


def _rope(xh, c, sa, sb):
    return xh * c + pltpu.roll(xh, HEAD_PAD - 8, 1) * sa + pltpu.roll(xh, 8, 1) * sb


def _expand_kv(ckv_b, kpe, wuk_ref, wv_ref, gk, rope_tabs, k_ref, v_ref):
    kk = jnp.dot(ckv_b, wuk_ref[...], preferred_element_type=F32)
    v_ref[...] = jnp.dot(ckv_b, wv_ref[...], preferred_element_type=F32).astype(BF16)
    for h in range(MLA_HEADS):
        sl = slice(h * HEAD_PAD, (h + 1) * HEAD_PAD)
        kh = _head_norm(kk[:, sl] + kpe, gk)
        if rope_tabs is not None:
            kh = _rope(kh, *rope_tabs)
        k_ref[:, sl] = kh.astype(BF16)


def _mla_proj_kernel(*refs, d, rope):
    (x_ref, mod_ref, g_ref, wcat_ref, gq_ref, wuq_ref, gkv_ref, wuk_ref, wv_ref, gqn_ref, gkn_ref) = refs[:11]
    if rope:
        c_ref, sa_ref, sb_ref = refs[11:14]
        q_ref, k_ref, v_ref = refs[14:17]
        tabs = (c_ref[...], sa_ref[...], sb_ref[...])
    else:
        q_ref, k_ref, v_ref, ckv_ref, kpe_ref = refs[11:16]
        tabs = None
    h = _normmod(x_ref[...], g_ref[0], mod_ref[0, 0, :, 0:d], mod_ref[0, 0, :, d:2 * d]).astype(BF16)
    t = jnp.dot(h, wcat_ref[...], preferred_element_type=F32)
    nq = gq_ref.shape[1]
    nkv = gkv_ref.shape[1]
    qd, craw, kpe = t[:, :nq], t[:, nq:nq + nkv], t[:, nq + nkv:]
    cq = (qd * lax.rsqrt(jnp.mean(qd * qd, axis=-1, keepdims=True) + EPS) * gq_ref[...]).astype(BF16)
    ckv = craw * lax.rsqrt(jnp.mean(craw * craw, axis=-1, keepdims=True) + EPS) * gkv_ref[...]
    if not rope:
        ckv_ref[...] = ckv
        kpe_ref[...] = kpe
    q = jnp.dot(cq, wuq_ref[...], preferred_element_type=F32)
    gqn = gqn_ref[...]
    for hd in range(MLA_HEADS):
        sl = slice(hd * HEAD_PAD, (hd + 1) * HEAD_PAD)
        qh = _head_norm(q[:, sl], gqn)
        if rope:
            qh = _rope(qh, *tabs)
        q_ref[:, sl] = qh.astype(BF16)
    _expand_kv(ckv.astype(BF16), kpe, wuk_ref, wv_ref, gkn_ref[...], tabs, k_ref, v_ref)


def _kv_ctx_kernel(ckv_ref, kpe_ref, wuk_ref, wv_ref, gkn_ref, k_ref, v_ref):
    _expand_kv(ckv_ref[...].astype(BF16), kpe_ref[...], wuk_ref, wv_ref, gkn_ref[...], None, k_ref, v_ref)


def _softmax_pv(parts, l_shape):
    m = parts[0][0].max(axis=-1, keepdims=True)
    for s, _ in parts[1:]:
        m = jnp.maximum(m, s.max(axis=-1, keepdims=True))
    l = jnp.zeros(l_shape, F32)
    o = None
    for s, v in parts:
        e = jnp.exp(s - m)
        l = l + jnp.sum(e, axis=-1, keepdims=True)
        pv = jnp.dot(e.astype(BF16), v, preferred_element_type=F32)
        o = pv if o is None else o + pv
    return o * (1.0 / l)


_NT = (((1,), (1,)), ((), ()))


def _attn_prompt_kernel(q_ref, k_ref, v_ref, o_ref):
    t = q_ref.shape[0]
    lane = lax.broadcasted_iota(jnp.int32, (t, LANES), 1)
    for hp in range(MLA_HEADS // 2):
        vp = v_ref[:, hp * LANES:(hp + 1) * LANES]
        outs = []
        for hh in range(2):
            sl = slice((2 * hp + hh) * HEAD_PAD, (2 * hp + hh + 1) * HEAD_PAD)
            s = lax.dot_general(q_ref[:, sl], k_ref[:, sl], _NT, preferred_element_type=F32)
            outs.append(_softmax_pv([(s, vp)], (t, 1)))
        o_ref[:, hp * LANES:(hp + 1) * LANES] = jnp.where(lane < V_HEAD, outs[0], outs[1]).astype(BF16)


def _attn_sample_kernel(q_ref, kc_ref, vc_ref, kl_ref, vl_ref, o_ref):
    t = q_ref.shape[0]
    lane = lax.broadcasted_iota(jnp.int32, (t, LANES), 1)
    outs = []
    for hh in range(2):
        sl = slice(hh * HEAD_PAD, (hh + 1) * HEAD_PAD)
        qh = q_ref[:, sl]
        sc = lax.dot_general(qh, kc_ref[:, sl], _NT, preferred_element_type=F32)
        sl_ = lax.dot_general(qh, kl_ref[:, sl], _NT, preferred_element_type=F32)
        outs.append(_softmax_pv([(sc, vc_ref[...]), (sl_, vl_ref[...])], (t, 1)))
    o_ref[...] = jnp.where(lane < V_HEAD, outs[0], outs[1]).astype(BF16)


def _mla_weights(P, j):
    d = P["mla_w_dq"].shape[1]
    nq = P["mla_w_dq"].shape[2]
    nkv = P["mla_g_kv"].shape[1]
    pad = HEAD_PAD - QK_DIM
    w_dkv = P["mla_w_dkv"][j]
    w_kpe = jnp.pad(w_dkv[:, nkv:], ((0, 0), (QK_NOPE, pad)))
    wcat = jnp.concatenate([P["mla_w_dq"][j], w_dkv[:, :nkv], w_kpe], axis=1).astype(BF16)
    wuq = jnp.pad(P["mla_w_uq"][j].reshape(nq, MLA_HEADS, QK_DIM), ((0, 0), (0, 0), (0, pad)))
    wuq = wuq.reshape(nq, MLA_HEADS * HEAD_PAD).astype(BF16)
    wukv = P["mla_w_ukv"][j].reshape(nkv, MLA_HEADS, QK_NOPE + V_HEAD)
    wuk = jnp.pad(wukv[:, :, :QK_NOPE], ((0, 0), (0, 0), (0, HEAD_PAD - QK_NOPE)))
    wuk = wuk.reshape(nkv, MLA_HEADS * HEAD_PAD).astype(BF16)
    wv = wukv[:, :, QK_NOPE:].reshape(nkv, MLA_HEADS * V_HEAD).astype(BF16)
    gq = P["mla_g_q"][j].reshape(1, nq)
    gkv = P["mla_g_kv"][j].reshape(1, nkv)
    gqn = (jnp.pad(P["mla_g_qn"][j], (0, pad)) * (QK_DIM ** -0.5)).reshape(1, HEAD_PAD)
    gkn = jnp.pad(P["mla_g_kn"][j], (0, pad)).reshape(1, HEAD_PAD)
    return wcat, gq, wuq, gkv, wuk, wv, gqn, gkn


def _full(shape):
    return pl.BlockSpec(shape, lambda *_: (0,) * len(shape))


def _mla_proj(x, mod4, g1, W, layer, row0, nrows, cidx, tm, tabs):
    m, d = x.shape
    wcat, gq, wuq, gkv, wuk, wv, gqn, gkn = W
    blk0 = row0 // tm
    hq = MLA_HEADS * HEAD_PAD
    hv = MLA_HEADS * V_HEAD
    nkv = gkv.shape[1]
    rope = tabs is not None
    in_specs = [pl.BlockSpec((tm, d), lambda i: (blk0 + i, 0)),
                pl.BlockSpec((1, 1, 1, 6 * d), lambda i: (layer, cidx(blk0 + i), 0, 0)),
                pl.BlockSpec((1, 1, d), lambda i: (layer, 0, 0)),
                _full(wcat.shape), _full(gq.shape), _full(wuq.shape), _full(gkv.shape), _full(wuk.shape),
                _full(wv.shape), _full(gqn.shape), _full(gkn.shape)]
    args = [x, mod4, g1, wcat, gq, wuq, gkv, wuk, wv, gqn, gkn]
    out_specs = [pl.BlockSpec((tm, hq), lambda i: (i, 0)),
                 pl.BlockSpec((tm, hq), lambda i: (i, 0)),
                 pl.BlockSpec((tm, hv), lambda i: (i, 0))]
    out_shape = [jax.ShapeDtypeStruct((nrows, hq), BF16),
                 jax.ShapeDtypeStruct((nrows, hq), BF16),
                 jax.ShapeDtypeStruct((nrows, hv), BF16)]
    if rope:
        seq = tabs[0].shape[0]
        per = seq // tm
        in_specs += [pl.BlockSpec((tm, HEAD_PAD), lambda i: (i % per, 0))] * 3
        args += list(tabs)
    else:
        out_specs += [pl.BlockSpec((tm, nkv), lambda i: (i, 0)), pl.BlockSpec((tm, HEAD_PAD), lambda i: (i, 0))]
        out_shape += [jax.ShapeDtypeStruct((nrows, nkv), F32), jax.ShapeDtypeStruct((nrows, HEAD_PAD), F32)]
    return pl.pallas_call(
        functools.partial(_mla_proj_kernel, d=d, rope=rope),
        grid=(nrows // tm,),
        in_specs=in_specs, out_specs=out_specs, out_shape=out_shape,
        compiler_params=_cparams(("arbitrary",)),
        name="mla_proj_rope" if rope else "mla_proj",
    )(*args)


def _mla_layer(x, mod4, P, layer, j, cache_ckv, cache_kpe, npr, seq_p, seq_s):
    m, d = x.shape
    nbp, nbs = npr // seq_p, (m - npr) // seq_s
    past = cache_ckv.shape[1]
    W = _mla_weights(P, j)
    wuk, wv, gkn = W[4], W[5], W[7]
    g1 = P["g_norm1"].reshape(-1, 1, d)
    hq = MLA_HEADS * HEAD_PAD
    hv = MLA_HEADS * V_HEAD
    tm = 256
    cidx = _cond_index(tm, npr, seq_s)
    qp, kp, vp, ckv, kpe128 = _mla_proj(x, mod4, g1, W, layer, 0, npr, cidx, tm, None)
    qs, ks, vs = _mla_proj(x, mod4, g1, W, layer, npr, m - npr, cidx, tm, _rope_tables(seq_s))
    cc = cache_ckv.reshape(nbs * past, -1)
    ck = jnp.pad(cache_kpe.reshape(nbs * past, -1), ((0, 0), (QK_NOPE, HEAD_PAD - QK_DIM)))
    kc, vc = pl.pallas_call(
        _kv_ctx_kernel,
        grid=(nbs,),
        in_specs=[pl.BlockSpec((past, cc.shape[1]), lambda i: (i, 0)),
                  pl.BlockSpec((past, HEAD_PAD), lambda i: (i, 0)),
                  _full(wuk.shape), _full(wv.shape), _full(gkn.shape)],
        out_specs=[pl.BlockSpec((past, hq), lambda i: (i, 0)), pl.BlockSpec((past, hv), lambda i: (i, 0))],
        out_shape=[jax.ShapeDtypeStruct((nbs * past, hq), BF16), jax.ShapeDtypeStruct((nbs * past, hv), BF16)],
        compiler_params=_cparams(("arbitrary",)),
        name="mla_kv_ctx",
    )(cc, ck, wuk, wv, gkn)
    op = pl.pallas_call(
        _attn_prompt_kernel,
        grid=(nbp,),
        in_specs=[pl.BlockSpec((seq_p, hq), lambda b: (b, 0)),
                  pl.BlockSpec((seq_p, hq), lambda b: (b, 0)),
                  pl.BlockSpec((seq_p, hv), lambda b: (b, 0))],
        out_specs=pl.BlockSpec((seq_p, hv), lambda b: (b, 0)),
        out_shape=jax.ShapeDtypeStruct((npr, hv), BF16),
        compiler_params=_cparams(("arbitrary",)),
        name="attn_prompt",
    )(qp, kp, vp)
    tq = min(seq_s, 512)
    nqt = seq_s // tq
    pair = 2 * HEAD_PAD
    os_ = pl.pallas_call(
        _attn_sample_kernel,
        grid=(nbs, nqt, MLA_HEADS // 2),
        in_specs=[pl.BlockSpec((tq, pair), lambda b, qi, hp: (b * nqt + qi, hp)),
                  pl.BlockSpec((past, pair), lambda b, qi, hp: (b, hp)),
                  pl.BlockSpec((past, LANES), lambda b, qi, hp: (b, hp)),
                  pl.BlockSpec((seq_s, pair), lambda b, qi, hp: (b, hp)),
                  pl.BlockSpec((seq_s, LANES), lambda b, qi, hp: (b, hp))],
        out_specs=pl.BlockSpec((tq, LANES), lambda b, qi, hp: (b * nqt + qi, hp)),
        out_shape=jax.ShapeDtypeStruct((m - npr, hv), BF16),
        compiler_params=_cparams(("arbitrary", "arbitrary", "arbitrary")),
        name="attn_sample",
    )(qs, kc, vc, ks, vs)
    o = jnp.concatenate([op, os_], axis=0)
    zb = jnp.zeros((P["mla_w_o"].shape[0], d), F32)
    tmr = min(512, seq_s)
    xn = _proj_resid(x, o, mod4, P["mla_w_o"], zb, layer, j, _cond_index(tmr, npr, seq_s), tmr)
    new_ckv = ckv.reshape(nbp, seq_p, -1)
    new_kpe = kpe128[:, QK_NOPE:QK_DIM].reshape(nbp, seq_p, QK_ROPE)
    return xn, new_ckv, new_kpe


SSD_R = 8
SSD_GW = SSD_R * SSD_HEADDIM
SSD_NX = SSD_GW // LANES
SSD_DTW = 2 * SSD_R
SSD_TOP = 8


def _ssd_in_kernel(x_ref, mod_ref, g_ref, w_ref, wdt_ref, zx_ref, dt_ref, h_sc, *, d):
    @pl.when(pl.program_id(1) == 0)
    def _():
        h = _normmod(x_ref[...], g_ref[0], mod_ref[0, 0, :, 0:d], mod_ref[0, 0, :, d:2 * d]).astype(BF16)
        h_sc[...] = h
        dt = jnp.dot(h, wdt_ref[...], preferred_element_type=F32)
        for g in range(SSD_GROUPS):
            dt_ref[g] = dt[:, g * SSD_DTW:(g + 1) * SSD_DTW]

    r = jnp.dot(h_sc[...], w_ref[0].astype(BF16), preferred_element_type=F32)
    for c in range(r.shape[1] // LANES):
        zx_ref[c] = r[:, c * LANES:(c + 1) * LANES]


def _split3(x):
    hi = x.astype(BF16)
    r1 = x - hi.astype(F32)
    mid = r1.astype(BF16)
    lo = (r1 - mid.astype(F32)).astype(BF16)
    return hi, mid, lo


def _dot3_left(tri, x):
    return sum(jnp.dot(tri, p, preferred_element_type=F32) for p in _split3(x))


def _dot3_right(x, tri):
    return sum(jnp.dot(p, tri, preferred_element_type=F32) for p in _split3(x))


def _pair(lane, col_a, col_b):
    return jnp.where(lane < SSD_HEADDIM, col_a, col_b)


def _ssd_core_kernel(*refs, t, has_h0):
    (z_ref, x_ref, b_ref, c_ref, dt_ref, wx_ref, wb_ref, wc_ref, bx_ref, bb_ref, bc_ref,
     dtb_ref, alog_ref, dsum_ref, gn_ref) = refs[:15]
    if has_h0:
        h0_ref, y_ref = refs[15:17]
        st_ref = None
        scr = refs[17:]
    else:
        y_ref, st_ref = refs[15:17]
        h0_ref = None
        scr = refs[17:]
    (pad_sc, xc_sc, bcv_sc, ccv_sc, bt_sc, cb_sc, y_sc, dt_sc, da_sc, dtt_sc, dat_sc, tmp_sc, s_sc) = scr
    nc = t // CHUNK
    half = (SSD_CONV - 1) // 2
    nblk = SSD_NX + 2

    @pl.when((pl.program_id(0) == 0) & (pl.program_id(1) == 0))
    def _():
        pad_sc[:, 0:SSD_TOP, :] = jnp.zeros((nblk, SSD_TOP, LANES), F32)
        pad_sc[:, SSD_TOP + t:2 * SSD_TOP + t, :] = jnp.zeros((nblk, SSD_TOP, LANES), F32)
        tmp_sc[...] = jnp.zeros_like(tmp_sc)

    pad_sc[0:SSD_NX, SSD_TOP:SSD_TOP + t, :] = x_ref[...]
    pad_sc[SSD_NX, SSD_TOP:SSD_TOP + t, :] = b_ref[0]
    pad_sc[SSD_NX + 1, SSD_TOP:SSD_TOP + t, :] = c_ref[0]

    for blk in range(nblk):
        if blk < SSD_NX:
            w_r, bias = wx_ref, bx_ref[blk]
            widx = blk
        elif blk == SSD_NX:
            w_r, bias, widx = wb_ref, bb_ref[0], 0
        else:
            w_r, bias, widx = wc_ref, bc_ref[0], 0
        for c in range(nc):
            r0 = c * CHUNK
            acc = jnp.broadcast_to(bias, (CHUNK, LANES))
            for k in range(SSD_CONV):
                acc = acc + w_r[widx, k:k + 1, :] * pad_sc[blk, r0 + SSD_TOP - half + k:r0 + SSD_TOP - half + k + CHUNK, :]
            acc = _silu(acc)
            if blk < SSD_NX:
                xc_sc[blk, r0:r0 + CHUNK, :] = acc
            elif blk == SSD_NX:
                bcv_sc[r0:r0 + CHUNK, :] = acc.astype(BF16)
                bt_sc[c] = acc.T.astype(BF16)
            else:
                ccv_sc[r0:r0 + CHUNK, :] = acc.astype(BF16)

    v = dt_ref[0] + dtb_ref[0]
    dt = jnp.maximum(v, 0.0) + jnp.log1p(jnp.exp(-jnp.abs(v)))
    da = dt * (-jnp.exp(alog_ref[0]))
    dt_sc[...] = dt
    da_sc[...] = da
    for c in range(nc):
        r0 = c * CHUNK
        tmp_sc[:, 0:SSD_DTW] = dt[r0:r0 + CHUNK, :]
        dtt_sc[c] = tmp_sc[...].T[0:SSD_DTW, :]
        tmp_sc[:, 0:SSD_DTW] = da[r0:r0 + CHUNK, :]
        dat_sc[c] = tmp_sc[...].T[0:SSD_DTW, :]

    row = lax.broadcasted_iota(jnp.int32, (CHUNK, CHUNK), 0)
    colm = lax.broadcasted_iota(jnp.int32, (CHUNK, CHUNK), 1)
    lower = colm <= row
    upper = colm >= row
    tri_l = jnp.where(lower, 1.0, 0.0).astype(BF16)
    tri_u = jnp.where(upper, 1.0, 0.0).astype(BF16)
    lane = lax.broadcasted_iota(jnp.int32, (CHUNK, LANES), 1)
    lane1 = lax.broadcasted_iota(jnp.int32, (1, LANES), 1)

    for dr in range(2):
        lo = dr * SSD_R
        mask = upper if dr else lower
        tri = tri_u if dr else tri_l
        tri_t = tri_l if dr else tri_u
        if has_h0:
            s_sc[...] = h0_ref[0, dr, 0].T
        else:
            s_sc[...] = jnp.zeros_like(s_sc)

        def chunk(it, carry, dr=dr, lo=lo, mask=mask, tri=tri, tri_t=tri_t):
            c = (nc - 1 - it) if dr else it
            r0 = pl.multiple_of(c * CHUNK, CHUNK)
            rows = pl.ds(r0, CHUNK)
            cum = _dot3_left(tri, da_sc[rows, :])
            cum_t = _dot3_right(dat_sc[c], tri_t)
            dtt = dtt_sc[c]
            edge = cum[0:1, :] if dr else cum[CHUNK - 1:CHUNK, :]
            w2 = dt_sc[rows, :] * jnp.exp(edge - cum)
            ecum = jnp.exp(cum)
            eedge = jnp.exp(edge)
            bmat = bcv_sc[rows, :]
            cmat = ccv_sc[rows, :]
            if dr == 0:
                cb = lax.dot_general(cmat, bmat, _NT, preferred_element_type=F32)
                cb_sc[c] = cb
            else:
                cb = cb_sc[c]
            s_prev = s_sc[...]
            yoff = jnp.dot(cmat, s_prev.astype(BF16), preferred_element_type=F32)
            xd = []
            for q in range(SSD_NX):
                ha, hb = lo + 2 * q, lo + 2 * q + 1
                xblk = xc_sc[q, rows, :]
                xb16 = xblk.astype(BF16)
                yd = []
                for hh in (ha, hb):
                    seg = cum[:, hh:hh + 1] - cum_t[hh:hh + 1, :]
                    lm = jnp.exp(jnp.where(mask, seg, -jnp.inf))
                    mh = (cb * lm * dtt[hh:hh + 1, :]).astype(BF16)
                    yd.append(jnp.dot(mh, xb16, preferred_element_type=F32))
                y = _pair(lane, yd[0], yd[1])
                y = y + yoff[:, q * LANES:(q + 1) * LANES] * _pair(lane, ecum[:, ha:ha + 1], ecum[:, hb:hb + 1])
                if dr == 0:
                    y_sc[q, rows, :] = y + xblk * dsum_ref[0, :, q * LANES:(q + 1) * LANES]
                else:
                    y_sc[q, rows, :] = y_sc[q, rows, :] + y
                xd.append((xblk * _pair(lane, w2[:, ha:ha + 1], w2[:, hb:hb + 1])).astype(BF16))
            st = jnp.dot(bt_sc[c], jnp.concatenate(xd, axis=1), preferred_element_type=F32)
            cd = jnp.concatenate(
                [_pair(lane1, eedge[:, lo + 2 * q:lo + 2 * q + 1], eedge[:, lo + 2 * q + 1:lo + 2 * q + 2])
                 for q in range(SSD_NX)], axis=1)
            s_sc[...] = s_prev * cd + st
            return carry

        lax.fori_loop(0, nc, chunk, 0)
        if st_ref is not None:
            st_ref[0, dr, 0] = s_sc[...].T

    rl = min(t, 256)
    for r0 in range(0, t, rl):
        yz = [y_sc[q, r0:r0 + rl, :] * _silu(z_ref[q, r0:r0 + rl, :]) for q in range(SSD_NX)]
        ms = sum(jnp.sum(a * a, axis=-1, keepdims=True) for a in yz) * (1.0 / SSD_GW)
        rstd = lax.rsqrt(ms + EPS)
        for q in range(SSD_NX):
            sl = slice(q * LANES, (q + 1) * LANES)
            y_ref[r0:r0 + rl, sl] = (yz[q] * rstd * gn_ref[0, :, sl]).astype(BF16)


def _ssd_core(zx, dtc, cw, cbias, dtb, alog, dsum, gn, h0, t, row0, nseq):
    has_h0 = h0 is not None
    b0 = row0 // t
    nc = t // CHUNK
    nz = SSD_GROUPS * SSD_NX
    in_specs = [pl.BlockSpec((SSD_NX, t, LANES), lambda b, g: (g, b0 + b, 0)),
                pl.BlockSpec((SSD_NX, t, LANES), lambda b, g: (SSD_GROUPS + g, b0 + b, 0)),
                pl.BlockSpec((1, t, LANES), lambda b, g: (2 * nz + g, b0 + b, 0)),
                pl.BlockSpec((1, t, LANES), lambda b, g: (2 * nz + SSD_GROUPS + g, b0 + b, 0)),
                pl.BlockSpec((1, t, SSD_DTW), lambda b, g: (g, b0 + b, 0)),
                pl.BlockSpec((SSD_NX, 8, LANES), lambda b, g: (g, 0, 0)),
                pl.BlockSpec((1, 8, LANES), lambda b, g: (nz + g, 0, 0)),
                pl.BlockSpec((1, 8, LANES), lambda b, g: (nz + SSD_GROUPS + g, 0, 0)),
                pl.BlockSpec((SSD_NX, 1, LANES), lambda b, g: (g, 0, 0)),
                pl.BlockSpec((1, 1, LANES), lambda b, g: (nz + g, 0, 0)),
                pl.BlockSpec((1, 1, LANES), lambda b, g: (nz + SSD_GROUPS + g, 0, 0)),
                pl.BlockSpec((1, 1, SSD_DTW), lambda b, g: (g, 0, 0)),
                pl.BlockSpec((1, 1, SSD_DTW), lambda b, g: (g, 0, 0)),
                pl.BlockSpec((1, 1, SSD_GW), lambda b, g: (g, 0, 0)),
                pl.BlockSpec((1, 1, SSD_GW), lambda b, g: (g, 0, 0))]
    args = [zx, zx, zx, zx, dtc, cw, cw, cw, cbias, cbias, cbias, dtb, alog, dsum, gn]
    st_spec = pl.BlockSpec((1, 2, 1, SSD_GW, SSD_STATE), lambda b, g: (b, 0, g, 0, 0))
    y_spec = pl.BlockSpec((t, SSD_GW), lambda b, g: (b, g))
    y_shape = jax.ShapeDtypeStruct((nseq * t, SSD_GROUPS * SSD_GW), BF16)
    if has_h0:
        in_specs.append(st_spec)
        args.append(h0)
        out_specs, out_shape = y_spec, y_shape
    else:
        out_specs = [y_spec, st_spec]
        out_shape = [y_shape, jax.ShapeDtypeStruct((nseq, 2, SSD_GROUPS, SSD_GW, SSD_STATE), F32)]
    scratch = [pltpu.VMEM((SSD_NX + 2, t + 2 * SSD_TOP, LANES), F32),
               pltpu.VMEM((SSD_NX, t, LANES), F32),
               pltpu.VMEM((t, LANES), BF16),
               pltpu.VMEM((t, LANES), BF16),
               pltpu.VMEM((nc, CHUNK, CHUNK), BF16),
               pltpu.VMEM((nc, CHUNK, CHUNK), F32),
               pltpu.VMEM((SSD_NX, t, LANES), F32),
               pltpu.VMEM((t, SSD_DTW), F32),
               pltpu.VMEM((t, SSD_DTW), F32),
               pltpu.VMEM((nc, SSD_DTW, CHUNK), F32),
               pltpu.VMEM((nc, SSD_DTW, CHUNK), F32),
               pltpu.VMEM((CHUNK, LANES), F32),
               pltpu.VMEM((SSD_STATE, SSD_GW), F32)]
    return pl.pallas_call(
        functools.partial(_ssd_core_kernel, t=t, has_h0=has_h0),
        grid=(nseq, SSD_GROUPS),
        in_specs=in_specs, out_specs=out_specs, out_shape=out_shape,
        scratch_shapes=scratch,
        compiler_params=_cparams(("arbitrary", "arbitrary")),
        name="ssd_core_h0" if has_h0 else "ssd_core",
    )(*args)


def _ssd_layer(x, mod4, P, layer, j, st0, npr, seq_p, seq_s):
    m, d = x.shape
    nbp, nbs = npr // seq_p, (m - npr) // seq_s
    heads = SSD_GROUPS * SSD_R
    inner = heads * SSD_HEADDIM
    gn_w = SSD_GROUPS * SSD_STATE
    nzx = 2 * inner + 2 * gn_w
    tm, tn = min(1024, seq_s), 512
    cidx = _cond_index(tm, npr, seq_s)
    w_in = P["ssd_w_in"]
    wdt = w_in[j][:, nzx:].reshape(d, 2, SSD_GROUPS, SSD_R).transpose(0, 2, 1, 3).reshape(d, 2 * heads).astype(BF16)
    zx, dtc = pl.pallas_call(
        functools.partial(_ssd_in_kernel, d=d),
        grid=(m // tm, nzx // tn),
        in_specs=[pl.BlockSpec((tm, d), lambda i, n: (i, 0)),
                  _mod_spec(layer, cidx, 6 * d, 2),
                  pl.BlockSpec((1, 1, d), lambda i, n: (layer, 0, 0)),
                  pl.BlockSpec((1, d, tn), lambda i, n: (j, 0, n)),
                  pl.BlockSpec((d, 2 * heads), lambda i, n: (0, 0))],
        out_specs=[pl.BlockSpec((tn // LANES, tm, LANES), lambda i, n: (n, i, 0)),
                   pl.BlockSpec((SSD_GROUPS, tm, SSD_DTW), lambda i, n: (0, i, 0))],
        out_shape=[jax.ShapeDtypeStruct((nzx // LANES, m, LANES), F32),
                   jax.ShapeDtypeStruct((SSD_GROUPS, m, SSD_DTW), F32)],
        scratch_shapes=[pltpu.VMEM((tm, d), BF16)],
        compiler_params=_cparams(("arbitrary", "arbitrary")),
        name="ssd_in",
    )(x, mod4, P["g_norm1"].reshape(-1, 1, d), w_in, wdt)

    def compact(a):
        return a.reshape(2, SSD_GROUPS, SSD_R).transpose(1, 0, 2).reshape(SSD_GROUPS, 1, SSD_DTW)

    ncb = (inner + 2 * gn_w) // LANES
    cw = jnp.pad(P["ssd_w_conv"][j], ((0, 8 - SSD_CONV), (0, 0))).reshape(8, ncb, LANES).transpose(1, 0, 2)
    cbias = P["ssd_b_conv"][j].reshape(ncb, 1, LANES)
    dsum = jnp.repeat(P["ssd_d"][j][0] + P["ssd_d"][j][1], SSD_HEADDIM).reshape(SSD_GROUPS, 1, SSD_GW)
    gn = P["ssd_g_norm"][j].reshape(SSD_GROUPS, 1, SSD_GW)
    common = (zx, dtc, cw, cbias, compact(P["ssd_dt_bias"][j]), compact(P["ssd_a_log"][j]), dsum, gn)
    yp, st = _ssd_core(*common, None, seq_p, 0, nbp)
    h0 = st0.reshape(nbs, 2, SSD_GROUPS, SSD_GW, SSD_STATE)
    ys = _ssd_core(*common, h0, seq_s, npr, nbs)
    y = jnp.concatenate([yp, ys], axis=0)
    zb = jnp.zeros((P["ssd_w_out"].shape[0], d), F32)
    tmr = min(512, seq_s)
    xn = _proj_resid(x, y, mod4, P["ssd_w_out"], zb, layer, j, _cond_index(tmr, npr, seq_s), tmr)
    return xn, st.reshape(nbp, 2, heads, SSD_HEADDIM, SSD_STATE)


def kernel(x_prompt, x_sample, cache_ckv, cache_kpe, state_ssm, c, c_ctx, w_ada, b_ada, g_norm1, g_norm2,
           mla_w_dq, mla_g_q, mla_w_uq, mla_w_dkv, mla_g_kv, mla_w_ukv, mla_g_qn, mla_g_kn, mla_w_o,
           cv_w_pw1, cv_b_pw1, cv_w_dw, cv_b_dw, cv_g_ln, cv_b_ln, cv_w_pw2, cv_b_pw2,
           ssd_w_in, ssd_w_conv, ssd_b_conv, ssd_dt_bias, ssd_a_log, ssd_d, ssd_g_norm, ssd_w_out,
           ffn_w_in, ffn_w_out):
    P = dict(g_norm1=g_norm1, g_norm2=g_norm2,
             mla_w_dq=mla_w_dq, mla_g_q=mla_g_q, mla_w_uq=mla_w_uq, mla_w_dkv=mla_w_dkv, mla_g_kv=mla_g_kv,
             mla_w_ukv=mla_w_ukv, mla_g_qn=mla_g_qn, mla_g_kn=mla_g_kn, mla_w_o=mla_w_o,
             cv_w_pw1=cv_w_pw1, cv_b_pw1=cv_b_pw1, cv_w_dw=cv_w_dw, cv_b_dw=cv_b_dw, cv_g_ln=cv_g_ln,
             cv_b_ln=cv_b_ln, cv_w_pw2=cv_w_pw2, cv_b_pw2=cv_b_pw2,
             ssd_w_in=ssd_w_in, ssd_w_conv=ssd_w_conv, ssd_b_conv=ssd_b_conv, ssd_dt_bias=ssd_dt_bias,
             ssd_a_log=ssd_a_log, ssd_d=ssd_d, ssd_g_norm=ssd_g_norm, ssd_w_out=ssd_w_out)
    nbp, seq_p, d = x_prompt.shape
    nbs, seq_s, _ = x_sample.shape
    depth = w_ada.shape[0]
    npr = nbp * seq_p
    x = jnp.concatenate([x_prompt.reshape(npr, d), x_sample.reshape(nbs * seq_s, d)], axis=0)
    cond8 = jnp.concatenate([c_ctx[None, :], c, jnp.zeros((8 - 1 - nbs, d), F32)], axis=0)
    mod4 = _ada(cond8, w_ada, b_ada).reshape(depth, 8, 1, 6 * d)
    ffn_tm = min(1024, seq_s)
    ffn_cidx = _cond_index(ffn_tm, npr, seq_s)
    ckvs, kpes, ssms = [], [], []
    for i in range(depth):
        kind, j = i % N_MIXERS, i // N_MIXERS
        if kind == 0:
            x, ckv, kpe = _mla_layer(x, mod4, P, i, j, cache_ckv[:, j], cache_kpe[:, j], npr, seq_p, seq_s)
            ckvs.append(ckv)
            kpes.append(kpe)
        elif kind == 1:
            x = _conv_layer(x, mod4, P, i, j, npr, seq_p, seq_s)
        else:
            x, st = _ssd_layer(x, mod4, P, i, j, state_ssm[:, j], npr, seq_p, seq_s)
            ssms.append(st)
        x = _ffn(x, mod4, g_norm2, ffn_w_in, ffn_w_out, i, ffn_cidx, ffn_tm, 256)
    return (x[:npr].reshape(nbp, seq_p, d), x[npr:].reshape(nbs, seq_s, d),
            jnp.stack(ckvs, axis=1), jnp.stack(kpes, axis=1), jnp.stack(ssms, axis=1))
```

```python
import functools

import jax
import jax.numpy as jnp
from jax import lax
from jax.experimental import pallas as pl
from jax.experimental.pallas import tpu as pltpu

F32 = jnp.float32
BF16 = jnp.bfloat16

EPS = 1e-6
ROPE_BASE = 10000.0
GRID_W = 64
N_MIXERS = 3

LANES = 128
VMEM_LIMIT_BYTES = 56 * 1024 * 1024

MLA_HEADS = 16
QK_NOPE = 64
QK_ROPE = 32
QK_DIM = QK_NOPE + QK_ROPE
V_HEAD = 64
HEAD_PAD = LANES
CONV_WIDTH = 31
SSD_HEADDIM = 64
SSD_GROUPS = 4
SSD_STATE = 128
SSD_CONV = 5
CHUNK = 128


def _cparams(sem):
    return pltpu.CompilerParams(dimension_semantics=sem, vmem_limit_bytes=VMEM_LIMIT_BYTES)


def _silu(x):
    return x * jax.nn.sigmoid(x)


def _normmod(x, g, sh, sc):
    ms = jnp.mean(x * x, axis=-1, keepdims=True)
    return (x * lax.rsqrt(ms + EPS)) * (g * (1.0 + sc)) + sh


def _cond_index(tm, n_prompt_rows, sample_seq):
    def f(i):
        r = i * tm
        return jnp.where(r < n_prompt_rows, 0, 1 + (r - n_prompt_rows) // sample_seq)
    return f


def _mod_spec(layer, cidx, d6, ngrid):
    if ngrid == 1:
        return pl.BlockSpec((1, 1, 1, d6), lambda i: (layer, cidx(i), 0, 0))
    return pl.BlockSpec((1, 1, 1, d6), lambda i, j: (layer, cidx(i), 0, 0))


def _ada_kernel(c_ref, w_ref, b_ref, o_ref):
    s = _silu(c_ref[...]).astype(BF16)
    o_ref[0] = jnp.dot(s, w_ref[0].astype(BF16), preferred_element_type=F32) + b_ref[0]


def _ada(cond8, w_ada, b_ada):
    depth, d, n = w_ada.shape
    tn = n // 4
    return pl.pallas_call(
        _ada_kernel,
        grid=(depth, n // tn),
        in_specs=[pl.BlockSpec((8, d), lambda l, j: (0, 0)),
                  pl.BlockSpec((1, d, tn), lambda l, j: (l, 0, j)),
                  pl.BlockSpec((1, 1, tn), lambda l, j: (l, 0, j))],
        out_specs=pl.BlockSpec((1, 8, tn), lambda l, j: (l, 0, j)),
        out_shape=jax.ShapeDtypeStruct((depth, 8, n), F32),
        compiler_params=_cparams(("arbitrary", "arbitrary")),
        name="ada",
    )(cond8, w_ada, b_ada.reshape(depth, 1, n))


def _ffn_kernel(*refs, d, n_a):
    if n_a is None:
        x_ref, mod_ref, g_ref, wa_ref, wu_ref, wo_ref, o_ref, h_sc = refs
        acc = o_ref
    else:
        x_ref, mod_ref, g_ref, wa_ref, wu_ref, wo_ref, oa_ref, ob_ref, h_sc, acc = refs
    i, j = pl.program_id(0), pl.program_id(1)

    @pl.when(j == 0)
    def _():
        sh = mod_ref[0, 0, :, 3 * d:4 * d]
        sc = mod_ref[0, 0, :, 4 * d:5 * d]
        h_sc[...] = _normmod(x_ref[...], g_ref[0], sh, sc).astype(BF16)
        acc[...] = jnp.zeros_like(acc)

    h = h_sc[...]
    a = jnp.dot(h, wa_ref[0].astype(BF16), preferred_element_type=F32)
    u = jnp.dot(h, wu_ref[0].astype(BF16), preferred_element_type=F32)
    t = (_silu(a) * u).astype(BF16)
    acc[...] += jnp.dot(t, wo_ref[0].astype(BF16), preferred_element_type=F32)
    last = j == pl.num_programs(1) - 1
    gate = mod_ref[0, 0, :, 5 * d:6 * d]
    if n_a is None:
        @pl.when(last)
        def _():
            o_ref[...] = x_ref[...] + gate * acc[...]
    else:
        @pl.when(last & (i < n_a))
        def _():
            oa_ref[...] = x_ref[...] + gate * acc[...]

        @pl.when(last & (i >= n_a))
        def _():
            ob_ref[...] = x_ref[...] + gate * acc[...]


def _ffn(x, mod4, g2, w_in, w_out, layer, cidx, tm, th, split_rows=None):
    m, d = x.shape
    hid = w_out.shape[1]
    nj = hid // th
    scratch = [pltpu.VMEM((tm, d), BF16)]
    if split_rows is None:
        n_a = None
        out_specs = pl.BlockSpec((tm, d), lambda i, j: (i, 0))
        out_shape = jax.ShapeDtypeStruct((m, d), F32)
    else:
        n_a = split_rows // tm
        out_specs = [pl.BlockSpec((tm, d), lambda i, j: (jnp.minimum(i, n_a - 1), 0)),
                     pl.BlockSpec((tm, d), lambda i, j: (jnp.maximum(i - n_a, 0), 0))]
        out_shape = [jax.ShapeDtypeStruct((split_rows, d), F32), jax.ShapeDtypeStruct((m - split_rows, d), F32)]
        scratch.append(pltpu.VMEM((tm, d), F32))
    return pl.pallas_call(
        functools.partial(_ffn_kernel, d=d, n_a=n_a),
        grid=(m // tm, nj),
        in_specs=[pl.BlockSpec((tm, d), lambda i, j: (i, 0)),
                  _mod_spec(layer, cidx, 6 * d, 2),
                  pl.BlockSpec((1, 1, d), lambda i, j: (layer, 0, 0)),
                  pl.BlockSpec((1, d, th), lambda i, j: (layer, 0, j)),
                  pl.BlockSpec((1, d, th), lambda i, j: (layer, 0, j + nj)),
                  pl.BlockSpec((1, th, d), lambda i, j: (layer, j, 0))],
        out_specs=out_specs,
        out_shape=out_shape,
        scratch_shapes=scratch,
        compiler_params=_cparams(("arbitrary", "arbitrary")),
        name="ffn",
    )(x, mod4, g2.reshape(g2.shape[0], 1, d), w_in, w_in, w_out)


def _proj_resid_kernel(*refs, d, n_a):
    if n_a is None:
        x_ref, y_ref, mod_ref, w_ref, b_ref, o_ref, w_sc = refs
        x = x_ref[...]
    else:
        xa_ref, xb_ref, y_ref, mod_ref, w_ref, b_ref, o_ref, w_sc = refs
        x = jnp.where(pl.program_id(0) < n_a, xa_ref[...], xb_ref[...])

    @pl.when(pl.program_id(0) == 0)
    def _():
        w_sc[...] = w_ref[0].astype(BF16)

    gate = mod_ref[0, 0, :, 2 * d:3 * d]
    r = jnp.dot(y_ref[...], w_sc[...], preferred_element_type=F32) + b_ref[0]
    o_ref[...] = x + gate * r


def _proj_resid(x, y, mod4, w, b, layer, widx, cidx, tm, npr):
    m, k = y.shape
    if isinstance(x, tuple):
        d = x[0].shape[1]
        n_a = npr // tm
        x_specs = [pl.BlockSpec((tm, d), lambda i: (jnp.minimum(i, n_a - 1), 0)),
                   pl.BlockSpec((tm, d), lambda i: (jnp.maximum(i - n_a, 0), 0))]
        x_args = list(x)
    else:
        d = x.shape[1]
        n_a = None
        x_specs = [pl.BlockSpec((tm, d), lambda i: (i, 0))]
        x_args = [x]
    return pl.pallas_call(
        functools.partial(_proj_resid_kernel, d=d, n_a=n_a),
        grid=(m // tm,),
        in_specs=x_specs + [pl.BlockSpec((tm, k), lambda i: (i, 0)),
                            _mod_spec(layer, cidx, 6 * d, 1),
                            pl.BlockSpec((1, k, d), lambda i: (widx, 0, 0)),
                            pl.BlockSpec((1, 1, d), lambda i: (widx, 0, 0))],
        out_specs=pl.BlockSpec((tm, d), lambda i: (i, 0)),
        out_shape=jax.ShapeDtypeStruct((m, d), F32),
        scratch_shapes=[pltpu.VMEM((k, d), BF16)],
        compiler_params=_cparams(("arbitrary",)),
        name="proj_resid",
    )(*x_args, y, mod4, w, b.reshape(b.shape[0], 1, d))


def _conv_a_kernel(x_ref, mod_ref, g_ref, w_ref, b_ref, o_ref, w_sc, *, d):
    @pl.when(pl.program_id(0) == 0)
    def _():
        w_sc[...] = w_ref[0].astype(BF16)

    h = _normmod(x_ref[...], g_ref[0], mod_ref[0, 0, :, 0:d], mod_ref[0, 0, :, d:2 * d]).astype(BF16)
    y = jnp.dot(h, w_sc[...], preferred_element_type=F32) + b_ref[0]
    u = y[:, :d] * jax.nn.sigmoid(y[:, d:])
    for cb in range(d // LANES):
        o_ref[cb] = u[:, cb * LANES:(cb + 1) * LANES]


def _conv_b_kernel(*refs, d, t, rc, top):
    (x_ref, u_ref, mod_ref, wdw_ref, bdw_ref, gln_ref, bln_ref, w2_ref, b2_ref) = refs[:9]
    o_ref, upad, conv_sc, y_sc, w_sc = refs[-5:]
    nb = d // LANES
    half = (CONV_WIDTH - 1) // 2

    @pl.when(pl.program_id(0) == 0)
    def _():
        w_sc[...] = w2_ref[0].astype(BF16)
        upad[:, 0:top, :] = jnp.zeros((nb, top, LANES), F32)
        upad[:, top + t:2 * top + t, :] = jnp.zeros((nb, top, LANES), F32)

    upad[:, top:top + t, :] = u_ref[...]

    def conv_block(cb, carry):
        for r0 in range(0, t, rc):
            acc = jnp.broadcast_to(bdw_ref[cb], (rc, LANES))
            for k in range(CONV_WIDTH):
                acc = acc + wdw_ref[cb, pl.ds(k, 1), :] * upad[cb, pl.ds(r0 + top - half + k, rc), :]
            conv_sc[cb, pl.ds(r0, rc), :] = acc
        return carry

    lax.fori_loop(0, nb, conv_block, 0)

    rl = min(t, 256)
    for r0 in range(0, t, rl):
        s = jnp.zeros((rl, 1), F32)
        for cb in range(nb):
            s = s + jnp.sum(conv_sc[cb, r0:r0 + rl, :], axis=-1, keepdims=True)
        mean = s * (1.0 / d)
        v = jnp.zeros((rl, 1), F32)
        for cb in range(nb):
            c = conv_sc[cb, r0:r0 + rl, :] - mean
            v = v + jnp.sum(c * c, axis=-1, keepdims=True)
        rstd = lax.rsqrt(v * (1.0 / d) + EPS)
        for cb in range(nb):
            sl = slice(cb * LANES, (cb + 1) * LANES)
            y = (conv_sc[cb, r0:r0 + rl, :] - mean) * rstd * gln_ref[:, sl] + bln_ref[:, sl]
            y_sc[r0:r0 + rl, sl] = _silu(y).astype(BF16)

    gate = mod_ref[0, 0, :, 2 * d:3 * d]
    r = jnp.dot(y_sc[...], w_sc[...], preferred_element_type=F32) + b2_ref[0]
    o_ref[...] = x_ref[...] + gate * r


def _conv_b(x, u, mod4, wdw, bdw, gln, bln, w2, b2, layer, j, t, row0, nseq, cond0, cond_step, obuf):
    m, d = x.shape
    nb = d // LANES
    top = 16
    blk0 = row0 // t
    in_specs = [pl.BlockSpec((t, d), lambda i: (blk0 + i, 0)),
                pl.BlockSpec((nb, t, LANES), lambda i: (0, blk0 + i, 0)),
                pl.BlockSpec((1, 1, 1, 6 * d), lambda i: (layer, cond0 + cond_step * i, 0, 0)),
                pl.BlockSpec((nb, 32, LANES), lambda i: (0, 0, 0)),
                pl.BlockSpec((nb, 1, LANES), lambda i: (0, 0, 0)),
                pl.BlockSpec((1, d), lambda i: (0, 0)),
                pl.BlockSpec((1, d), lambda i: (0, 0)),
                pl.BlockSpec((1, d, d), lambda i: (j, 0, 0)),
                pl.BlockSpec((1, 1, d), lambda i: (j, 0, 0))]
    args = [x, u, mod4, wdw, bdw, gln, bln, w2, b2]
    aliases = {}
    if obuf is not None:
        in_specs.append(pl.BlockSpec(memory_space=pl.ANY))
        args.append(obuf)
        aliases = {len(args) - 1: 0}
    return pl.pallas_call(
        functools.partial(_conv_b_kernel, d=d, t=t, rc=64, top=top),
        grid=(nseq,),
        in_specs=in_specs,
        out_specs=pl.BlockSpec((t, d), lambda i: (blk0 + i, 0)),
        out_shape=jax.ShapeDtypeStruct((m, d), F32),
        input_output_aliases=aliases,
        scratch_shapes=[pltpu.VMEM((nb, t + 2 * top, LANES), F32),
                        pltpu.VMEM((nb, t, LANES), F32),
                        pltpu.VMEM((t, d), BF16),
                        pltpu.VMEM((d, d), BF16)],
        compiler_params=_cparams(("arbitrary",)),
        name="conv_b",
    )(*args)


def _conv_layer(x, mod4, P, layer, j, npr, seq_p, seq_s):
    m, d = x.shape
    nb = d // LANES
    tm = 256
    cidx = _cond_index(tm, npr, seq_s)
    u = pl.pallas_call(
        functools.partial(_conv_a_kernel, d=d),
        grid=(m // tm,),
        in_specs=[pl.BlockSpec((tm, d), lambda i: (i, 0)),
                  _mod_spec(layer, cidx, 6 * d, 1),
                  pl.BlockSpec((1, 1, d), lambda i: (layer, 0, 0)),
                  pl.BlockSpec((1, d, 2 * d), lambda i: (j, 0, 0)),
                  pl.BlockSpec((1, 1, 2 * d), lambda i: (j, 0, 0))],
        out_specs=pl.BlockSpec((nb, tm, LANES), lambda i: (0, i, 0)),
        out_shape=jax.ShapeDtypeStruct((nb, m, LANES), F32),
        scratch_shapes=[pltpu.VMEM((d, 2 * d), BF16)],
        compiler_params=_cparams(("arbitrary",)),
        name="conv_a",
    )(x, mod4, P["g_norm1"].reshape(-1, 1, d), P["cv_w_pw1"], P["cv_b_pw1"].reshape(-1, 1, 2 * d))
    wdw = jnp.pad(P["cv_w_dw"][j], ((0, 32 - CONV_WIDTH), (0, 0))).reshape(32, nb, LANES).transpose(1, 0, 2)
    bdw = P["cv_b_dw"][j].reshape(nb, 1, LANES)
    gln = P["cv_g_ln"][j].reshape(1, d)
    bln = P["cv_b_ln"][j].reshape(1, d)
    b2 = P["cv_b_pw2"].reshape(-1, 1, d)
    args = (x, u, mod4, wdw, bdw, gln, bln, P["cv_w_pw2"], b2, layer, j)
    op = _conv_b(*args, seq_p, 0, npr // seq_p, 0, 0, None)
    return _conv_b(*args, seq_s, npr, (m - npr) // seq_s, 1, 1, op)


_LOG2E = 1.4426950408889634
_ROT = QK_ROPE // 4


def _swap_index():
    idx = list(range(HEAD_PAD))
    for base in (QK_NOPE, QK_NOPE + 2 * _ROT):
        for l in range(_ROT):
            idx[base + l], idx[base + _ROT + l] = base + _ROT + l, base + l
    return jnp.asarray(idx, jnp.int32)


def _swap_cols(w):
    lane = jnp.arange(HEAD_PAD)
    rot = (lane >= QK_NOPE) & (lane < QK_DIM)
    return jnp.where(rot, jnp.take(w, _swap_index(), axis=-1), 0.0)


def _rope_tables(seq, gqn, gkn):
    rows = seq // GRID_W
    row = jnp.repeat(jnp.arange(rows, dtype=F32), GRID_W)
    col = jnp.tile(jnp.arange(GRID_W, dtype=F32), rows)
    axis = QK_ROPE // 2
    inv = ROPE_BASE ** (-jnp.arange(0, axis, 2, dtype=F32) / axis)
    ar, ac = row[:, None] * inv, col[:, None] * inv
    cr, sr, cc, sc = jnp.cos(ar), jnp.sin(ar), jnp.cos(ac), jnp.sin(ac)
    one = jnp.ones((seq, QK_NOPE), F32)
    zn = jnp.zeros((seq, QK_NOPE), F32)
    zp = jnp.zeros((seq, HEAD_PAD - QK_DIM), F32)
    c = jnp.concatenate([one, cr, cr, cc, cc, zp], axis=1)
    s = jnp.concatenate([zn, -sr, sr, -sc, sc, zp], axis=1)
    idx = _swap_index()
    return tuple(t for g in (gqn, gkn) for t in (c * g, s * jnp.take(g, idx, axis=-1)))


def _head_rstd(xh):
    ms = jnp.sum(xh * xh, axis=-1, keepdims=True) * (1.0 / QK_DIM)
    return lax.rsqrt(ms + EPS)


def _expand_kv(ckv_b, kpe, wuk_ref, wv_ref, gk, rot, k_ref, v_ref, v_transposed):
    kk = jnp.dot(ckv_b, wuk_ref[...], preferred_element_type=F32)
    if v_transposed:
        v = lax.dot_general(wv_ref[...], ckv_b, _NT, preferred_element_type=F32)
    else:
        v = jnp.dot(ckv_b, wv_ref[...], preferred_element_type=F32)
    v_ref[...] = v.astype(BF16)
    for h in range(MLA_HEADS):
        sl = slice(h * HEAD_PAD, (h + 1) * HEAD_PAD)
        kh = kk[:, sl] + kpe
        r = _head_rstd(kh)
        kh = (kh * gk) * r if rot is None else (kh * rot[0] + rot[1]) * r
        k_ref[:, sl] = kh.astype(BF16)


def _mla_proj_kernel(*refs, d, rope):
    (x_ref, mod_ref, g_ref, wcat_ref, gq_ref, wuq_ref, gkv_ref, wuk_ref, wv_ref, gqn_ref, gkn_ref) = refs[:11]
    if rope:
        cq_ref, sq_ref, ck_ref, sk_ref = refs[11:15]
        q_ref, k_ref, v_ref = refs[15:18]
    else:
        q_ref, k_ref, v_ref, ckv_ref, kpe_ref = refs[11:16]
    h = _normmod(x_ref[...], g_ref[0], mod_ref[0, 0, :, 0:d], mod_ref[0, 0, :, d:2 * d]).astype(BF16)
    t = jnp.dot(h, wcat_ref[...], preferred_element_type=F32)
    nq = gq_ref.shape[1]
    nkv = gkv_ref.shape[1]
    hq = MLA_HEADS * HEAD_PAD
    qd, craw, kpe = t[:, :nq], t[:, nq:nq + nkv], t[:, nq + nkv:nq + nkv + HEAD_PAD]
    cq = (qd * lax.rsqrt(jnp.mean(qd * qd, axis=-1, keepdims=True) + EPS) * gq_ref[...]).astype(BF16)
    ckv = craw * lax.rsqrt(jnp.mean(craw * craw, axis=-1, keepdims=True) + EPS) * gkv_ref[...]
    if not rope:
        ckv_ref[...] = ckv
        kpe_ref[...] = kpe
    q = jnp.dot(cq, wuq_ref[...], preferred_element_type=F32)
    if rope:
        ct, st = cq_ref[...], sq_ref[...]
    else:
        gqn = gqn_ref[...]
    for hd in range(MLA_HEADS):
        sl = slice(hd * HEAD_PAD, (hd + 1) * HEAD_PAD)
        qh = q[:, sl]
        r = _head_rstd(qh)
        qh = (qh * ct + q[:, hq + hd * HEAD_PAD:hq + (hd + 1) * HEAD_PAD] * st) * r if rope else (qh * gqn) * r
        q_ref[:, sl] = qh.astype(BF16)
    rot = None
    if rope:
        kpe2 = t[:, nq + nkv + HEAD_PAD:]
        rot = (ck_ref[...], kpe2 * sk_ref[...])
    _expand_kv(ckv.astype(BF16), kpe, wuk_ref, wv_ref, gkn_ref[...], rot, k_ref, v_ref, rope)


def _kv_ctx_kernel(ckv_ref, kpe_ref, wuk_ref, wv_ref, gkn_ref, k_ref, v_ref):
    _expand_kv(ckv_ref[...].astype(BF16), kpe_ref[...], wuk_ref, wv_ref, gkn_ref[...], None, k_ref, v_ref, True)


_NT = (((1,), (1,)), ((), ()))


def _attn_prompt_kernel(q_ref, k_ref, v_ref, o_ref):
    t = q_ref.shape[0]
    lane = lax.broadcasted_iota(jnp.int32, (t, LANES), 1)
    for hp in range(MLA_HEADS // 2):
        vp = v_ref[:, hp * LANES:(hp + 1) * LANES]
        outs = []
        for hh in range(2):
            sl = slice((2 * hp + hh) * HEAD_PAD, (2 * hp + hh + 1) * HEAD_PAD)
            s = lax.dot_general(q_ref[:, sl], k_ref[:, sl], _NT, preferred_element_type=F32)
            e = jnp.exp2(s - s.max(axis=-1, keepdims=True))
            l = jnp.sum(e, axis=-1, keepdims=True)
            outs.append(jnp.dot(e.astype(BF16), vp, preferred_element_type=F32) * (1.0 / l))
        o_ref[:, hp * LANES:(hp + 1) * LANES] = jnp.where(lane < V_HEAD, outs[0], outs[1]).astype(BF16)


def _attn_sample_kernel(q_ref, kc_ref, vtc_ref, kl_ref, vtl_ref, obuf_ref, o_ref):
    del obuf_ref
    nh = q_ref.shape[1] // HEAD_PAD
    row = lax.broadcasted_iota(jnp.int32, (LANES, 1), 0)

    def scores(hh):
        sl = slice(hh * HEAD_PAD, (hh + 1) * HEAD_PAD)
        qh = q_ref[:, sl]
        return [lax.dot_general(k_ref[:, sl], qh, _NT, preferred_element_type=F32)
                for k_ref in (kc_ref, kl_ref)]

    halves = []
    nxt = scores(0)
    for hh in range(nh):
        sts = nxt
        if hh + 1 < nh:
            nxt = scores(hh + 1)
        m = jnp.max(sts[0], axis=0, keepdims=True)
        for st in sts[1:]:
            m = jnp.maximum(m, jnp.max(st, axis=0, keepdims=True))
        pr = slice((hh // 2) * LANES, (hh // 2 + 1) * LANES)
        own = (row < V_HEAD) if hh % 2 == 0 else (row >= V_HEAD)
        acc = None
        for st, vt_ref in zip(sts, (vtc_ref, vtl_ref)):
            e = jnp.exp2(st - m).astype(BF16)
            vt = vt_ref[pr, :]
            lhs = jnp.where(own, vt, jnp.ones_like(vt))
            pv = jnp.dot(lhs, e, preferred_element_type=F32)
            acc = pv if acc is None else acc + pv
        if hh % 2 == 0:
            halves.append(acc[0:V_HEAD] * (1.0 / acc[V_HEAD:V_HEAD + 1]))
        else:
            halves.append(acc[V_HEAD:2 * V_HEAD] * (1.0 / acc[0:1]))
    o_ref[...] = jnp.concatenate(halves, axis=0).T.astype(BF16)


def _mla_weights(P, j):
    d = P["mla_w_dq"].shape[1]
    nq = P["mla_w_dq"].shape[2]
    nkv = P["mla_g_kv"].shape[1]
    pad = HEAD_PAD - QK_DIM
    w_dkv = P["mla_w_dkv"][j]
    w_kpe = jnp.pad(w_dkv[:, nkv:], ((0, 0), (QK_NOPE, pad)))
    wcat = jnp.concatenate([P["mla_w_dq"][j], w_dkv[:, :nkv], w_kpe], axis=1).astype(BF16)
    wuq = jnp.pad(P["mla_w_uq"][j].reshape(nq, MLA_HEADS, QK_DIM), ((0, 0), (0, 0), (0, pad)))
    wuq = wuq.reshape(nq, MLA_HEADS * HEAD_PAD).astype(BF16)
    wukv = P["mla_w_ukv"][j].reshape(nkv, MLA_HEADS, QK_NOPE + V_HEAD)
    wuk = jnp.pad(wukv[:, :, :QK_NOPE], ((0, 0), (0, 0), (0, HEAD_PAD - QK_NOPE)))
    wuk = wuk.reshape(nkv, MLA_HEADS * HEAD_PAD).astype(BF16)
    wv = wukv[:, :, QK_NOPE:].reshape(nkv, MLA_HEADS * V_HEAD).astype(BF16)
    gq = P["mla_g_q"][j].reshape(1, nq)
    gkv = P["mla_g_kv"][j].reshape(1, nkv)
    gqn = (jnp.pad(P["mla_g_qn"][j], (0, pad)) * (QK_DIM ** -0.5 * _LOG2E)).reshape(1, HEAD_PAD)
    gkn = jnp.pad(P["mla_g_kn"][j], (0, pad)).reshape(1, HEAD_PAD)
    wuq2 = _swap_cols(wuq.reshape(nq, MLA_HEADS, HEAD_PAD)).reshape(nq, MLA_HEADS * HEAD_PAD)
    wuq_r = jnp.concatenate([wuq, wuq2], axis=1)
    wcat_r = jnp.concatenate([wcat, _swap_cols(w_kpe).astype(BF16)], axis=1)
    return (wcat, gq, wuq, gkv, wuk, wv, gqn, gkn), (wcat_r, wuq_r, wv.T)


def _full(shape):
    return pl.BlockSpec(shape, lambda *_: (0,) * len(shape))


def _mla_proj(x, xblk0, mod4, g1, W, layer, row0, nrows, cidx, tm, rope_args):
    d = x.shape[1]
    wcat, gq, wuq, gkv, wuk, wv, gqn, gkn = W
    blk0 = row0 // tm
    hq = MLA_HEADS * HEAD_PAD
    hv = MLA_HEADS * V_HEAD
    nkv = gkv.shape[1]
    rope = rope_args is not None
    in_specs = [pl.BlockSpec((tm, d), lambda i: (xblk0 + i, 0)),
                pl.BlockSpec((1, 1, 1, 6 * d), lambda i: (layer, cidx(blk0 + i), 0, 0)),
                pl.BlockSpec((1, 1, d), lambda i: (layer, 0, 0)),
                _full(wcat.shape), _full(gq.shape), _full(wuq.shape), _full(gkv.shape), _full(wuk.shape),
                _full(wv.shape), _full(gqn.shape), _full(gkn.shape)]
    args = [x, mod4, g1, wcat, gq, wuq, gkv, wuk, wv, gqn, gkn]
    out_specs = [pl.BlockSpec((tm, hq), lambda i: (i, 0)),
                 pl.BlockSpec((tm, hq), lambda i: (i, 0))]
    out_shape = [jax.ShapeDtypeStruct((nrows, hq), BF16),
                 jax.ShapeDtypeStruct((nrows, hq), BF16)]
    if rope:
        wcat_r, wuq_r, wvt, tabs = rope_args
        per = tabs[0].shape[0] // tm
        in_specs[3], in_specs[5], in_specs[8] = _full(wcat_r.shape), _full(wuq_r.shape), _full(wvt.shape)
        args[3], args[5], args[8] = wcat_r, wuq_r, wvt
        in_specs += [pl.BlockSpec((tm, HEAD_PAD), lambda i: (i % per, 0))] * 4
        args += list(tabs)
        out_specs += [pl.BlockSpec((hv, tm), lambda i: (0, i))]
        out_shape += [jax.ShapeDtypeStruct((hv, nrows), BF16)]
    else:
        out_specs += [pl.BlockSpec((tm, hv), lambda i: (i, 0)),
                      pl.BlockSpec((tm, nkv), lambda i: (i, 0)), pl.BlockSpec((tm, HEAD_PAD), lambda i: (i, 0))]
        out_shape += [jax.ShapeDtypeStruct((nrows, hv), BF16),
                      jax.ShapeDtypeStruct((nrows, nkv), F32), jax.ShapeDtypeStruct((nrows, HEAD_PAD), F32)]
    return pl.pallas_call(
        functools.partial(_mla_proj_kernel, d=d, rope=rope),
        grid=(nrows // tm,),
        in_specs=in_specs, out_specs=out_specs, out_shape=out_shape,
        compiler_params=_cparams(("arbitrary",)),
        name="mla_proj_rope" if rope else "mla_proj",
    )(*args)


def _mla_layer(x, mod4, P, layer, j, cache_ckv, cache_kpe, npr, seq_p, seq_s):
    xa, xb = x if isinstance(x, tuple) else (x, x)
    d = xa.shape[1]
    nrs = xb.shape[0] if isinstance(x, tuple) else xb.shape[0] - npr
    m = npr + nrs
    nbp, nbs = npr // seq_p, nrs // seq_s
    past = cache_ckv.shape[1]
    W, (wcat_r, wuq_r, wvt) = _mla_weights(P, j)
    wuk, gqn, gkn = W[4], W[6], W[7]
    g1 = P["g_norm1"].reshape(-1, 1, d)
    hq = MLA_HEADS * HEAD_PAD
    hv = MLA_HEADS * V_HEAD
    tm = 256
    cidx = _cond_index(tm, npr, seq_s)
    xb0 = 0 if isinstance(x, tuple) else npr // tm
    tabs = _rope_tables(seq_s, gqn[0], gkn[0])
    qp, kp, vp, ckv, kpe128 = _mla_proj(xa, 0, mod4, g1, W, layer, 0, npr, cidx, tm, None)
    qs, ks, vts = _mla_proj(xb, xb0, mod4, g1, W, layer, npr, nrs, cidx, tm, (wcat_r, wuq_r, wvt, tabs))
    cc = cache_ckv.reshape(nbs * past, -1)
    ck = jnp.pad(cache_kpe.reshape(nbs * past, -1), ((0, 0), (QK_NOPE, HEAD_PAD - QK_DIM)))
    kc, vtc = pl.pallas_call(
        _kv_ctx_kernel,
        grid=(nbs,),
        in_specs=[pl.BlockSpec((past, cc.shape[1]), lambda i: (i, 0)),
                  pl.BlockSpec((past, HEAD_PAD), lambda i: (i, 0)),
                  _full(wuk.shape), _full(wvt.shape), _full(gkn.shape)],
        out_specs=[pl.BlockSpec((past, hq), lambda i: (i, 0)), pl.BlockSpec((hv, past), lambda i: (0, i))],
        out_shape=[jax.ShapeDtypeStruct((nbs * past, hq), BF16), jax.ShapeDtypeStruct((hv, nbs * past), BF16)],
        compiler_params=_cparams(("arbitrary",)),
        name="mla_kv_ctx",
    )(cc, ck, wuk, wvt, gkn)
    op = pl.pallas_call(
        _attn_prompt_kernel,
        grid=(nbp,),
        in_specs=[pl.BlockSpec((seq_p, hq), lambda b: (b, 0)),
                  pl.BlockSpec((seq_p, hq), lambda b: (b, 0)),
                  pl.BlockSpec((seq_p, hv), lambda b: (b, 0))],
        out_specs=pl.BlockSpec((seq_p, hv), lambda b: (b, 0)),
        out_shape=jax.ShapeDtypeStruct((m, hv), BF16),
        compiler_params=_cparams(("arbitrary",)),
        name="attn_prompt",
    )(qp, kp, vp)
    tq = min(seq_s, 512)
    nqt = seq_s // tq
    nh = 4
    hw, vw = nh * HEAD_PAD, nh * V_HEAD
    o0 = npr // tq
    o = pl.pallas_call(
        _attn_sample_kernel,
        grid=(nbs, nqt, MLA_HEADS // nh),
        in_specs=[pl.BlockSpec((tq, hw), lambda b, qi, hp: (b * nqt + qi, hp)),
                  pl.BlockSpec((past, hw), lambda b, qi, hp: (b, hp)),
                  pl.BlockSpec((vw, past), lambda b, qi, hp: (hp, b)),
                  pl.BlockSpec((seq_s, hw), lambda b, qi, hp: (b, hp)),
                  pl.BlockSpec((vw, seq_s), lambda b, qi, hp: (hp, b)),
                  pl.BlockSpec(memory_space=pl.ANY)],
        out_specs=pl.BlockSpec((tq, vw), lambda b, qi, hp: (o0 + b * nqt + qi, hp)),
        out_shape=jax.ShapeDtypeStruct((m, hv), BF16),
        input_output_aliases={5: 0},
        compiler_params=_cparams(("arbitrary", "arbitrary", "arbitrary")),
        name="attn_sample",
    )(qs, kc, vtc, ks, vts, op)
    zb = jnp.zeros((P["mla_w_o"].shape[0], d), F32)
    tmr = min(512, seq_s)
    xn = _proj_resid(x, o, mod4, P["mla_w_o"], zb, layer, j, _cond_index(tmr, npr, seq_s), tmr, npr)
    new_ckv = ckv.reshape(nbp, seq_p, -1)
    new_kpe = kpe128[:, QK_NOPE:QK_DIM].reshape(nbp, seq_p, QK_ROPE)
    return xn, new_ckv, new_kpe


SSD_R = 8
SSD_GW = SSD_R * SSD_HEADDIM
SSD_NX = SSD_GW // LANES
SSD_DTW = 2 * SSD_R
SSD_TOP = 8


def _ssd_in_kernel(x_ref, mod_ref, g_ref, w_ref, wdt_ref, zx_ref, dt_ref, h_sc, *, d):
    @pl.when(pl.program_id(1) == 0)
    def _():
        h = _normmod(x_ref[...], g_ref[0], mod_ref[0, 0, :, 0:d], mod_ref[0, 0, :, d:2 * d]).astype(BF16)
        h_sc[...] = h
        dt = jnp.dot(h, wdt_ref[...], preferred_element_type=F32)
        for g in range(SSD_GROUPS):
            dt_ref[g] = dt[:, g * SSD_DTW:(g + 1) * SSD_DTW]

    r = jnp.dot(h_sc[...], w_ref[0].astype(BF16), preferred_element_type=F32)
    for c in range(r.shape[1] // LANES):
        zx_ref[c] = r[:, c * LANES:(c + 1) * LANES]


def _split3(x):
    hi = x.astype(BF16)
    r1 = x - hi.astype(F32)
    mid = r1.astype(BF16)
    lo = (r1 - mid.astype(F32)).astype(BF16)
    return hi, mid, lo


def _dot3_left(tri, x):
    return sum(jnp.dot(tri, p, preferred_element_type=F32) for p in _split3(x))


def _dot3_right(x, tri):
    return sum(jnp.dot(p, tri, preferred_element_type=F32) for p in _split3(x))


def _pair(lane, col_a, col_b):
    return jnp.where(lane < SSD_HEADDIM, col_a, col_b)


def _ssd_core_kernel(*refs, t, has_h0):
    (z_ref, x_ref, b_ref, c_ref, dt_ref, wx_ref, wb_ref, wc_ref, bx_ref, bb_ref, bc_ref,
     dtb_ref, alog_ref, dsum_ref, gn_ref) = refs[:15]
    if has_h0:
        h0_ref, y_ref = refs[15], refs[17]
        st_ref = None
        scr = refs[18:]
    else:
        y_ref, st_ref = refs[15:17]
        h0_ref = None
        scr = refs[17:]
    (pad_sc, xc_sc, bcv_sc, ccv_sc, bt_sc, cb_sc, y_sc, dt_sc, da_sc, dtt_sc, dat_sc, tmp_sc, s_sc) = scr
    nc = t // CHUNK
    half = (SSD_CONV - 1) // 2
    nblk = SSD_NX + 2

    @pl.when((pl.program_id(0) == 0) & (pl.program_id(1) == 0))
    def _():
        pad_sc[:, 0:SSD_TOP, :] = jnp.zeros((nblk, SSD_TOP, LANES), F32)
        pad_sc[:, SSD_TOP + t:2 * SSD_TOP + t, :] = jnp.zeros((nblk, SSD_TOP, LANES), F32)
        tmp_sc[...] = jnp.zeros_like(tmp_sc)

    pad_sc[0:SSD_NX, SSD_TOP:SSD_TOP + t, :] = x_ref[...]
    pad_sc[SSD_NX, SSD_TOP:SSD_TOP + t, :] = b_ref[0]
    pad_sc[SSD_NX + 1, SSD_TOP:SSD_TOP + t, :] = c_ref[0]

    for blk in range(nblk):
        if blk < SSD_NX:
            w_r, bias = wx_ref, bx_ref[blk]
            widx = blk
        elif blk == SSD_NX:
            w_r, bias, widx = wb_ref, bb_ref[0], 0
        else:
            w_r, bias, widx = wc_ref, bc_ref[0], 0
        for c in range(nc):
            r0 = c * CHUNK
            acc = jnp.broadcast_to(bias, (CHUNK, LANES))
            for k in range(SSD_CONV):
                acc = acc + w_r[widx, k:k + 1, :] * pad_sc[blk, r0 + SSD_TOP - half + k:r0 + SSD_TOP - half + k + CHUNK, :]
            acc = _silu(acc)
            if blk < SSD_NX:
                xc_sc[blk, r0:r0 + CHUNK, :] = acc
            elif blk == SSD_NX:
                bcv_sc[r0:r0 + CHUNK, :] = acc.astype(BF16)
                bt_sc[c] = acc.T.astype(BF16)
            else:
                ccv_sc[r0:r0 + CHUNK, :] = acc.astype(BF16)

    v = dt_ref[0] + dtb_ref[0]
    dt = jnp.maximum(v, 0.0) + jnp.log1p(jnp.exp(-jnp.abs(v)))
    da = dt * (-jnp.exp(alog_ref[0]))
    dt_sc[...] = dt
    da_sc[...] = da
    for c in range(nc):
        r0 = c * CHUNK
        tmp_sc[:, 0:SSD_DTW] = dt[r0:r0 + CHUNK, :]
        dtt_sc[c] = tmp_sc[...].T[0:SSD_DTW, :]
        tmp_sc[:, 0:SSD_DTW] = da[r0:r0 + CHUNK, :]
        dat_sc[c] = tmp_sc[...].T[0:SSD_DTW, :]

    row = lax.broadcasted_iota(jnp.int32, (CHUNK, CHUNK), 0)
    colm = lax.broadcasted_iota(jnp.int32, (CHUNK, CHUNK), 1)
    lower = colm <= row
    upper = colm >= row
    tri_l = jnp.where(lower, 1.0, 0.0).astype(BF16)
    tri_u = jnp.where(upper, 1.0, 0.0).astype(BF16)
    lane = lax.broadcasted_iota(jnp.int32, (CHUNK, LANES), 1)
    lane1 = lax.broadcasted_iota(jnp.int32, (1, LANES), 1)

    for dr in range(2):
        lo = dr * SSD_R
        mask = upper if dr else lower
        tri = tri_u if dr else tri_l
        tri_t = tri_l if dr else tri_u
        if has_h0:
            s_sc[...] = h0_ref[0, dr, 0].T
        else:
            s_sc[...] = jnp.zeros_like(s_sc)

        def chunk(it, carry, dr=dr, lo=lo, mask=mask, tri=tri, tri_t=tri_t):
            c = (nc - 1 - it) if dr else it
            r0 = pl.multiple_of(c * CHUNK, CHUNK)
            rows = pl.ds(r0, CHUNK)
            cum = _dot3_left(tri, da_sc[rows, :])
            cum_t = _dot3_right(dat_sc[c], tri_t)
            dtt = dtt_sc[c]
            edge = cum[0:1, :] if dr else cum[CHUNK - 1:CHUNK, :]
            w2 = dt_sc[rows, :] * jnp.exp(edge - cum)
            ecum = jnp.exp(cum)
            eedge = jnp.exp(edge)
            bmat = bcv_sc[rows, :]
            cmat = ccv_sc[rows, :]
            if dr == 0:
                cb = lax.dot_general(cmat, bmat, _NT, preferred_element_type=F32)
                cb_sc[c] = cb
            else:
                cb = cb_sc[c]
            s_prev = s_sc[...]
            yoff = jnp.dot(cmat, s_prev.astype(BF16), preferred_element_type=F32)
            xd = []
            for q in range(SSD_NX):
                ha, hb = lo + 2 * q, lo + 2 * q + 1
                xblk = xc_sc[q, rows, :]
                xb16 = xblk.astype(BF16)
                yd = []
                for hh in (ha, hb):
                    seg = cum[:, hh:hh + 1] - cum_t[hh:hh + 1, :]
                    lm = jnp.exp(jnp.where(mask, seg, -jnp.inf))
                    mh = (cb * lm * dtt[hh:hh + 1, :]).astype(BF16)
                    yd.append(jnp.dot(mh, xb16, preferred_element_type=F32))
                y = _pair(lane, yd[0], yd[1])
                y = y + yoff[:, q * LANES:(q + 1) * LANES] * _pair(lane, ecum[:, ha:ha + 1], ecum[:, hb:hb + 1])
                if dr == 0:
                    y_sc[q, rows, :] = y + xblk * dsum_ref[0, :, q * LANES:(q + 1) * LANES]
                else:
                    y_sc[q, rows, :] = y_sc[q, rows, :] + y
                xd.append((xblk * _pair(lane, w2[:, ha:ha + 1], w2[:, hb:hb + 1])).astype(BF16))
            st = jnp.dot(bt_sc[c], jnp.concatenate(xd, axis=1), preferred_element_type=F32)
            cd = jnp.concatenate(
                [_pair(lane1, eedge[:, lo + 2 * q:lo + 2 * q + 1], eedge[:, lo + 2 * q + 1:lo + 2 * q + 2])
                 for q in range(SSD_NX)], axis=1)
            s_sc[...] = s_prev * cd + st
            return carry

        lax.fori_loop(0, nc, chunk, 0)
        if st_ref is not None:
            st_ref[0, dr, 0] = s_sc[...].T

    rl = min(t, 256)
    for r0 in range(0, t, rl):
        yz = [y_sc[q, r0:r0 + rl, :] * _silu(z_ref[q, r0:r0 + rl, :]) for q in range(SSD_NX)]
        ms = sum(jnp.sum(a * a, axis=-1, keepdims=True) for a in yz) * (1.0 / SSD_GW)
        rstd = lax.rsqrt(ms + EPS)
        for q in range(SSD_NX):
            sl = slice(q * LANES, (q + 1) * LANES)
            y_ref[r0:r0 + rl, sl] = (yz[q] * rstd * gn_ref[0, :, sl]).astype(BF16)


def _ssd_core(zx, dtc, cw, cbias, dtb, alog, dsum, gn, h0, ybuf, t, row0, nseq):
    has_h0 = h0 is not None
    m = zx.shape[1]
    b0 = row0 // t
    nc = t // CHUNK
    nz = SSD_GROUPS * SSD_NX
    in_specs = [pl.BlockSpec((SSD_NX, t, LANES), lambda b, g: (g, b0 + b, 0)),
                pl.BlockSpec((SSD_NX, t, LANES), lambda b, g: (SSD_GROUPS + g, b0 + b, 0)),
                pl.BlockSpec((1, t, LANES), lambda b, g: (2 * nz + g, b0 + b, 0)),
                pl.BlockSpec((1, t, LANES), lambda b, g: (2 * nz + SSD_GROUPS + g, b0 + b, 0)),
                pl.BlockSpec((1, t, SSD_DTW), lambda b, g: (g, b0 + b, 0)),
                pl.BlockSpec((SSD_NX, 8, LANES), lambda b, g: (g, 0, 0)),
                pl.BlockSpec((1, 8, LANES), lambda b, g: (nz + g, 0, 0)),
                pl.BlockSpec((1, 8, LANES), lambda b, g: (nz + SSD_GROUPS + g, 0, 0)),
                pl.BlockSpec((SSD_NX, 1, LANES), lambda b, g: (g, 0, 0)),
                pl.BlockSpec((1, 1, LANES), lambda b, g: (nz + g, 0, 0)),
                pl.BlockSpec((1, 1, LANES), lambda b, g: (nz + SSD_GROUPS + g, 0, 0)),
                pl.BlockSpec((1, 1, SSD_DTW), lambda b, g: (g, 0, 0)),
                pl.BlockSpec((1, 1, SSD_DTW), lambda b, g: (g, 0, 0)),
                pl.BlockSpec((1, 1, SSD_GW), lambda b, g: (g, 0, 0)),
                pl.BlockSpec((1, 1, SSD_GW), lambda b, g: (g, 0, 0))]
    args = [zx, zx, zx, zx, dtc, cw, cw, cw, cbias, cbias, cbias, dtb, alog, dsum, gn]
    st_spec = pl.BlockSpec((1, 2, 1, SSD_GW, SSD_STATE), lambda b, g: (b, 0, g, 0, 0))
    y_spec = pl.BlockSpec((t, SSD_GW), lambda b, g: (b0 + b, g))
    y_shape = jax.ShapeDtypeStruct((m, SSD_GROUPS * SSD_GW), BF16)
    aliases = {}
    if has_h0:
        in_specs += [st_spec, pl.BlockSpec(memory_space=pl.ANY)]
        args += [h0, ybuf]
        aliases = {len(args) - 1: 0}
        out_specs, out_shape = y_spec, y_shape
    else:
        out_specs = [y_spec, st_spec]
        out_shape = [y_shape, jax.ShapeDtypeStruct((nseq, 2, SSD_GROUPS, SSD_GW, SSD_STATE), F32)]
    scratch = [pltpu.VMEM((SSD_NX + 2, t + 2 * SSD_TOP, LANES), F32),
               pltpu.VMEM((SSD_NX, t, LANES), F32),
               pltpu.VMEM((t, LANES), BF16),
               pltpu.VMEM((t, LANES), BF16),
               pltpu.VMEM((nc, CHUNK, CHUNK), BF16),
               pltpu.VMEM((nc, CHUNK, CHUNK), F32),
               pltpu.VMEM((SSD_NX, t, LANES), F32),
               pltpu.VMEM((t, SSD_DTW), F32),
               pltpu.VMEM((t, SSD_DTW), F32),
               pltpu.VMEM((nc, SSD_DTW, CHUNK), F32),
               pltpu.VMEM((nc, SSD_DTW, CHUNK), F32),
               pltpu.VMEM((CHUNK, LANES), F32),
               pltpu.VMEM((SSD_STATE, SSD_GW), F32)]
    return pl.pallas_call(
        functools.partial(_ssd_core_kernel, t=t, has_h0=has_h0),
        grid=(nseq, SSD_GROUPS),
        in_specs=in_specs, out_specs=out_specs, out_shape=out_shape,
        input_output_aliases=aliases,
        scratch_shapes=scratch,
        compiler_params=_cparams(("arbitrary", "arbitrary")),
        name="ssd_core_h0" if has_h0 else "ssd_core",
    )(*args)


def _ssd_layer(x, mod4, P, layer, j, st0, npr, seq_p, seq_s):
    m, d = x.shape
    nbp, nbs = npr // seq_p, (m - npr) // seq_s
    heads = SSD_GROUPS * SSD_R
    inner = heads * SSD_HEADDIM
    gn_w = SSD_GROUPS * SSD_STATE
    nzx = 2 * inner + 2 * gn_w
    tm, tn = min(1024, seq_s), 1024
    cidx = _cond_index(tm, npr, seq_s)
    w_in = P["ssd_w_in"]
    wdt = w_in[j][:, nzx:].reshape(d, 2, SSD_GROUPS, SSD_R).transpose(0, 2, 1, 3).reshape(d, 2 * heads).astype(BF16)
    zx, dtc = pl.pallas_call(
        functools.partial(_ssd_in_kernel, d=d),
        grid=(m // tm, nzx // tn),
        in_specs=[pl.BlockSpec((tm, d), lambda i, n: (i, 0)),
                  _mod_spec(layer, cidx, 6 * d, 2),
                  pl.BlockSpec((1, 1, d), lambda i, n: (layer, 0, 0)),
                  pl.BlockSpec((1, d, tn), lambda i, n: (j, 0, n)),
                  pl.BlockSpec((d, 2 * heads), lambda i, n: (0, 0))],
        out_specs=[pl.BlockSpec((tn // LANES, tm, LANES), lambda i, n: (n, i, 0)),
                   pl.BlockSpec((SSD_GROUPS, tm, SSD_DTW), lambda i, n: (0, i, 0))],
        out_shape=[jax.ShapeDtypeStruct((nzx // LANES, m, LANES), F32),
                   jax.ShapeDtypeStruct((SSD_GROUPS, m, SSD_DTW), F32)],
        scratch_shapes=[pltpu.VMEM((tm, d), BF16)],
        compiler_params=_cparams(("arbitrary", "arbitrary")),
        name="ssd_in",
    )(x, mod4, P["g_norm1"].reshape(-1, 1, d), w_in, wdt)

    def compact(a):
        return a.reshape(2, SSD_GROUPS, SSD_R).transpose(1, 0, 2).reshape(SSD_GROUPS, 1, SSD_DTW)

    ncb = (inner + 2 * gn_w) // LANES
    cw = jnp.pad(P["ssd_w_conv"][j], ((0, 8 - SSD_CONV), (0, 0))).reshape(8, ncb, LANES).transpose(1, 0, 2)
    cbias = P["ssd_b_conv"][j].reshape(ncb, 1, LANES)
    dsum = jnp.repeat(P["ssd_d"][j][0] + P["ssd_d"][j][1], SSD_HEADDIM).reshape(SSD_GROUPS, 1, SSD_GW)
    gn = P["ssd_g_norm"][j].reshape(SSD_GROUPS, 1, SSD_GW)
    common = (zx, dtc, cw, cbias, compact(P["ssd_dt_bias"][j]), compact(P["ssd_a_log"][j]), dsum, gn)
    yp, st = _ssd_core(*common, None, None, seq_p, 0, nbp)
    h0 = st0.reshape(nbs, 2, SSD_GROUPS, SSD_GW, SSD_STATE)
    y = _ssd_core(*common, h0, yp, seq_s, npr, nbs)
    zb = jnp.zeros((P["ssd_w_out"].shape[0], d), F32)
    tmr = min(512, seq_s)
    xn = _proj_resid(x, y, mod4, P["ssd_w_out"], zb, layer, j, _cond_index(tmr, npr, seq_s), tmr, npr)
    return xn, st.reshape(nbp, 2, heads, SSD_HEADDIM, SSD_STATE)


def kernel(x_prompt, x_sample, cache_ckv, cache_kpe, state_ssm, c, c_ctx, w_ada, b_ada, g_norm1, g_norm2,
           mla_w_dq, mla_g_q, mla_w_uq, mla_w_dkv, mla_g_kv, mla_w_ukv, mla_g_qn, mla_g_kn, mla_w_o,
           cv_w_pw1, cv_b_pw1, cv_w_dw, cv_b_dw, cv_g_ln, cv_b_ln, cv_w_pw2, cv_b_pw2,
           ssd_w_in, ssd_w_conv, ssd_b_conv, ssd_dt_bias, ssd_a_log, ssd_d, ssd_g_norm, ssd_w_out,
           ffn_w_in, ffn_w_out):
    P = dict(g_norm1=g_norm1, g_norm2=g_norm2,
             mla_w_dq=mla_w_dq, mla_g_q=mla_g_q, mla_w_uq=mla_w_uq, mla_w_dkv=mla_w_dkv, mla_g_kv=mla_g_kv,
             mla_w_ukv=mla_w_ukv, mla_g_qn=mla_g_qn, mla_g_kn=mla_g_kn, mla_w_o=mla_w_o,
             cv_w_pw1=cv_w_pw1, cv_b_pw1=cv_b_pw1, cv_w_dw=cv_w_dw, cv_b_dw=cv_b_dw, cv_g_ln=cv_g_ln,
             cv_b_ln=cv_b_ln, cv_w_pw2=cv_w_pw2, cv_b_pw2=cv_b_pw2,
             ssd_w_in=ssd_w_in, ssd_w_conv=ssd_w_conv, ssd_b_conv=ssd_b_conv, ssd_dt_bias=ssd_dt_bias,
             ssd_a_log=ssd_a_log, ssd_d=ssd_d, ssd_g_norm=ssd_g_norm, ssd_w_out=ssd_w_out)
    nbp, seq_p, d = x_prompt.shape
    nbs, seq_s, _ = x_sample.shape
    depth = w_ada.shape[0]
    npr = nbp * seq_p
    x = (x_prompt.reshape(npr, d), x_sample.reshape(nbs * seq_s, d))
    cond8 = jnp.concatenate([c_ctx[None, :], c, jnp.zeros((8 - 1 - nbs, d), F32)], axis=0)
    mod4 = _ada(cond8, w_ada, b_ada).reshape(depth, 8, 1, 6 * d)
    ffn_tm = min(1024, seq_s)
    ffn_cidx = _cond_index(ffn_tm, npr, seq_s)
    ckvs, kpes, ssms = [], [], []
    for i in range(depth):
        kind, j = i % N_MIXERS, i // N_MIXERS
        if kind != 0 and isinstance(x, tuple):
            x = jnp.concatenate(x, axis=0)
        if kind == 0:
            x, ckv, kpe = _mla_layer(x, mod4, P, i, j, cache_ckv[:, j], cache_kpe[:, j], npr, seq_p, seq_s)
            ckvs.append(ckv)
            kpes.append(kpe)
        elif kind == 1:
            x = _conv_layer(x, mod4, P, i, j, npr, seq_p, seq_s)
        else:
            x, st = _ssd_layer(x, mod4, P, i, j, state_ssm[:, j], npr, seq_p, seq_s)
            ssms.append(st)
        x = _ffn(x, mod4, g_norm2, ffn_w_in, ffn_w_out, i, ffn_cidx, ffn_tm, 256,
                 split_rows=npr if i == depth - 1 else None)
    return (x[0].reshape(nbp, seq_p, d), x[1].reshape(nbs, seq_s, d),
            jnp.stack(ckvs, axis=1), jnp.stack(kpes, axis=1), jnp.stack(ssms, axis=1))
```

```python
import functools

import jax
import jax.numpy as jnp
import numpy as np
from jax import lax
from jax.experimental import pallas as pl
from jax.experimental.pallas import tpu as pltpu

F32 = jnp.float32
BF16 = jnp.bfloat16

EPS = 1e-6
ROPE_BASE = 10000.0
GRID_W = 64
N_MIXERS = 3

LANES = 128
VMEM_LIMIT_BYTES = 56 * 1024 * 1024

MLA_HEADS = 16
QK_NOPE = 64
QK_ROPE = 32
QK_DIM = QK_NOPE + QK_ROPE
V_HEAD = 64
HEAD_PAD = LANES
CONV_WIDTH = 31
SSD_HEADDIM = 64
SSD_GROUPS = 4
SSD_STATE = 128
SSD_CONV = 5
CHUNK = 128


def _cparams(sem):
    return pltpu.CompilerParams(dimension_semantics=sem, vmem_limit_bytes=VMEM_LIMIT_BYTES)


def _silu(x):
    return x * jax.nn.sigmoid(x)


def _normmod(x, g, sh, sc):
    ms = jnp.mean(x * x, axis=-1, keepdims=True)
    return (x * lax.rsqrt(ms + EPS)) * (g * (1.0 + sc)) + sh


def _cond_index(tm, n_prompt_rows, sample_seq):
    def f(i):
        r = i * tm
        return jnp.where(r < n_prompt_rows, 0, 1 + (r - n_prompt_rows) // sample_seq)
    return f


def _mod_spec(layer, cidx, d6, ngrid):
    if ngrid == 1:
        return pl.BlockSpec((1, 1, 1, d6), lambda i: (layer, cidx(i), 0, 0))
    return pl.BlockSpec((1, 1, 1, d6), lambda i, j: (layer, cidx(i), 0, 0))


def _ada_kernel(c_ref, w_ref, b_ref, o_ref):
    s = _silu(c_ref[...]).astype(BF16)
    o_ref[0] = jnp.dot(s, w_ref[0].astype(BF16), preferred_element_type=F32) + b_ref[0]


def _ada(cond8, w_ada, b_ada):
    depth, d, n = w_ada.shape
    tn = n // 4
    return pl.pallas_call(
        _ada_kernel,
        grid=(depth, n // tn),
        in_specs=[pl.BlockSpec((8, d), lambda l, j: (0, 0)),
                  pl.BlockSpec((1, d, tn), lambda l, j: (l, 0, j)),
                  pl.BlockSpec((1, 1, tn), lambda l, j: (l, 0, j))],
        out_specs=pl.BlockSpec((1, 8, tn), lambda l, j: (l, 0, j)),
        out_shape=jax.ShapeDtypeStruct((depth, 8, n), F32),
        compiler_params=_cparams(("arbitrary", "arbitrary")),
        name="ada",
    )(cond8, w_ada, b_ada.reshape(depth, 1, n))


def _ffn_kernel(*refs, d, n_a):
    if n_a is None:
        x_ref, mod_ref, g_ref, wa_ref, wu_ref, wo_ref, o_ref, h_sc = refs
        acc = o_ref
    else:
        x_ref, mod_ref, g_ref, wa_ref, wu_ref, wo_ref, oa_ref, ob_ref, h_sc, acc = refs
    i, j = pl.program_id(0), pl.program_id(1)

    @pl.when(j == 0)
    def _():
        sh = mod_ref[0, 0, :, 3 * d:4 * d]
        sc = mod_ref[0, 0, :, 4 * d:5 * d]
        h_sc[...] = _normmod(x_ref[...], g_ref[0], sh, sc).astype(BF16)
        acc[...] = jnp.zeros_like(acc)

    h = h_sc[...]
    a = jnp.dot(h, wa_ref[0].astype(BF16), preferred_element_type=F32)
    u = jnp.dot(h, wu_ref[0].astype(BF16), preferred_element_type=F32)
    t = (_silu(a) * u).astype(BF16)
    acc[...] += jnp.dot(t, wo_ref[0].astype(BF16), preferred_element_type=F32)
    last = j == pl.num_programs(1) - 1
    gate = mod_ref[0, 0, :, 5 * d:6 * d]
    if n_a is None:
        @pl.when(last)
        def _():
            o_ref[...] = x_ref[...] + gate * acc[...]
    else:
        @pl.when(last & (i < n_a))
        def _():
            oa_ref[...] = x_ref[...] + gate * acc[...]

        @pl.when(last & (i >= n_a))
        def _():
            ob_ref[...] = x_ref[...] + gate * acc[...]


def _ffn(x, mod4, g2, w_in, w_out, layer, cidx, tm, th, split_rows=None):
    m, d = x.shape
    hid = w_out.shape[1]
    nj = hid // th
    scratch = [pltpu.VMEM((tm, d), BF16)]
    if split_rows is None:
        n_a = None
        out_specs = pl.BlockSpec((tm, d), lambda i, j: (i, 0))
        out_shape = jax.ShapeDtypeStruct((m, d), F32)
    else:
        n_a = split_rows // tm
        out_specs = [pl.BlockSpec((tm, d), lambda i, j: (jnp.minimum(i, n_a - 1), 0)),
                     pl.BlockSpec((tm, d), lambda i, j: (jnp.maximum(i - n_a, 0), 0))]
        out_shape = [jax.ShapeDtypeStruct((split_rows, d), F32), jax.ShapeDtypeStruct((m - split_rows, d), F32)]
        scratch.append(pltpu.VMEM((tm, d), F32))
    return pl.pallas_call(
        functools.partial(_ffn_kernel, d=d, n_a=n_a),
        grid=(m // tm, nj),
        in_specs=[pl.BlockSpec((tm, d), lambda i, j: (i, 0)),
                  _mod_spec(layer, cidx, 6 * d, 2),
                  pl.BlockSpec((1, 1, d), lambda i, j: (layer, 0, 0)),
                  pl.BlockSpec((1, d, th), lambda i, j: (layer, 0, j)),
                  pl.BlockSpec((1, d, th), lambda i, j: (layer, 0, j + nj)),
                  pl.BlockSpec((1, th, d), lambda i, j: (layer, j, 0))],
        out_specs=out_specs,
        out_shape=out_shape,
        scratch_shapes=scratch,
        compiler_params=_cparams(("arbitrary", "arbitrary")),
        name="ffn",
    )(x, mod4, g2.reshape(g2.shape[0], 1, d), w_in, w_in, w_out)


def _proj_resid_kernel(*refs, d, n_a):
    if n_a is None:
        x_ref, y_ref, mod_ref, w_ref, b_ref, o_ref, w_sc = refs
        x = x_ref[...]
    else:
        xa_ref, xb_ref, y_ref, mod_ref, w_ref, b_ref, o_ref, w_sc = refs
        x = jnp.where(pl.program_id(0) < n_a, xa_ref[...], xb_ref[...])

    @pl.when(pl.program_id(0) == 0)
    def _():
        w_sc[...] = w_ref[0].astype(BF16)

    gate = mod_ref[0, 0, :, 2 * d:3 * d]
    r = jnp.dot(y_ref[...], w_sc[...], preferred_element_type=F32) + b_ref[0]
    o_ref[...] = x + gate * r


def _proj_resid(x, y, mod4, w, b, layer, widx, cidx, tm, npr):
    m, k = y.shape
    if isinstance(x, tuple):
        d = x[0].shape[1]
        n_a = npr // tm
        x_specs = [pl.BlockSpec((tm, d), lambda i: (jnp.minimum(i, n_a - 1), 0)),
                   pl.BlockSpec((tm, d), lambda i: (jnp.maximum(i - n_a, 0), 0))]
        x_args = list(x)
    else:
        d = x.shape[1]
        n_a = None
        x_specs = [pl.BlockSpec((tm, d), lambda i: (i, 0))]
        x_args = [x]
    return pl.pallas_call(
        functools.partial(_proj_resid_kernel, d=d, n_a=n_a),
        grid=(m // tm,),
        in_specs=x_specs + [pl.BlockSpec((tm, k), lambda i: (i, 0)),
                            _mod_spec(layer, cidx, 6 * d, 1),
                            pl.BlockSpec((1, k, d), lambda i: (widx, 0, 0)),
                            pl.BlockSpec((1, 1, d), lambda i: (widx, 0, 0))],
        out_specs=pl.BlockSpec((tm, d), lambda i: (i, 0)),
        out_shape=jax.ShapeDtypeStruct((m, d), F32),
        scratch_shapes=[pltpu.VMEM((k, d), BF16)],
        compiler_params=_cparams(("arbitrary",)),
        name="proj_resid",
    )(*x_args, y, mod4, w, b.reshape(b.shape[0], 1, d))


def _conv_a_kernel(x_ref, mod_ref, g_ref, w_ref, b_ref, o_ref, w_sc, *, d):
    @pl.when(pl.program_id(0) == 0)
    def _():
        w_sc[...] = w_ref[0].astype(BF16)

    h = _normmod(x_ref[...], g_ref[0], mod_ref[0, 0, :, 0:d], mod_ref[0, 0, :, d:2 * d]).astype(BF16)
    y = jnp.dot(h, w_sc[...], preferred_element_type=F32) + b_ref[0]
    u = y[:, :d] * jax.nn.sigmoid(y[:, d:])
    for cb in range(d // LANES):
        o_ref[cb] = u[:, cb * LANES:(cb + 1) * LANES]


def _conv_b_kernel(*refs, d, t, rc, top):
    (x_ref, u_ref, mod_ref, wdw_ref, bdw_ref, gln_ref, bln_ref, w2_ref, b2_ref) = refs[:9]
    o_ref, upad, conv_sc, y_sc, w_sc = refs[-5:]
    nb = d // LANES
    half = (CONV_WIDTH - 1) // 2

    @pl.when(pl.program_id(0) == 0)
    def _():
        w_sc[...] = w2_ref[0].astype(BF16)
        upad[:, 0:top, :] = jnp.zeros((nb, top, LANES), F32)
        upad[:, top + t:2 * top + t, :] = jnp.zeros((nb, top, LANES), F32)

    upad[:, top:top + t, :] = u_ref[...]

    def conv_block(cb, carry):
        for r0 in range(0, t, rc):
            acc = jnp.broadcast_to(bdw_ref[cb], (rc, LANES))
            for k in range(CONV_WIDTH):
                acc = acc + wdw_ref[cb, pl.ds(k, 1), :] * upad[cb, pl.ds(r0 + top - half + k, rc), :]
            conv_sc[cb, pl.ds(r0, rc), :] = acc
        return carry

    lax.fori_loop(0, nb, conv_block, 0)

    rl = min(t, 256)
    for r0 in range(0, t, rl):
        s = jnp.zeros((rl, 1), F32)
        for cb in range(nb):
            s = s + jnp.sum(conv_sc[cb, r0:r0 + rl, :], axis=-1, keepdims=True)
        mean = s * (1.0 / d)
        v = jnp.zeros((rl, 1), F32)
        for cb in range(nb):
            c = conv_sc[cb, r0:r0 + rl, :] - mean
            v = v + jnp.sum(c * c, axis=-1, keepdims=True)
        rstd = lax.rsqrt(v * (1.0 / d) + EPS)
        for cb in range(nb):
            sl = slice(cb * LANES, (cb + 1) * LANES)
            y = (conv_sc[cb, r0:r0 + rl, :] - mean) * rstd * gln_ref[:, sl] + bln_ref[:, sl]
            y_sc[r0:r0 + rl, sl] = _silu(y).astype(BF16)

    gate = mod_ref[0, 0, :, 2 * d:3 * d]
    r = jnp.dot(y_sc[...], w_sc[...], preferred_element_type=F32) + b2_ref[0]
    o_ref[...] = x_ref[...] + gate * r


def _conv_b(x, u, mod4, wdw, bdw, gln, bln, w2, b2, layer, j, t, row0, nseq, cond0, cond_step, obuf):
    m, d = x.shape
    nb = d // LANES
    top = 16
    blk0 = row0 // t
    in_specs = [pl.BlockSpec((t, d), lambda i: (blk0 + i, 0)),
                pl.BlockSpec((nb, t, LANES), lambda i: (0, blk0 + i, 0)),
                pl.BlockSpec((1, 1, 1, 6 * d), lambda i: (layer, cond0 + cond_step * i, 0, 0)),
                pl.BlockSpec((nb, 32, LANES), lambda i: (0, 0, 0)),
                pl.BlockSpec((nb, 1, LANES), lambda i: (0, 0, 0)),
                pl.BlockSpec((1, d), lambda i: (0, 0)),
                pl.BlockSpec((1, d), lambda i: (0, 0)),
                pl.BlockSpec((1, d, d), lambda i: (j, 0, 0)),
                pl.BlockSpec((1, 1, d), lambda i: (j, 0, 0))]
    args = [x, u, mod4, wdw, bdw, gln, bln, w2, b2]
    aliases = {}
    if obuf is not None:
        in_specs.append(pl.BlockSpec(memory_space=pl.ANY))
        args.append(obuf)
        aliases = {len(args) - 1: 0}
    return pl.pallas_call(
        functools.partial(_conv_b_kernel, d=d, t=t, rc=64, top=top),
        grid=(nseq,),
        in_specs=in_specs,
        out_specs=pl.BlockSpec((t, d), lambda i: (blk0 + i, 0)),
        out_shape=jax.ShapeDtypeStruct((m, d), F32),
        input_output_aliases=aliases,
        scratch_shapes=[pltpu.VMEM((nb, t + 2 * top, LANES), F32),
                        pltpu.VMEM((nb, t, LANES), F32),
                        pltpu.VMEM((t, d), BF16),
                        pltpu.VMEM((d, d), BF16)],
        compiler_params=_cparams(("arbitrary",)),
        name="conv_b",
    )(*args)


def _conv_layer(x, mod4, P, layer, j, npr, seq_p, seq_s):
    m, d = x.shape
    nb = d // LANES
    tm = 256
    cidx = _cond_index(tm, npr, seq_s)
    u = pl.pallas_call(
        functools.partial(_conv_a_kernel, d=d),
        grid=(m // tm,),
        in_specs=[pl.BlockSpec((tm, d), lambda i: (i, 0)),
                  _mod_spec(layer, cidx, 6 * d, 1),
                  pl.BlockSpec((1, 1, d), lambda i: (layer, 0, 0)),
                  pl.BlockSpec((1, d, 2 * d), lambda i: (j, 0, 0)),
                  pl.BlockSpec((1, 1, 2 * d), lambda i: (j, 0, 0))],
        out_specs=pl.BlockSpec((nb, tm, LANES), lambda i: (0, i, 0)),
        out_shape=jax.ShapeDtypeStruct((nb, m, LANES), F32),
        scratch_shapes=[pltpu.VMEM((d, 2 * d), BF16)],
        compiler_params=_cparams(("arbitrary",)),
        name="conv_a",
    )(x, mod4, P["g_norm1"].reshape(-1, 1, d), P["cv_w_pw1"], P["cv_b_pw1"].reshape(-1, 1, 2 * d))
    wdw = jnp.pad(P["cv_w_dw"][j], ((0, 32 - CONV_WIDTH), (0, 0))).reshape(32, nb, LANES).transpose(1, 0, 2)
    bdw = P["cv_b_dw"][j].reshape(nb, 1, LANES)
    gln = P["cv_g_ln"][j].reshape(1, d)
    bln = P["cv_b_ln"][j].reshape(1, d)
    b2 = P["cv_b_pw2"].reshape(-1, 1, d)
    args = (x, u, mod4, wdw, bdw, gln, bln, P["cv_w_pw2"], b2, layer, j)
    op = _conv_b(*args, seq_p, 0, npr // seq_p, 0, 0, None)
    return _conv_b(*args, seq_s, npr, (m - npr) // seq_s, 1, 1, op)


_LOG2E = 1.4426950408889634
_ROT = QK_ROPE // 4


def _swap_index():
    idx = list(range(HEAD_PAD))
    for base in (QK_NOPE, QK_NOPE + 2 * _ROT):
        for l in range(_ROT):
            idx[base + l], idx[base + _ROT + l] = base + _ROT + l, base + l
    return jnp.asarray(idx, jnp.int32)


def _swap_cols(w):
    lane = jnp.arange(HEAD_PAD)
    rot = (lane >= QK_NOPE) & (lane < QK_DIM)
    return jnp.where(rot, jnp.take(w, _swap_index(), axis=-1), 0.0)


def _rope_tables(seq, gqn, gkn):
    rows = seq // GRID_W
    row = jnp.repeat(jnp.arange(rows, dtype=F32), GRID_W)
    col = jnp.tile(jnp.arange(GRID_W, dtype=F32), rows)
    axis = QK_ROPE // 2
    inv = ROPE_BASE ** (-jnp.arange(0, axis, 2, dtype=F32) / axis)
    ar, ac = row[:, None] * inv, col[:, None] * inv
    cr, sr, cc, sc = jnp.cos(ar), jnp.sin(ar), jnp.cos(ac), jnp.sin(ac)
    one = jnp.ones((seq, QK_NOPE), F32)
    zn = jnp.zeros((seq, QK_NOPE), F32)
    zp = jnp.zeros((seq, HEAD_PAD - QK_DIM), F32)
    c = jnp.concatenate([one, cr, cr, cc, cc, zp], axis=1)
    s = jnp.concatenate([zn, -sr, sr, -sc, sc, zp], axis=1)
    idx = _swap_index()
    return tuple(t for g in (gqn, gkn) for t in (c * g, s * jnp.take(g, idx, axis=-1)))


def _head_rstd(xh):
    ms = jnp.sum(xh * xh, axis=-1, keepdims=True) * (1.0 / QK_DIM)
    return lax.rsqrt(ms + EPS)


def _expand_kv(ckv_b, kpe, wuk_ref, wv_ref, gk, rot, k_ref, v_ref, v_transposed):
    kk = jnp.dot(ckv_b, wuk_ref[...], preferred_element_type=F32)
    if v_transposed:
        v = lax.dot_general(wv_ref[...], ckv_b, _NT, preferred_element_type=F32)
    else:
        v = jnp.dot(ckv_b, wv_ref[...], preferred_element_type=F32)
    v_ref[...] = v.astype(BF16)
    for h in range(MLA_HEADS):
        sl = slice(h * HEAD_PAD, (h + 1) * HEAD_PAD)
        kh = kk[:, sl] + kpe
        r = _head_rstd(kh)
        kh = (kh * gk) * r if rot is None else (kh * rot[0] + rot[1]) * r
        k_ref[:, sl] = kh.astype(BF16)


def _mla_proj_kernel(*refs, d, rope):
    (x_ref, mod_ref, g_ref, wcat_ref, gq_ref, wuq_ref, gkv_ref, wuk_ref, wv_ref, gqn_ref, gkn_ref) = refs[:11]
    if rope:
        cq_ref, sq_ref, ck_ref, sk_ref = refs[11:15]
        q_ref, k_ref, v_ref = refs[15:18]
    else:
        q_ref, k_ref, v_ref, ckv_ref, kpe_ref = refs[11:16]
    h = _normmod(x_ref[...], g_ref[0], mod_ref[0, 0, :, 0:d], mod_ref[0, 0, :, d:2 * d]).astype(BF16)
    t = jnp.dot(h, wcat_ref[...], preferred_element_type=F32)
    nq = gq_ref.shape[1]
    nkv = gkv_ref.shape[1]
    hq = MLA_HEADS * HEAD_PAD
    qd, craw, kpe = t[:, :nq], t[:, nq:nq + nkv], t[:, nq + nkv:nq + nkv + HEAD_PAD]
    cq = (qd * lax.rsqrt(jnp.mean(qd * qd, axis=-1, keepdims=True) + EPS) * gq_ref[...]).astype(BF16)
    ckv = craw * lax.rsqrt(jnp.mean(craw * craw, axis=-1, keepdims=True) + EPS) * gkv_ref[...]
    if not rope:
        ckv_ref[...] = ckv
        kpe_ref[...] = kpe
    q = jnp.dot(cq, wuq_ref[...], preferred_element_type=F32)
    if rope:
        ct, st = cq_ref[...], sq_ref[...]
    else:
        gqn = gqn_ref[...]
    for hd in range(MLA_HEADS):
        sl = slice(hd * HEAD_PAD, (hd + 1) * HEAD_PAD)
        qh = q[:, sl]
        r = _head_rstd(qh)
        qh = (qh * ct + q[:, hq + hd * HEAD_PAD:hq + (hd + 1) * HEAD_PAD] * st) * r if rope else (qh * gqn) * r
        q_ref[:, sl] = qh.astype(BF16)
    rot = None
    if rope:
        kpe2 = t[:, nq + nkv + HEAD_PAD:]
        rot = (ck_ref[...], kpe2 * sk_ref[...])
    _expand_kv(ckv.astype(BF16), kpe, wuk_ref, wv_ref, gkn_ref[...], rot, k_ref, v_ref, rope)


def _kv_ctx_kernel(ckv_ref, kpe_ref, wuk_ref, wv_ref, gkn_ref, k_ref, v_ref):
    _expand_kv(ckv_ref[...].astype(BF16), kpe_ref[...], wuk_ref, wv_ref, gkn_ref[...], None, k_ref, v_ref, True)


_NT = (((1,), (1,)), ((), ()))


def _attn_prompt_kernel(q_ref, k_ref, v_ref, o_ref):
    t = q_ref.shape[0]
    lane = lax.broadcasted_iota(jnp.int32, (t, LANES), 1)
    for hp in range(MLA_HEADS // 2):
        vp = v_ref[:, hp * LANES:(hp + 1) * LANES]
        outs = []
        for hh in range(2):
            sl = slice((2 * hp + hh) * HEAD_PAD, (2 * hp + hh + 1) * HEAD_PAD)
            s = lax.dot_general(q_ref[:, sl], k_ref[:, sl], _NT, preferred_element_type=F32)
            e = jnp.exp2(s - s.max(axis=-1, keepdims=True))
            l = jnp.sum(e, axis=-1, keepdims=True)
            outs.append(jnp.dot(e.astype(BF16), vp, preferred_element_type=F32) * (1.0 / l))
        o_ref[:, hp * LANES:(hp + 1) * LANES] = jnp.where(lane < V_HEAD, outs[0], outs[1]).astype(BF16)


def _attn_sample_kernel(q_ref, kc_ref, vtc_ref, kl_ref, vtl_ref, obuf_ref, o_ref):
    del obuf_ref
    nh = q_ref.shape[1] // HEAD_PAD
    row = lax.broadcasted_iota(jnp.int32, (LANES, 1), 0)

    def scores(hh):
        sl = slice(hh * HEAD_PAD, (hh + 1) * HEAD_PAD)
        qh = q_ref[:, sl]
        return [lax.dot_general(k_ref[:, sl], qh, _NT, preferred_element_type=F32)
                for k_ref in (kc_ref, kl_ref)]

    halves = []
    nxt = scores(0)
    for hh in range(nh):
        sts = nxt
        if hh + 1 < nh:
            nxt = scores(hh + 1)
        m = jnp.max(sts[0], axis=0, keepdims=True)
        for st in sts[1:]:
            m = jnp.maximum(m, jnp.max(st, axis=0, keepdims=True))
        pr = slice((hh // 2) * LANES, (hh // 2 + 1) * LANES)
        own = (row < V_HEAD) if hh % 2 == 0 else (row >= V_HEAD)
        acc = None
        for st, vt_ref in zip(sts, (vtc_ref, vtl_ref)):
            e = jnp.exp2(st - m).astype(BF16)
            vt = vt_ref[pr, :]
            lhs = jnp.where(own, vt, jnp.ones_like(vt))
            pv = jnp.dot(lhs, e, preferred_element_type=F32)
            acc = pv if acc is None else acc + pv
        if hh % 2 == 0:
            halves.append(acc[0:V_HEAD] * (1.0 / acc[V_HEAD:V_HEAD + 1]))
        else:
            halves.append(acc[V_HEAD:2 * V_HEAD] * (1.0 / acc[0:1]))
    o_ref[...] = jnp.concatenate(halves, axis=0).T.astype(BF16)


def _mla_weights(P, j):
    d = P["mla_w_dq"].shape[1]
    nq = P["mla_w_dq"].shape[2]
    nkv = P["mla_g_kv"].shape[1]
    pad = HEAD_PAD - QK_DIM
    w_dkv = P["mla_w_dkv"][j]
    w_kpe = jnp.pad(w_dkv[:, nkv:], ((0, 0), (QK_NOPE, pad)))
    wcat = jnp.concatenate([P["mla_w_dq"][j], w_dkv[:, :nkv], w_kpe], axis=1).astype(BF16)
    wuq = jnp.pad(P["mla_w_uq"][j].reshape(nq, MLA_HEADS, QK_DIM), ((0, 0), (0, 0), (0, pad)))
    wuq = wuq.reshape(nq, MLA_HEADS * HEAD_PAD).astype(BF16)
    wukv = P["mla_w_ukv"][j].reshape(nkv, MLA_HEADS, QK_NOPE + V_HEAD)
    wuk = jnp.pad(wukv[:, :, :QK_NOPE], ((0, 0), (0, 0), (0, HEAD_PAD - QK_NOPE)))
    wuk = wuk.reshape(nkv, MLA_HEADS * HEAD_PAD).astype(BF16)
    wv = wukv[:, :, QK_NOPE:].reshape(nkv, MLA_HEADS * V_HEAD).astype(BF16)
    gq = P["mla_g_q"][j].reshape(1, nq)
    gkv = P["mla_g_kv"][j].reshape(1, nkv)
    gqn = (jnp.pad(P["mla_g_qn"][j], (0, pad)) * (QK_DIM ** -0.5 * _LOG2E)).reshape(1, HEAD_PAD)
    gkn = jnp.pad(P["mla_g_kn"][j], (0, pad)).reshape(1, HEAD_PAD)
    wuq2 = _swap_cols(wuq.reshape(nq, MLA_HEADS, HEAD_PAD)).reshape(nq, MLA_HEADS * HEAD_PAD)
    wuq_r = jnp.concatenate([wuq, wuq2], axis=1)
    wcat_r = jnp.concatenate([wcat, _swap_cols(w_kpe).astype(BF16)], axis=1)
    return (wcat, gq, wuq, gkv, wuk, wv, gqn, gkn), (wcat_r, wuq_r, wv.T)


def _full(shape):
    return pl.BlockSpec(shape, lambda *_: (0,) * len(shape))


def _mla_proj(x, xblk0, mod4, g1, W, layer, row0, nrows, cidx, tm, rope_args):
    d = x.shape[1]
    wcat, gq, wuq, gkv, wuk, wv, gqn, gkn = W
    blk0 = row0 // tm
    hq = MLA_HEADS * HEAD_PAD
    hv = MLA_HEADS * V_HEAD
    nkv = gkv.shape[1]
    rope = rope_args is not None
    in_specs = [pl.BlockSpec((tm, d), lambda i: (xblk0 + i, 0)),
                pl.BlockSpec((1, 1, 1, 6 * d), lambda i: (layer, cidx(blk0 + i), 0, 0)),
                pl.BlockSpec((1, 1, d), lambda i: (layer, 0, 0)),
                _full(wcat.shape), _full(gq.shape), _full(wuq.shape), _full(gkv.shape), _full(wuk.shape),
                _full(wv.shape), _full(gqn.shape), _full(gkn.shape)]
    args = [x, mod4, g1, wcat, gq, wuq, gkv, wuk, wv, gqn, gkn]
    out_specs = [pl.BlockSpec((tm, hq), lambda i: (i, 0)),
                 pl.BlockSpec((tm, hq), lambda i: (i, 0))]
    out_shape = [jax.ShapeDtypeStruct((nrows, hq), BF16),
                 jax.ShapeDtypeStruct((nrows, hq), BF16)]
    if rope:
        wcat_r, wuq_r, wvt, tabs = rope_args
        per = tabs[0].shape[0] // tm
        in_specs[3], in_specs[5], in_specs[8] = _full(wcat_r.shape), _full(wuq_r.shape), _full(wvt.shape)
        args[3], args[5], args[8] = wcat_r, wuq_r, wvt
        in_specs += [pl.BlockSpec((tm, HEAD_PAD), lambda i: (i % per, 0))] * 4
        args += list(tabs)
        out_specs += [pl.BlockSpec((hv, tm), lambda i: (0, i))]
        out_shape += [jax.ShapeDtypeStruct((hv, nrows), BF16)]
    else:
        out_specs += [pl.BlockSpec((tm, hv), lambda i: (i, 0)),
                      pl.BlockSpec((tm, nkv), lambda i: (i, 0)), pl.BlockSpec((tm, HEAD_PAD), lambda i: (i, 0))]
        out_shape += [jax.ShapeDtypeStruct((nrows, hv), BF16),
                      jax.ShapeDtypeStruct((nrows, nkv), F32), jax.ShapeDtypeStruct((nrows, HEAD_PAD), F32)]
    return pl.pallas_call(
        functools.partial(_mla_proj_kernel, d=d, rope=rope),
        grid=(nrows // tm,),
        in_specs=in_specs, out_specs=out_specs, out_shape=out_shape,
        compiler_params=_cparams(("arbitrary",)),
        name="mla_proj_rope" if rope else "mla_proj",
    )(*args)


def _mla_layer(x, mod4, P, layer, j, cache_ckv, cache_kpe, npr, seq_p, seq_s):
    xa, xb = x if isinstance(x, tuple) else (x, x)
    d = xa.shape[1]
    nrs = xb.shape[0] if isinstance(x, tuple) else xb.shape[0] - npr
    m = npr + nrs
    nbp, nbs = npr // seq_p, nrs // seq_s
    past = cache_ckv.shape[1]
    W, (wcat_r, wuq_r, wvt) = _mla_weights(P, j)
    wuk, gqn, gkn = W[4], W[6], W[7]
    g1 = P["g_norm1"].reshape(-1, 1, d)
    hq = MLA_HEADS * HEAD_PAD
    hv = MLA_HEADS * V_HEAD
    tm = 256
    cidx = _cond_index(tm, npr, seq_s)
    xb0 = 0 if isinstance(x, tuple) else npr // tm
    tabs = _rope_tables(seq_s, gqn[0], gkn[0])
    qp, kp, vp, ckv, kpe128 = _mla_proj(xa, 0, mod4, g1, W, layer, 0, npr, cidx, tm, None)
    qs, ks, vts = _mla_proj(xb, xb0, mod4, g1, W, layer, npr, nrs, cidx, tm, (wcat_r, wuq_r, wvt, tabs))
    cc = cache_ckv.reshape(nbs * past, -1)
    ck = jnp.pad(cache_kpe.reshape(nbs * past, -1), ((0, 0), (QK_NOPE, HEAD_PAD - QK_DIM)))
    kc, vtc = pl.pallas_call(
        _kv_ctx_kernel,
        grid=(nbs,),
        in_specs=[pl.BlockSpec((past, cc.shape[1]), lambda i: (i, 0)),
                  pl.BlockSpec((past, HEAD_PAD), lambda i: (i, 0)),
                  _full(wuk.shape), _full(wvt.shape), _full(gkn.shape)],
        out_specs=[pl.BlockSpec((past, hq), lambda i: (i, 0)), pl.BlockSpec((hv, past), lambda i: (0, i))],
        out_shape=[jax.ShapeDtypeStruct((nbs * past, hq), BF16), jax.ShapeDtypeStruct((hv, nbs * past), BF16)],
        compiler_params=_cparams(("arbitrary",)),
        name="mla_kv_ctx",
    )(cc, ck, wuk, wvt, gkn)
    op = pl.pallas_call(
        _attn_prompt_kernel,
        grid=(nbp,),
        in_specs=[pl.BlockSpec((seq_p, hq), lambda b: (b, 0)),
                  pl.BlockSpec((seq_p, hq), lambda b: (b, 0)),
                  pl.BlockSpec((seq_p, hv), lambda b: (b, 0))],
        out_specs=pl.BlockSpec((seq_p, hv), lambda b: (b, 0)),
        out_shape=jax.ShapeDtypeStruct((m, hv), BF16),
        compiler_params=_cparams(("arbitrary",)),
        name="attn_prompt",
    )(qp, kp, vp)
    tq = min(seq_s, 512)
    nqt = seq_s // tq
    nh = 4
    hw, vw = nh * HEAD_PAD, nh * V_HEAD
    o0 = npr // tq
    o = pl.pallas_call(
        _attn_sample_kernel,
        grid=(nbs, nqt, MLA_HEADS // nh),
        in_specs=[pl.BlockSpec((tq, hw), lambda b, qi, hp: (b * nqt + qi, hp)),
                  pl.BlockSpec((past, hw), lambda b, qi, hp: (b, hp)),
                  pl.BlockSpec((vw, past), lambda b, qi, hp: (hp, b)),
                  pl.BlockSpec((seq_s, hw), lambda b, qi, hp: (b, hp)),
                  pl.BlockSpec((vw, seq_s), lambda b, qi, hp: (hp, b)),
                  pl.BlockSpec(memory_space=pl.ANY)],
        out_specs=pl.BlockSpec((tq, vw), lambda b, qi, hp: (o0 + b * nqt + qi, hp)),
        out_shape=jax.ShapeDtypeStruct((m, hv), BF16),
        input_output_aliases={5: 0},
        compiler_params=_cparams(("arbitrary", "arbitrary", "arbitrary")),
        name="attn_sample",
    )(qs, kc, vtc, ks, vts, op)
    zb = jnp.zeros((P["mla_w_o"].shape[0], d), F32)
    tmr = min(512, seq_s)
    xn = _proj_resid(x, o, mod4, P["mla_w_o"], zb, layer, j, _cond_index(tmr, npr, seq_s), tmr, npr)
    new_ckv = ckv.reshape(nbp, seq_p, -1)
    new_kpe = kpe128[:, QK_NOPE:QK_DIM].reshape(nbp, seq_p, QK_ROPE)
    return xn, new_ckv, new_kpe


SSD_R = 8
SSD_GW = SSD_R * SSD_HEADDIM
SSD_NX = SSD_GW // LANES
SSD_DTW = 2 * SSD_R
SSD_TOP = 8


def _ssd_in_kernel(x_ref, mod_ref, g_ref, w_ref, wdt_ref, zx_ref, dt_ref, h_sc, *, d, heads):
    @pl.when(pl.program_id(1) == 0)
    def _():
        h = _normmod(x_ref[...], g_ref[0], mod_ref[0, 0, :, 0:d], mod_ref[0, 0, :, d:2 * d]).astype(BF16)
        h_sc[...] = h
        row = lax.broadcasted_iota(jnp.int32, wdt_ref.shape[1:], 0)
        wdt = jnp.where(row < 2 * heads, wdt_ref[0], 0.0).astype(BF16)
        dtt = lax.dot_general(wdt, h, _NT, preferred_element_type=F32)
        for g in range(SSD_GROUPS):
            dt_ref[g, 0:SSD_R, :] = dtt[g * SSD_R:(g + 1) * SSD_R, :]
            dt_ref[g, SSD_R:SSD_DTW, :] = dtt[heads + g * SSD_R:heads + (g + 1) * SSD_R, :]

    r = lax.dot_general(h_sc[...], w_ref[0].astype(BF16), _NT, preferred_element_type=F32)
    for c in range(r.shape[1] // LANES):
        zx_ref[c] = r[:, c * LANES:(c + 1) * LANES]


def _split3(x):
    hi = x.astype(BF16)
    r1 = x - hi.astype(F32)
    mid = r1.astype(BF16)
    lo = (r1 - mid.astype(F32)).astype(BF16)
    return hi, mid, lo


def _dot3_right(x, tri):
    return sum(jnp.dot(p, tri, preferred_element_type=F32) for p in _split3(x))


SSD_NPIECE = 7


def _ssd_spread_matrix():
    sel = np.zeros((2, CHUNK, SSD_R * LANES + 2 * SSD_GW), np.float32)
    for dr in range(2):
        for r in range(SSD_R):
            h = dr * SSD_R + r
            for p in range(3):
                sel[dr, p * SSD_DTW + h, r * LANES:(r + 1) * LANES] = 1.0
            for p in range(3, 5):
                sel[dr, p * SSD_DTW + h, SSD_R * LANES + r * SSD_HEADDIM:SSD_R * LANES + (r + 1) * SSD_HEADDIM] = 1.0
            for p in range(5, 7):
                c0 = SSD_R * LANES + SSD_GW
                sel[dr, p * SSD_DTW + h, c0 + r * SSD_HEADDIM:c0 + (r + 1) * SSD_HEADDIM] = 1.0
    return jnp.asarray(sel, BF16)


def _ssd_core_kernel(*refs, t, has_h0):
    (z_ref, x_ref, b_ref, c_ref, dt_ref, wx_ref, wb_ref, wc_ref, bx_ref, bb_ref, bc_ref,
     dtb_ref, alog_ref, dsum_ref, gn_ref, bsel_ref) = refs[:16]
    if has_h0:
        h0_ref, y_ref = refs[16], refs[18]
        st_ref = None
        scr = refs[19:]
    else:
        y_ref, st_ref = refs[16:18]
        h0_ref = None
        scr = refs[18:]
    (pad_sc, xc_sc, bcv_sc, ccv_sc, bt_sc, cb_sc, y_sc, dtt_sc, dat_sc, s_sc) = scr
    nc = t // CHUNK
    half = (SSD_CONV - 1) // 2
    nblk = SSD_NX + 2

    @pl.when((pl.program_id(0) == 0) & (pl.program_id(1) == 0))
    def _():
        pad_sc[:, 0:SSD_TOP, :] = jnp.zeros((nblk, SSD_TOP, LANES), F32)
        pad_sc[:, SSD_TOP + t:2 * SSD_TOP + t, :] = jnp.zeros((nblk, SSD_TOP, LANES), F32)

    pad_sc[0:SSD_NX, SSD_TOP:SSD_TOP + t, :] = x_ref[...]
    pad_sc[SSD_NX, SSD_TOP:SSD_TOP + t, :] = b_ref[0]
    pad_sc[SSD_NX + 1, SSD_TOP:SSD_TOP + t, :] = c_ref[0]

    for blk in range(nblk):
        if blk < SSD_NX:
            w_r, bias = wx_ref, bx_ref[blk]
            widx = blk
        elif blk == SSD_NX:
            w_r, bias, widx = wb_ref, bb_ref[0], 0
        else:
            w_r, bias, widx = wc_ref, bc_ref[0], 0
        for c in range(nc):
            r0 = c * CHUNK
            acc = jnp.broadcast_to(bias, (CHUNK, LANES))
            for k in range(SSD_CONV):
                acc = acc + w_r[widx, k:k + 1, :] * pad_sc[blk, r0 + SSD_TOP - half + k:r0 + SSD_TOP - half + k + CHUNK, :]
            acc = _silu(acc)
            if blk < SSD_NX:
                xc_sc[blk, r0:r0 + CHUNK, :] = acc
            elif blk == SSD_NX:
                bcv_sc[r0:r0 + CHUNK, :] = acc.astype(BF16)
                bt_sc[c] = acc.T.astype(BF16)
            else:
                ccv_sc[r0:r0 + CHUNK, :] = acc.astype(BF16)

    a_row = -jnp.exp(alog_ref[0]) * _LOG2E
    for c in range(nc):
        v = dt_ref[0, :, c * CHUNK:(c + 1) * CHUNK] + dtb_ref[0]
        dt = jnp.maximum(v, 0.0) + jnp.log1p(jnp.exp(-jnp.abs(v)))
        dtt_sc[c] = dt
        dat_sc[c] = dt * a_row

    row = lax.broadcasted_iota(jnp.int32, (CHUNK, CHUNK), 0)
    colm = lax.broadcasted_iota(jnp.int32, (CHUNK, CHUNK), 1)
    lower = colm <= row
    upper = colm >= row
    tri_l = jnp.where(lower, 1.0, 0.0).astype(BF16)
    tri_u = jnp.where(upper, 1.0, 0.0).astype(BF16)
    lane = lax.broadcasted_iota(jnp.int32, (CHUNK, LANES), 1)
    head_row = lax.broadcasted_iota(jnp.int32, (SSD_DTW, 1), 0)
    zero16 = jnp.zeros((SSD_DTW, CHUNK), F32)

    col0, w0, e0 = 0, SSD_R * LANES, SSD_R * LANES + SSD_GW
    zpad = jnp.zeros((CHUNK - SSD_NPIECE * SSD_DTW, CHUNK), F32)

    def prepare(dr, c):
        lo = dr * SSD_R
        tri_t = tri_l if dr else tri_u
        own = (head_row >= lo) & (head_row < lo + SSD_R)
        rows = slice(c * CHUNK, (c + 1) * CHUNK)
        dtt = dtt_sc[c]
        cum_t = _dot3_right(dat_sc[c], tri_t)
        edge = cum_t[:, 0:1] if dr else cum_t[:, CHUNK - 1:CHUNK]
        w2_t = dtt * jnp.exp2(edge - cum_t)
        ecum_t = jnp.exp2(cum_t)
        pieces = (list(_split3(jnp.where(own, cum_t, zero16)))
                  + list(_split3(jnp.where(own, w2_t, zero16))[:2])
                  + list(_split3(jnp.where(own, ecum_t, zero16))[:2]))
        staged = jnp.concatenate([p.astype(F32) for p in pieces] + [zpad], axis=0)
        spread = jnp.dot(staged.T.astype(BF16), bsel_ref[dr], preferred_element_type=F32)
        cmat = ccv_sc[rows, :]
        if dr == 0:
            cb = lax.dot_general(cmat, bcv_sc[rows, :], _NT, preferred_element_type=F32)
            cb_sc[c] = cb
        else:
            cb = cb_sc[c]
        rowterm = cum_t - jnp.log2(dtt)
        return dict(dr=dr, c=c, rows=rows, lo=lo, rowterm=rowterm, spread=spread, cmat=cmat, cb=cb)

    def scan_chunk(pre):
        dr, c, rows, lo = pre["dr"], pre["c"], pre["rows"], pre["lo"]
        rowterm, spread, cmat, cb = pre["rowterm"], pre["spread"], pre["cmat"], pre["cb"]
        mask = upper if dr else lower
        s_prev = s_sc[...]
        yoff = jnp.dot(cmat, s_prev.astype(BF16), preferred_element_type=F32)
        xd = []
        for q in range(SSD_NX):
            xblk = xc_sc[q, rows, :]
            xb16 = xblk.astype(BF16)
            mh = []
            for r in (2 * q, 2 * q + 1):
                seg = spread[:, col0 + r * LANES:col0 + (r + 1) * LANES] - rowterm[lo + r:lo + r + 1, :]
                lm = jnp.exp2(jnp.where(mask, seg, -jnp.inf))
                mh.append((cb * lm).astype(BF16))
            rhs = jnp.concatenate([jnp.where(lane < SSD_HEADDIM, xb16, jnp.zeros_like(xb16)),
                                   jnp.where(lane >= SSD_HEADDIM, xb16, jnp.zeros_like(xb16))], axis=0)
            y = jnp.dot(jnp.concatenate(mh, axis=1), rhs, preferred_element_type=F32)
            sl = slice(q * LANES, (q + 1) * LANES)
            y = y + yoff[:, sl] * spread[:, e0 + q * LANES:e0 + (q + 1) * LANES]
            if dr == 0:
                y_sc[q, rows, :] = y + xblk * dsum_ref[0, :, sl]
            else:
                y_sc[q, rows, :] = y_sc[q, rows, :] + y
            xd.append((xblk * spread[:, w0 + q * LANES:w0 + (q + 1) * LANES]).astype(BF16))
        st = jnp.dot(bt_sc[c], jnp.concatenate(xd, axis=1), preferred_element_type=F32)
        er = 0 if dr else CHUNK - 1
        cd = spread[er:er + 1, e0:e0 + SSD_GW]
        s_sc[...] = s_prev * cd + st

    order = [(0, c) for c in range(nc)] + [(1, c) for c in range(nc - 1, -1, -1)]
    pre = prepare(*order[0])
    for k, (dr, c) in enumerate(order):
        if c == (nc - 1 if dr else 0):
            if has_h0:
                s_sc[...] = h0_ref[0, dr, 0].T
            else:
                s_sc[...] = jnp.zeros_like(s_sc)
        nxt = prepare(*order[k + 1]) if k + 1 < len(order) else None
        scan_chunk(pre)
        pre = nxt
        if st_ref is not None and c == (0 if dr else nc - 1):
            st_ref[0, dr, 0] = s_sc[...].T

    rl = min(t, 256)
    for r0 in range(0, t, rl):
        yz = [y_sc[q, r0:r0 + rl, :] * _silu(z_ref[q, r0:r0 + rl, :]) for q in range(SSD_NX)]
        ms = sum(jnp.sum(a * a, axis=-1, keepdims=True) for a in yz) * (1.0 / SSD_GW)
        rstd = lax.rsqrt(ms + EPS)
        for q in range(SSD_NX):
            sl = slice(q * LANES, (q + 1) * LANES)
            y_ref[r0:r0 + rl, sl] = (yz[q] * rstd * gn_ref[0, :, sl]).astype(BF16)


def _ssd_core(zx, dtc, cw, cbias, dtb, alog, dsum, gn, h0, ybuf, t, row0, nseq):
    has_h0 = h0 is not None
    m = zx.shape[1]
    bsel = _ssd_spread_matrix()
    b0 = row0 // t
    nc = t // CHUNK
    nz = SSD_GROUPS * SSD_NX
    in_specs = [pl.BlockSpec((SSD_NX, t, LANES), lambda b, g: (g, b0 + b, 0)),
                pl.BlockSpec((SSD_NX, t, LANES), lambda b, g: (SSD_GROUPS + g, b0 + b, 0)),
                pl.BlockSpec((1, t, LANES), lambda b, g: (2 * nz + g, b0 + b, 0)),
                pl.BlockSpec((1, t, LANES), lambda b, g: (2 * nz + SSD_GROUPS + g, b0 + b, 0)),
                pl.BlockSpec((1, SSD_DTW, t), lambda b, g: (g, 0, b0 + b)),
                pl.BlockSpec((SSD_NX, 8, LANES), lambda b, g: (g, 0, 0)),
                pl.BlockSpec((1, 8, LANES), lambda b, g: (nz + g, 0, 0)),
                pl.BlockSpec((1, 8, LANES), lambda b, g: (nz + SSD_GROUPS + g, 0, 0)),
                pl.BlockSpec((SSD_NX, 1, LANES), lambda b, g: (g, 0, 0)),
                pl.BlockSpec((1, 1, LANES), lambda b, g: (nz + g, 0, 0)),
                pl.BlockSpec((1, 1, LANES), lambda b, g: (nz + SSD_GROUPS + g, 0, 0)),
                pl.BlockSpec((1, SSD_DTW, CHUNK), lambda b, g: (g, 0, 0)),
                pl.BlockSpec((1, SSD_DTW, CHUNK), lambda b, g: (g, 0, 0)),
                pl.BlockSpec((1, 1, SSD_GW), lambda b, g: (g, 0, 0)),
                pl.BlockSpec((1, 1, SSD_GW), lambda b, g: (g, 0, 0)),
                pl.BlockSpec(bsel.shape, lambda b, g: (0, 0, 0))]
    args = [zx, zx, zx, zx, dtc, cw, cw, cw, cbias, cbias, cbias, dtb, alog, dsum, gn, bsel]
    st_spec = pl.BlockSpec((1, 2, 1, SSD_GW, SSD_STATE), lambda b, g: (b, 0, g, 0, 0))
    y_spec = pl.BlockSpec((t, SSD_GW), lambda b, g: (b0 + b, g))
    y_shape = jax.ShapeDtypeStruct((m, SSD_GROUPS * SSD_GW), BF16)
    aliases = {}
    if has_h0:
        in_specs += [st_spec, pl.BlockSpec(memory_space=pl.ANY)]
        args += [h0, ybuf]
        aliases = {len(args) - 1: 0}
        out_specs, out_shape = y_spec, y_shape
    else:
        out_specs = [y_spec, st_spec]
        out_shape = [y_shape, jax.ShapeDtypeStruct((nseq, 2, SSD_GROUPS, SSD_GW, SSD_STATE), F32)]
    scratch = [pltpu.VMEM((SSD_NX + 2, t + 2 * SSD_TOP, LANES), F32),
               pltpu.VMEM((SSD_NX, t, LANES), F32),
               pltpu.VMEM((t, LANES), BF16),
               pltpu.VMEM((t, LANES), BF16),
               pltpu.VMEM((nc, CHUNK, CHUNK), BF16),
               pltpu.VMEM((nc, CHUNK, CHUNK), F32),
               pltpu.VMEM((SSD_NX, t, LANES), F32),
               pltpu.VMEM((nc, SSD_DTW, CHUNK), F32),
               pltpu.VMEM((nc, SSD_DTW, CHUNK), F32),
               pltpu.VMEM((SSD_STATE, SSD_GW), F32)]
    return pl.pallas_call(
        functools.partial(_ssd_core_kernel, t=t, has_h0=has_h0),
        grid=(nseq, SSD_GROUPS),
        in_specs=in_specs, out_specs=out_specs, out_shape=out_shape,
        input_output_aliases=aliases,
        scratch_shapes=scratch,
        compiler_params=_cparams(("arbitrary", "arbitrary")),
        name="ssd_core_h0" if has_h0 else "ssd_core",
    )(*args)


def _ssd_layer(x, mod4, P, layer, j, st0, npr, seq_p, seq_s):
    m, d = x.shape
    nbp, nbs = npr // seq_p, (m - npr) // seq_s
    heads = SSD_GROUPS * SSD_R
    inner = heads * SSD_HEADDIM
    gn_w = SSD_GROUPS * SSD_STATE
    nzx = 2 * inner + 2 * gn_w
    tm, tn = min(1024, seq_s), 1024
    cidx = _cond_index(tm, npr, seq_s)
    w_in_t = jnp.swapaxes(P["ssd_w_in"], 1, 2)
    zx, dtc = pl.pallas_call(
        functools.partial(_ssd_in_kernel, d=d, heads=heads),
        grid=(m // tm, nzx // tn),
        in_specs=[pl.BlockSpec((tm, d), lambda i, n: (i, 0)),
                  _mod_spec(layer, cidx, 6 * d, 2),
                  pl.BlockSpec((1, 1, d), lambda i, n: (layer, 0, 0)),
                  pl.BlockSpec((1, tn, d), lambda i, n: (j, n, 0)),
                  pl.BlockSpec((1, LANES, d), lambda i, n: (j, nzx // LANES, 0))],
        out_specs=[pl.BlockSpec((tn // LANES, tm, LANES), lambda i, n: (n, i, 0)),
                   pl.BlockSpec((SSD_GROUPS, SSD_DTW, tm), lambda i, n: (0, 0, i))],
        out_shape=[jax.ShapeDtypeStruct((nzx // LANES, m, LANES), F32),
                   jax.ShapeDtypeStruct((SSD_GROUPS, SSD_DTW, m), F32)],
        scratch_shapes=[pltpu.VMEM((tm, d), BF16)],
        compiler_params=_cparams(("arbitrary", "arbitrary")),
        name="ssd_in",
    )(x, mod4, P["g_norm1"].reshape(-1, 1, d), w_in_t, w_in_t)

    def compact(a):
        a = a.reshape(2, SSD_GROUPS, SSD_R).transpose(1, 0, 2).reshape(SSD_GROUPS, SSD_DTW, 1)
        return jnp.broadcast_to(a, (SSD_GROUPS, SSD_DTW, CHUNK))

    ncb = (inner + 2 * gn_w) // LANES
    cw = jnp.pad(P["ssd_w_conv"][j], ((0, 8 - SSD_CONV), (0, 0))).reshape(8, ncb, LANES).transpose(1, 0, 2)
    cbias = P["ssd_b_conv"][j].reshape(ncb, 1, LANES)
    dsum = jnp.repeat(P["ssd_d"][j][0] + P["ssd_d"][j][1], SSD_HEADDIM).reshape(SSD_GROUPS, 1, SSD_GW)
    gn = P["ssd_g_norm"][j].reshape(SSD_GROUPS, 1, SSD_GW)
    common = (zx, dtc, cw, cbias, compact(P["ssd_dt_bias"][j]), compact(P["ssd_a_log"][j]), dsum, gn)
    yp, st = _ssd_core(*common, None, None, seq_p, 0, nbp)
    h0 = st0.reshape(nbs, 2, SSD_GROUPS, SSD_GW, SSD_STATE)
    y = _ssd_core(*common, h0, yp, seq_s, npr, nbs)
    zb = jnp.zeros((P["ssd_w_out"].shape[0], d), F32)
    tmr = min(512, seq_s)
    xn = _proj_resid(x, y, mod4, P["ssd_w_out"], zb, layer, j, _cond_index(tmr, npr, seq_s), tmr, npr)
    return xn, st.reshape(nbp, 2, heads, SSD_HEADDIM, SSD_STATE)


def kernel(x_prompt, x_sample, cache_ckv, cache_kpe, state_ssm, c, c_ctx, w_ada, b_ada, g_norm1, g_norm2,
           mla_w_dq, mla_g_q, mla_w_uq, mla_w_dkv, mla_g_kv, mla_w_ukv, mla_g_qn, mla_g_kn, mla_w_o,
           cv_w_pw1, cv_b_pw1, cv_w_dw, cv_b_dw, cv_g_ln, cv_b_ln, cv_w_pw2, cv_b_pw2,
           ssd_w_in, ssd_w_conv, ssd_b_conv, ssd_dt_bias, ssd_a_log, ssd_d, ssd_g_norm, ssd_w_out,
           ffn_w_in, ffn_w_out):
    P = dict(g_norm1=g_norm1, g_norm2=g_norm2,
             mla_w_dq=mla_w_dq, mla_g_q=mla_g_q, mla_w_uq=mla_w_uq, mla_w_dkv=mla_w_dkv, mla_g_kv=mla_g_kv,
             mla_w_ukv=mla_w_ukv, mla_g_qn=mla_g_qn, mla_g_kn=mla_g_kn, mla_w_o=mla_w_o,
             cv_w_pw1=cv_w_pw1, cv_b_pw1=cv_b_pw1, cv_w_dw=cv_w_dw, cv_b_dw=cv_b_dw, cv_g_ln=cv_g_ln,
             cv_b_ln=cv_b_ln, cv_w_pw2=cv_w_pw2, cv_b_pw2=cv_b_pw2,
             ssd_w_in=ssd_w_in, ssd_w_conv=ssd_w_conv, ssd_b_conv=ssd_b_conv, ssd_dt_bias=ssd_dt_bias,
             ssd_a_log=ssd_a_log, ssd_d=ssd_d, ssd_g_norm=ssd_g_norm, ssd_w_out=ssd_w_out)
    nbp, seq_p, d = x_prompt.shape
    nbs, seq_s, _ = x_sample.shape
    depth = w_ada.shape[0]
    npr = nbp * seq_p
    x = (x_prompt.reshape(npr, d), x_sample.reshape(nbs * seq_s, d))
    cond8 = jnp.concatenate([c_ctx[None, :], c, jnp.zeros((8 - 1 - nbs, d), F32)], axis=0)
    mod4 = _ada(cond8, w_ada, b_ada).reshape(depth, 8, 1, 6 * d)
    ffn_tm = min(1024, seq_s)
    ffn_cidx = _cond_index(ffn_tm, npr, seq_s)
    ckvs, kpes, ssms = [], [], []
    for i in range(depth):
        kind, j = i % N_MIXERS, i // N_MIXERS
        if kind != 0 and isinstance(x, tuple):
            x = jnp.concatenate(x, axis=0)
        if kind == 0:
            x, ckv, kpe = _mla_layer(x, mod4, P, i, j, cache_ckv[:, j], cache_kpe[:, j], npr, seq_p, seq_s)
            ckvs.append(ckv)
            kpes.append(kpe)
        elif kind == 1:
            x = _conv_layer(x, mod4, P, i, j, npr, seq_p, seq_s)
        else:
            x, st = _ssd_layer(x, mod4, P, i, j, state_ssm[:, j], npr, seq_p, seq_s)
            ssms.append(st)
        x = _ffn(x, mod4, g_norm2, ffn_w_in, ffn_w_out, i, ffn_cidx, ffn_tm, 256,
                 split_rows=npr if i == depth - 1 else None)
    return (x[0].reshape(nbp, seq_p, d), x[1].reshape(nbs, seq_s, d),
            jnp.stack(ckvs, axis=1), jnp.stack(kpes, axis=1), jnp.stack(ssms, axis=1))
```

```python
import functools

import jax
import jax.numpy as jnp
import numpy as np
from jax import lax
from jax.experimental import pallas as pl
from jax.experimental.pallas import tpu as pltpu

F32 = jnp.float32
BF16 = jnp.bfloat16

EPS = 1e-6
ROPE_BASE = 10000.0
GRID_W = 64
N_MIXERS = 3

LANES = 128
VMEM_LIMIT_BYTES = 56 * 1024 * 1024

MLA_HEADS = 16
QK_NOPE = 64
QK_ROPE = 32
QK_DIM = QK_NOPE + QK_ROPE
V_HEAD = 64
HEAD_PAD = LANES
CONV_WIDTH = 31
SSD_HEADDIM = 64
SSD_GROUPS = 4
SSD_STATE = 128
SSD_CONV = 5
CHUNK = 128


def _cparams(sem):
    return pltpu.CompilerParams(dimension_semantics=sem, vmem_limit_bytes=VMEM_LIMIT_BYTES)


def _silu(x):
    return x * jax.nn.sigmoid(x)


def _normmod(x, g, sh, sc):
    ms = jnp.mean(x * x, axis=-1, keepdims=True)
    return (x * lax.rsqrt(ms + EPS)) * (g * (1.0 + sc)) + sh


def _cond_index(tm, n_prompt_rows, sample_seq):
    def f(i):
        r = i * tm
        return jnp.where(r < n_prompt_rows, 0, 1 + (r - n_prompt_rows) // sample_seq)
    return f


def _mod_spec(layer, cidx, d6, ngrid):
    if ngrid == 1:
        return pl.BlockSpec((1, 1, 1, d6), lambda i: (layer, cidx(i), 0, 0))
    return pl.BlockSpec((1, 1, 1, d6), lambda i, j: (layer, cidx(i), 0, 0))


def _ada_kernel(c_ref, w_ref, b_ref, o_ref):
    s = _silu(c_ref[...]).astype(BF16)
    o_ref[0] = jnp.dot(s, w_ref[0].astype(BF16), preferred_element_type=F32) + b_ref[0]


def _ada(cond8, w_ada, b_ada):
    depth, d, n = w_ada.shape
    tn = n // 4
    return pl.pallas_call(
        _ada_kernel,
        grid=(depth, n // tn),
        in_specs=[pl.BlockSpec((8, d), lambda l, j: (0, 0)),
                  pl.BlockSpec((1, d, tn), lambda l, j: (l, 0, j)),
                  pl.BlockSpec((1, 1, tn), lambda l, j: (l, 0, j))],
        out_specs=pl.BlockSpec((1, 8, tn), lambda l, j: (l, 0, j)),
        out_shape=jax.ShapeDtypeStruct((depth, 8, n), F32),
        compiler_params=_cparams(("arbitrary", "arbitrary")),
        name="ada",
    )(cond8, w_ada, b_ada.reshape(depth, 1, n))


def _ffn_kernel(*refs, d, n_in_a, n_out_a, fused):
    refs = list(refs)
    xa_ref = refs.pop(0)
    xb_ref = refs.pop(0) if n_in_a is not None else None
    y_ref, wo1_ref = (refs.pop(0), refs.pop(0)) if fused else (None, None)
    mod_ref, g_ref, wa_ref, wu_ref, wo_ref = refs[:5]
    if n_out_a is None:
        o_ref, h_sc = refs[5:]
        acc = o_ref
    else:
        oa_ref, ob_ref, h_sc, acc = refs[5:]
    i, j = pl.program_id(0), pl.program_id(1)

    @pl.when(j == 0)
    def _():
        x = xa_ref[...] if xb_ref is None else jnp.where(i < n_in_a, xa_ref[...], xb_ref[...])
        if fused:
            x = x + mod_ref[0, 0, :, 2 * d:3 * d] * jnp.dot(y_ref[...], wo1_ref[...], preferred_element_type=F32)
        acc[...] = x
        h_sc[...] = _normmod(x, g_ref[0], mod_ref[0, 0, :, 3 * d:4 * d], mod_ref[0, 0, :, 4 * d:5 * d]).astype(BF16)

    h = h_sc[...]
    a = jnp.dot(h, wa_ref[0].astype(BF16), preferred_element_type=F32)
    u = jnp.dot(h, wu_ref[0].astype(BF16), preferred_element_type=F32)
    t = (_silu(a) * u).astype(BF16)
    acc[...] += mod_ref[0, 0, :, 5 * d:6 * d] * jnp.dot(t, wo_ref[0].astype(BF16), preferred_element_type=F32)
    if n_out_a is not None:
        last = j == pl.num_programs(1) - 1

        @pl.when(last & (i < n_out_a))
        def _():
            oa_ref[...] = acc[...]

        @pl.when(last & (i >= n_out_a))
        def _():
            ob_ref[...] = acc[...]


def _ffn(x, mod4, g2, w_in, w_out, layer, cidx, tm, th, npr, y=None, w_o=None, split=False):
    pair = isinstance(x, tuple)
    d = x[0].shape[1] if pair else x.shape[1]
    m = (x[0].shape[0] + x[1].shape[0]) if pair else x.shape[0]
    hid = w_out.shape[1]
    nj = hid // th
    n_a = npr // tm
    fused = y is not None
    if pair:
        in_specs = [pl.BlockSpec((tm, d), lambda i, j: (jnp.minimum(i, n_a - 1), 0)),
                    pl.BlockSpec((tm, d), lambda i, j: (jnp.maximum(i - n_a, 0), 0))]
        args = list(x)
    else:
        in_specs = [pl.BlockSpec((tm, d), lambda i, j: (i, 0))]
        args = [x]
    if fused:
        k = y.shape[1]
        in_specs += [pl.BlockSpec((tm, k), lambda i, j: (i, 0)), pl.BlockSpec((k, d), lambda i, j: (0, 0))]
        args += [y, w_o]
    in_specs += [_mod_spec(layer, cidx, 6 * d, 2),
                 pl.BlockSpec((1, 1, d), lambda i, j: (layer, 0, 0)),
                 pl.BlockSpec((1, d, th), lambda i, j: (layer, 0, j)),
                 pl.BlockSpec((1, d, th), lambda i, j: (layer, 0, j + nj)),
                 pl.BlockSpec((1, th, d), lambda i, j: (layer, j, 0))]
    args += [mod4, g2.reshape(g2.shape[0], 1, d), w_in, w_in, w_out]
    scratch = [pltpu.VMEM((tm, d), BF16)]
    if split:
        out_specs = [pl.BlockSpec((tm, d), lambda i, j: (jnp.minimum(i, n_a - 1), 0)),
                     pl.BlockSpec((tm, d), lambda i, j: (jnp.maximum(i - n_a, 0), 0))]
        out_shape = [jax.ShapeDtypeStruct((npr, d), F32), jax.ShapeDtypeStruct((m - npr, d), F32)]
        scratch.append(pltpu.VMEM((tm, d), F32))
    else:
        out_specs = pl.BlockSpec((tm, d), lambda i, j: (i, 0))
        out_shape = jax.ShapeDtypeStruct((m, d), F32)
    return pl.pallas_call(
        functools.partial(_ffn_kernel, d=d, n_in_a=n_a if pair else None, n_out_a=n_a if split else None,
                          fused=fused),
        grid=(m // tm, nj),
        in_specs=in_specs,
        out_specs=out_specs,
        out_shape=out_shape,
        scratch_shapes=scratch,
        compiler_params=_cparams(("arbitrary", "arbitrary")),
        name="ffn_proj" if fused else "ffn",
    )(*args)


def _conv_a_kernel(x_ref, mod_ref, g_ref, w_ref, b_ref, o_ref, w_sc, *, d):
    @pl.when(pl.program_id(0) == 0)
    def _():
        w_sc[...] = w_ref[0].astype(BF16)

    h = _normmod(x_ref[...], g_ref[0], mod_ref[0, 0, :, 0:d], mod_ref[0, 0, :, d:2 * d]).astype(BF16)
    y = jnp.dot(h, w_sc[...], preferred_element_type=F32) + b_ref[0]
    u = y[:, :d] * jax.nn.sigmoid(y[:, d:])
    for cb in range(d // LANES):
        o_ref[cb] = u[:, cb * LANES:(cb + 1) * LANES]


def _conv_b_kernel(*refs, d, t, rc, top):
    (x_ref, u_ref, mod_ref, wdw_ref, bdw_ref, gln_ref, bln_ref, w2_ref, b2_ref) = refs[:9]
    o_ref, upad, conv_sc, y_sc, w_sc = refs[-5:]
    nb = d // LANES
    half = (CONV_WIDTH - 1) // 2

    @pl.when(pl.program_id(0) == 0)
    def _():
        w_sc[...] = w2_ref[0].astype(BF16)
        upad[:, 0:top, :] = jnp.zeros((nb, top, LANES), F32)
        upad[:, top + t:2 * top + t, :] = jnp.zeros((nb, top, LANES), F32)

    upad[:, top:top + t, :] = u_ref[...]

    def conv_block(cb, carry):
        for r0 in range(0, t, rc):
            acc = jnp.broadcast_to(bdw_ref[cb], (rc, LANES))
            for k in range(CONV_WIDTH):
                acc = acc + wdw_ref[cb, pl.ds(k, 1), :] * upad[cb, pl.ds(r0 + top - half + k, rc), :]
            conv_sc[cb, pl.ds(r0, rc), :] = acc
        return carry

    lax.fori_loop(0, nb, conv_block, 0)

    rl = min(t, 256)
    for r0 in range(0, t, rl):
        s = jnp.zeros((rl, 1), F32)
        for cb in range(nb):
            s = s + jnp.sum(conv_sc[cb, r0:r0 + rl, :], axis=-1, keepdims=True)
        mean = s * (1.0 / d)
        v = jnp.zeros((rl, 1), F32)
        for cb in range(nb):
            c = conv_sc[cb, r0:r0 + rl, :] - mean
            v = v + jnp.sum(c * c, axis=-1, keepdims=True)
        rstd = lax.rsqrt(v * (1.0 / d) + EPS)
        for cb in range(nb):
            sl = slice(cb * LANES, (cb + 1) * LANES)
            y = (conv_sc[cb, r0:r0 + rl, :] - mean) * rstd * gln_ref[:, sl] + bln_ref[:, sl]
            y_sc[r0:r0 + rl, sl] = _silu(y).astype(BF16)

    gate = mod_ref[0, 0, :, 2 * d:3 * d]
    r = jnp.dot(y_sc[...], w_sc[...], preferred_element_type=F32) + b2_ref[0]
    o_ref[...] = x_ref[...] + gate * r


def _conv_b(x, u, mod4, wdw, bdw, gln, bln, w2, b2, layer, j, t, row0, nseq, cond0, cond_step, obuf):
    m, d = x.shape
    nb = d // LANES
    top = 16
    blk0 = row0 // t
    in_specs = [pl.BlockSpec((t, d), lambda i: (blk0 + i, 0)),
                pl.BlockSpec((nb, t, LANES), lambda i: (0, blk0 + i, 0)),
                pl.BlockSpec((1, 1, 1, 6 * d), lambda i: (layer, cond0 + cond_step * i, 0, 0)),
                pl.BlockSpec((nb, 32, LANES), lambda i: (0, 0, 0)),
                pl.BlockSpec((nb, 1, LANES), lambda i: (0, 0, 0)),
                pl.BlockSpec((1, d), lambda i: (0, 0)),
                pl.BlockSpec((1, d), lambda i: (0, 0)),
                pl.BlockSpec((1, d, d), lambda i: (j, 0, 0)),
                pl.BlockSpec((1, 1, d), lambda i: (j, 0, 0))]
    args = [x, u, mod4, wdw, bdw, gln, bln, w2, b2]
    aliases = {}
    if obuf is not None:
        in_specs.append(pl.BlockSpec(memory_space=pl.ANY))
        args.append(obuf)
        aliases = {len(args) - 1: 0}
    return pl.pallas_call(
        functools.partial(_conv_b_kernel, d=d, t=t, rc=64, top=top),
        grid=(nseq,),
        in_specs=in_specs,
        out_specs=pl.BlockSpec((t, d), lambda i: (blk0 + i, 0)),
        out_shape=jax.ShapeDtypeStruct((m, d), F32),
        input_output_aliases=aliases,
        scratch_shapes=[pltpu.VMEM((nb, t + 2 * top, LANES), F32),
                        pltpu.VMEM((nb, t, LANES), F32),
                        pltpu.VMEM((t, d), BF16),
                        pltpu.VMEM((d, d), BF16)],
        compiler_params=_cparams(("arbitrary",)),
        name="conv_b",
    )(*args)


def _conv_layer(x, mod4, P, layer, j, npr, seq_p, seq_s):
    m, d = x.shape
    nb = d // LANES
    tm = 256
    cidx = _cond_index(tm, npr, seq_s)
    u = pl.pallas_call(
        functools.partial(_conv_a_kernel, d=d),
        grid=(m // tm,),
        in_specs=[pl.BlockSpec((tm, d), lambda i: (i, 0)),
                  _mod_spec(layer, cidx, 6 * d, 1),
                  pl.BlockSpec((1, 1, d), lambda i: (layer, 0, 0)),
                  pl.BlockSpec((1, d, 2 * d), lambda i: (j, 0, 0)),
                  pl.BlockSpec((1, 1, 2 * d), lambda i: (j, 0, 0))],
        out_specs=pl.BlockSpec((nb, tm, LANES), lambda i: (0, i, 0)),
        out_shape=jax.ShapeDtypeStruct((nb, m, LANES), F32),
        scratch_shapes=[pltpu.VMEM((d, 2 * d), BF16)],
        compiler_params=_cparams(("arbitrary",)),
        name="conv_a",
    )(x, mod4, P["g_norm1"].reshape(-1, 1, d), P["cv_w_pw1"], P["cv_b_pw1"].reshape(-1, 1, 2 * d))
    wdw = jnp.pad(P["cv_w_dw"][j], ((0, 32 - CONV_WIDTH), (0, 0))).reshape(32, nb, LANES).transpose(1, 0, 2)
    bdw = P["cv_b_dw"][j].reshape(nb, 1, LANES)
    gln = P["cv_g_ln"][j].reshape(1, d)
    bln = P["cv_b_ln"][j].reshape(1, d)
    b2 = P["cv_b_pw2"].reshape(-1, 1, d)
    args = (x, u, mod4, wdw, bdw, gln, bln, P["cv_w_pw2"], b2, layer, j)
    op = _conv_b(*args, seq_p, 0, npr // seq_p, 0, 0, None)
    return _conv_b(*args, seq_s, npr, (m - npr) // seq_s, 1, 1, op)


_LOG2E = 1.4426950408889634
_ROT = QK_ROPE // 4


def _swap_index():
    idx = list(range(HEAD_PAD))
    for base in (QK_NOPE, QK_NOPE + 2 * _ROT):
        for l in range(_ROT):
            idx[base + l], idx[base + _ROT + l] = base + _ROT + l, base + l
    return jnp.asarray(idx, jnp.int32)


def _swap_cols(w):
    lane = jnp.arange(HEAD_PAD)
    rot = (lane >= QK_NOPE) & (lane < QK_DIM)
    return jnp.where(rot, jnp.take(w, _swap_index(), axis=-1), 0.0)


def _rope_tables(seq, gqn, gkn):
    rows = seq // GRID_W
    row = jnp.repeat(jnp.arange(rows, dtype=F32), GRID_W)
    col = jnp.tile(jnp.arange(GRID_W, dtype=F32), rows)
    axis = QK_ROPE // 2
    inv = ROPE_BASE ** (-jnp.arange(0, axis, 2, dtype=F32) / axis)
    ar, ac = row[:, None] * inv, col[:, None] * inv
    cr, sr, cc, sc = jnp.cos(ar), jnp.sin(ar), jnp.cos(ac), jnp.sin(ac)
    one = jnp.ones((seq, QK_NOPE), F32)
    zn = jnp.zeros((seq, QK_NOPE), F32)
    zp = jnp.zeros((seq, HEAD_PAD - QK_DIM), F32)
    c = jnp.concatenate([one, cr, cr, cc, cc, zp], axis=1)
    s = jnp.concatenate([zn, -sr, sr, -sc, sc, zp], axis=1)
    idx = _swap_index()
    return tuple(t for g in (gqn, gkn) for t in (c * g, s * jnp.take(g, idx, axis=-1)))


def _head_rstd(xh):
    ms = jnp.sum(xh * xh, axis=-1, keepdims=True) * (1.0 / QK_DIM)
    return lax.rsqrt(ms + EPS)


def _expand_kv(ckv_b, kpe, wuk_ref, wv_ref, gk, rot, k_ref, v_ref, v_transposed):
    kk = jnp.dot(ckv_b, wuk_ref[...], preferred_element_type=F32)
    if v_transposed:
        v = lax.dot_general(wv_ref[...], ckv_b, _NT, preferred_element_type=F32)
    else:
        v = jnp.dot(ckv_b, wv_ref[...], preferred_element_type=F32)
    v_ref[...] = v.astype(BF16)
    for h in range(MLA_HEADS):
        sl = slice(h * HEAD_PAD, (h + 1) * HEAD_PAD)
        kh = kk[:, sl] + kpe
        r = _head_rstd(kh)
        kh = (kh * gk) * r if rot is None else (kh * rot[0] + rot[1]) * r
        k_ref[:, sl] = kh.astype(BF16)


def _mla_proj_kernel(*refs, d, rope):
    (x_ref, mod_ref, g_ref, wcat_ref, gq_ref, wuq_ref, gkv_ref, wuk_ref, wv_ref, gqn_ref, gkn_ref) = refs[:11]
    if rope:
        cq_ref, sq_ref, ck_ref, sk_ref = refs[11:15]
        q_ref, k_ref, v_ref = refs[15:18]
    else:
        q_ref, k_ref, v_ref, ckv_ref, kpe_ref = refs[11:16]
    h = _normmod(x_ref[...], g_ref[0], mod_ref[0, 0, :, 0:d], mod_ref[0, 0, :, d:2 * d]).astype(BF16)
    t = jnp.dot(h, wcat_ref[...], preferred_element_type=F32)
    nq = gq_ref.shape[1]
    nkv = gkv_ref.shape[1]
    hq = MLA_HEADS * HEAD_PAD
    qd, craw, kpe = t[:, :nq], t[:, nq:nq + nkv], t[:, nq + nkv:nq + nkv + HEAD_PAD]
    cq = (qd * lax.rsqrt(jnp.mean(qd * qd, axis=-1, keepdims=True) + EPS) * gq_ref[...]).astype(BF16)
    ckv = craw * lax.rsqrt(jnp.mean(craw * craw, axis=-1, keepdims=True) + EPS) * gkv_ref[...]
    if not rope:
        ckv_ref[...] = ckv
        kpe_ref[...] = kpe
    q = jnp.dot(cq, wuq_ref[...], preferred_element_type=F32)
    if rope:
        ct, st = cq_ref[...], sq_ref[...]
    else:
        gqn = gqn_ref[...]
    for hd in range(MLA_HEADS):
        sl = slice(hd * HEAD_PAD, (hd + 1) * HEAD_PAD)
        qh = q[:, sl]
        r = _head_rstd(qh)
        qh = (qh * ct + q[:, hq + hd * HEAD_PAD:hq + (hd + 1) * HEAD_PAD] * st) * r if rope else (qh * gqn) * r
        q_ref[:, sl] = qh.astype(BF16)
    rot = None
    if rope:
        kpe2 = t[:, nq + nkv + HEAD_PAD:]
        rot = (ck_ref[...], kpe2 * sk_ref[...])
    _expand_kv(ckv.astype(BF16), kpe, wuk_ref, wv_ref, gkn_ref[...], rot, k_ref, v_ref, rope)


def _kv_ctx_kernel(ckv_ref, kpe_ref, wuk_ref, wv_ref, gkn_ref, k_ref, v_ref):
    _expand_kv(ckv_ref[...].astype(BF16), kpe_ref[...], wuk_ref, wv_ref, gkn_ref[...], None, k_ref, v_ref, True)


_NT = (((1,), (1,)), ((), ()))


def _attn_prompt_kernel(q_ref, k_ref, v_ref, x_ref, mod_ref, wo_ref, o_ref, a_sc, *, d):
    t = q_ref.shape[0]
    lane = lax.broadcasted_iota(jnp.int32, (t, LANES), 1)
    for hp in range(MLA_HEADS // 2):
        vp = v_ref[:, hp * LANES:(hp + 1) * LANES]
        outs = []
        for hh in range(2):
            sl = slice((2 * hp + hh) * HEAD_PAD, (2 * hp + hh + 1) * HEAD_PAD)
            s = lax.dot_general(q_ref[:, sl], k_ref[:, sl], _NT, preferred_element_type=F32)
            e = jnp.exp2(s - s.max(axis=-1, keepdims=True))
            l = jnp.sum(e, axis=-1, keepdims=True)
            outs.append(jnp.dot(e.astype(BF16), vp, preferred_element_type=F32) * (1.0 / l))
        a_sc[:, hp * LANES:(hp + 1) * LANES] = jnp.where(lane < V_HEAD, outs[0], outs[1]).astype(BF16)
    gate = mod_ref[0, 0, :, 2 * d:3 * d]
    o_ref[...] = x_ref[...] + gate * jnp.dot(a_sc[...], wo_ref[...], preferred_element_type=F32)


def _attn_sample_kernel(q_ref, kc_ref, vtc_ref, kl_ref, vtl_ref, x_ref, mod_ref, wo_ref, obuf_ref, o_ref, *, d):
    del obuf_ref
    nh = q_ref.shape[1] // HEAD_PAD
    row = lax.broadcasted_iota(jnp.int32, (LANES, 1), 0)

    def scores(hh):
        sl = slice(hh * HEAD_PAD, (hh + 1) * HEAD_PAD)
        qh = q_ref[:, sl]
        return [lax.dot_general(k_ref[:, sl], qh, _NT, preferred_element_type=F32)
                for k_ref in (kc_ref, kl_ref)]

    halves = []
    nxt = scores(0)
    for hh in range(nh):
        sts = nxt
        if hh + 1 < nh:
            nxt = scores(hh + 1)
        m = jnp.max(sts[0], axis=0, keepdims=True)
        for st in sts[1:]:
            m = jnp.maximum(m, jnp.max(st, axis=0, keepdims=True))
        pr = slice((hh // 2) * LANES, (hh // 2 + 1) * LANES)
        own = (row < V_HEAD) if hh % 2 == 0 else (row >= V_HEAD)
        acc = None
        for st, vt_ref in zip(sts, (vtc_ref, vtl_ref)):
            e = jnp.exp2(st - m).astype(BF16)
            vt = vt_ref[pr, :]
            lhs = jnp.where(own, vt, jnp.ones_like(vt))
            pv = jnp.dot(lhs, e, preferred_element_type=F32)
            acc = pv if acc is None else acc + pv
        if hh % 2 == 0:
            halves.append(acc[0:V_HEAD] * (1.0 / acc[V_HEAD:V_HEAD + 1]))
        else:
            halves.append(acc[V_HEAD:2 * V_HEAD] * (1.0 / acc[0:1]))
    attn = jnp.concatenate(halves, axis=0).T.astype(BF16)
    gate = mod_ref[0, 0, :, 2 * d:3 * d]
    o_ref[...] = x_ref[...] + gate * jnp.dot(attn, wo_ref[...], preferred_element_type=F32)


def _mla_weights(P, j):
    d = P["mla_w_dq"].shape[1]
    nq = P["mla_w_dq"].shape[2]
    nkv = P["mla_g_kv"].shape[1]
    pad = HEAD_PAD - QK_DIM
    w_dkv = P["mla_w_dkv"][j]
    w_kpe = jnp.pad(w_dkv[:, nkv:], ((0, 0), (QK_NOPE, pad)))
    wcat = jnp.concatenate([P["mla_w_dq"][j], w_dkv[:, :nkv], w_kpe], axis=1).astype(BF16)
    wuq = jnp.pad(P["mla_w_uq"][j].reshape(nq, MLA_HEADS, QK_DIM), ((0, 0), (0, 0), (0, pad)))
    wuq = wuq.reshape(nq, MLA_HEADS * HEAD_PAD).astype(BF16)
    wukv = P["mla_w_ukv"][j].reshape(nkv, MLA_HEADS, QK_NOPE + V_HEAD)
    wuk = jnp.pad(wukv[:, :, :QK_NOPE], ((0, 0), (0, 0), (0, HEAD_PAD - QK_NOPE)))
    wuk = wuk.reshape(nkv, MLA_HEADS * HEAD_PAD).astype(BF16)
    wv = wukv[:, :, QK_NOPE:].reshape(nkv, MLA_HEADS * V_HEAD).astype(BF16)
    gq = P["mla_g_q"][j].reshape(1, nq)
    gkv = P["mla_g_kv"][j].reshape(1, nkv)
    gqn = (jnp.pad(P["mla_g_qn"][j], (0, pad)) * (QK_DIM ** -0.5 * _LOG2E)).reshape(1, HEAD_PAD)
    gkn = jnp.pad(P["mla_g_kn"][j], (0, pad)).reshape(1, HEAD_PAD)
    wuq2 = _swap_cols(wuq.reshape(nq, MLA_HEADS, HEAD_PAD)).reshape(nq, MLA_HEADS * HEAD_PAD)
    wuq_r = jnp.concatenate([wuq, wuq2], axis=1)
    wcat_r = jnp.concatenate([wcat, _swap_cols(w_kpe).astype(BF16)], axis=1)
    return (wcat, gq, wuq, gkv, wuk, wv, gqn, gkn), (wcat_r, wuq_r, wv.T)


def _full(shape):
    return pl.BlockSpec(shape, lambda *_: (0,) * len(shape))


def _mla_proj(x, xblk0, mod4, g1, W, layer, row0, nrows, cidx, tm, rope_args):
    d = x.shape[1]
    wcat, gq, wuq, gkv, wuk, wv, gqn, gkn = W
    blk0 = row0 // tm
    hq = MLA_HEADS * HEAD_PAD
    hv = MLA_HEADS * V_HEAD
    nkv = gkv.shape[1]
    rope = rope_args is not None
    in_specs = [pl.BlockSpec((tm, d), lambda i: (xblk0 + i, 0)),
                pl.BlockSpec((1, 1, 1, 6 * d), lambda i: (layer, cidx(blk0 + i), 0, 0)),
                pl.BlockSpec((1, 1, d), lambda i: (layer, 0, 0)),
                _full(wcat.shape), _full(gq.shape), _full(wuq.shape), _full(gkv.shape), _full(wuk.shape),
                _full(wv.shape), _full(gqn.shape), _full(gkn.shape)]
    args = [x, mod4, g1, wcat, gq, wuq, gkv, wuk, wv, gqn, gkn]
    out_specs = [pl.BlockSpec((tm, hq), lambda i: (i, 0)),
                 pl.BlockSpec((tm, hq), lambda i: (i, 0))]
    out_shape = [jax.ShapeDtypeStruct((nrows, hq), BF16),
                 jax.ShapeDtypeStruct((nrows, hq), BF16)]
    if rope:
        wcat_r, wuq_r, wvt, tabs = rope_args
        per = tabs[0].shape[0] // tm
        in_specs[3], in_specs[5], in_specs[8] = _full(wcat_r.shape), _full(wuq_r.shape), _full(wvt.shape)
        args[3], args[5], args[8] = wcat_r, wuq_r, wvt
        in_specs += [pl.BlockSpec((tm, HEAD_PAD), lambda i: (i % per, 0))] * 4
        args += list(tabs)
        out_specs += [pl.BlockSpec((hv, tm), lambda i: (0, i))]
        out_shape += [jax.ShapeDtypeStruct((hv, nrows), BF16)]
    else:
        out_specs += [pl.BlockSpec((tm, hv), lambda i: (i, 0)),
                      pl.BlockSpec((tm, nkv), lambda i: (i, 0)), pl.BlockSpec((tm, HEAD_PAD), lambda i: (i, 0))]
        out_shape += [jax.ShapeDtypeStruct((nrows, hv), BF16),
                      jax.ShapeDtypeStruct((nrows, nkv), F32), jax.ShapeDtypeStruct((nrows, HEAD_PAD), F32)]
    return pl.pallas_call(
        functools.partial(_mla_proj_kernel, d=d, rope=rope),
        grid=(nrows // tm,),
        in_specs=in_specs, out_specs=out_specs, out_shape=out_shape,
        compiler_params=_cparams(("arbitrary",)),
        name="mla_proj_rope" if rope else "mla_proj",
    )(*args)


def _mla_layer(x, mod4, P, layer, j, cache_ckv, cache_kpe, npr, seq_p, seq_s):
    xa, xb = x if isinstance(x, tuple) else (x, x)
    d = xa.shape[1]
    nrs = xb.shape[0] if isinstance(x, tuple) else xb.shape[0] - npr
    m = npr + nrs
    nbp, nbs = npr // seq_p, nrs // seq_s
    past = cache_ckv.shape[1]
    W, (wcat_r, wuq_r, wvt) = _mla_weights(P, j)
    wuk, gqn, gkn = W[4], W[6], W[7]
    g1 = P["g_norm1"].reshape(-1, 1, d)
    hq = MLA_HEADS * HEAD_PAD
    hv = MLA_HEADS * V_HEAD
    tm = 256
    cidx = _cond_index(tm, npr, seq_s)
    xb0 = 0 if isinstance(x, tuple) else npr // tm
    tabs = _rope_tables(seq_s, gqn[0], gkn[0])
    qp, kp, vp, ckv, kpe128 = _mla_proj(xa, 0, mod4, g1, W, layer, 0, npr, cidx, tm, None)
    qs, ks, vts = _mla_proj(xb, xb0, mod4, g1, W, layer, npr, nrs, cidx, tm, (wcat_r, wuq_r, wvt, tabs))
    cc = cache_ckv.reshape(nbs * past, -1)
    ck = jnp.pad(cache_kpe.reshape(nbs * past, -1), ((0, 0), (QK_NOPE, HEAD_PAD - QK_DIM)))
    kc, vtc = pl.pallas_call(
        _kv_ctx_kernel,
        grid=(nbs,),
        in_specs=[pl.BlockSpec((past, cc.shape[1]), lambda i: (i, 0)),
                  pl.BlockSpec((past, HEAD_PAD), lambda i: (i, 0)),
                  _full(wuk.shape), _full(wvt.shape), _full(gkn.shape)],
        out_specs=[pl.BlockSpec((past, hq), lambda i: (i, 0)), pl.BlockSpec((hv, past), lambda i: (0, i))],
        out_shape=[jax.ShapeDtypeStruct((nbs * past, hq), BF16), jax.ShapeDtypeStruct((hv, nbs * past), BF16)],
        compiler_params=_cparams(("arbitrary",)),
        name="mla_kv_ctx",
    )(cc, ck, wuk, wvt, gkn)
    wo = P["mla_w_o"][j].astype(BF16)
    xp0 = 0
    xs0 = 0 if isinstance(x, tuple) else npr
    xn_p = pl.pallas_call(
        functools.partial(_attn_prompt_kernel, d=d),
        grid=(nbp,),
        in_specs=[pl.BlockSpec((seq_p, hq), lambda b: (b, 0)),
                  pl.BlockSpec((seq_p, hq), lambda b: (b, 0)),
                  pl.BlockSpec((seq_p, hv), lambda b: (b, 0)),
                  pl.BlockSpec((seq_p, d), lambda b: (xp0 // seq_p + b, 0)),
                  pl.BlockSpec((1, 1, 1, 6 * d), lambda b: (layer, 0, 0, 0)),
                  _full(wo.shape)],
        out_specs=pl.BlockSpec((seq_p, d), lambda b: (b, 0)),
        out_shape=jax.ShapeDtypeStruct((m, d), F32),
        scratch_shapes=[pltpu.VMEM((seq_p, hv), BF16)],
        compiler_params=_cparams(("arbitrary",)),
        name="attn_prompt",
    )(qp, kp, vp, xa, mod4, wo)
    tq = min(seq_s, 512)
    nqt = seq_s // tq
    o0 = npr // tq
    xn = pl.pallas_call(
        functools.partial(_attn_sample_kernel, d=d),
        grid=(nbs, nqt),
        in_specs=[pl.BlockSpec((tq, hq), lambda b, qi: (b * nqt + qi, 0)),
                  pl.BlockSpec((past, hq), lambda b, qi: (b, 0)),
                  pl.BlockSpec((hv, past), lambda b, qi: (0, b)),
                  pl.BlockSpec((seq_s, hq), lambda b, qi: (b, 0)),
                  pl.BlockSpec((hv, seq_s), lambda b, qi: (0, b)),
                  pl.BlockSpec((tq, d), lambda b, qi: (xs0 // tq + b * nqt + qi, 0)),
                  pl.BlockSpec((1, 1, 1, 6 * d), lambda b, qi: (layer, 1 + b, 0, 0)),
                  _full(wo.shape),
                  pl.BlockSpec(memory_space=pl.ANY)],
        out_specs=pl.BlockSpec((tq, d), lambda b, qi: (o0 + b * nqt + qi, 0)),
        out_shape=jax.ShapeDtypeStruct((m, d), F32),
        input_output_aliases={8: 0},
        compiler_params=_cparams(("arbitrary", "arbitrary")),
        name="attn_sample",
    )(qs, kc, vtc, ks, vts, xb, mod4, wo, xn_p)
    new_ckv = ckv.reshape(nbp, seq_p, -1)
    new_kpe = kpe128[:, QK_NOPE:QK_DIM].reshape(nbp, seq_p, QK_ROPE)
    return xn, new_ckv, new_kpe


SSD_R = 8
SSD_GW = SSD_R * SSD_HEADDIM
SSD_NX = SSD_GW // LANES
SSD_DTW = 2 * SSD_R
SSD_TOP = 8


def _ssd_in_kernel(x_ref, mod_ref, g_ref, w_ref, wdt_ref, zx_ref, dt_ref, h_sc, *, d, heads):
    @pl.when(pl.program_id(1) == 0)
    def _():
        h = _normmod(x_ref[...], g_ref[0], mod_ref[0, 0, :, 0:d], mod_ref[0, 0, :, d:2 * d]).astype(BF16)
        h_sc[...] = h
        row = lax.broadcasted_iota(jnp.int32, wdt_ref.shape[1:], 0)
        wdt = jnp.where(row < 2 * heads, wdt_ref[0], 0.0).astype(BF16)
        dtt = lax.dot_general(wdt, h, _NT, preferred_element_type=F32)
        for g in range(SSD_GROUPS):
            dt_ref[g, 0:SSD_R, :] = dtt[g * SSD_R:(g + 1) * SSD_R, :]
            dt_ref[g, SSD_R:SSD_DTW, :] = dtt[heads + g * SSD_R:heads + (g + 1) * SSD_R, :]

    r = lax.dot_general(h_sc[...], w_ref[0].astype(BF16), _NT, preferred_element_type=F32)
    for c in range(r.shape[1] // LANES):
        zx_ref[c] = r[:, c * LANES:(c + 1) * LANES]


def _split3(x):
    hi = x.astype(BF16)
    r1 = x - hi.astype(F32)
    mid = r1.astype(BF16)
    lo = (r1 - mid.astype(F32)).astype(BF16)
    return hi, mid, lo


def _dot3_right(x, tri):
    return sum(jnp.dot(p, tri, preferred_element_type=F32) for p in _split3(x))


SSD_NPIECE = 7


def _ssd_spread_matrix():
    sel = np.zeros((2, CHUNK, SSD_R * LANES + 2 * SSD_GW), np.float32)
    for dr in range(2):
        for r in range(SSD_R):
            h = dr * SSD_R + r
            for p in range(3):
                sel[dr, p * SSD_DTW + h, r * LANES:(r + 1) * LANES] = 1.0
            for p in range(3, 5):
                sel[dr, p * SSD_DTW + h, SSD_R * LANES + r * SSD_HEADDIM:SSD_R * LANES + (r + 1) * SSD_HEADDIM] = 1.0
            for p in range(5, 7):
                c0 = SSD_R * LANES + SSD_GW
                sel[dr, p * SSD_DTW + h, c0 + r * SSD_HEADDIM:c0 + (r + 1) * SSD_HEADDIM] = 1.0
    return jnp.asarray(sel, BF16)


def _ssd_core_kernel(*refs, t, has_h0):
    (z_ref, x_ref, b_ref, c_ref, dt_ref, wx_ref, wb_ref, wc_ref, bx_ref, bb_ref, bc_ref,
     dtb_ref, alog_ref, dsum_ref, gn_ref, bsel_ref) = refs[:16]
    if has_h0:
        h0_ref, y_ref = refs[16], refs[18]
        st_ref = None
        scr = refs[19:]
    else:
        y_ref, st_ref = refs[16:18]
        h0_ref = None
        scr = refs[18:]
    (pad_sc, xc_sc, bcv_sc, ccv_sc, bt_sc, cb_sc, y_sc, dtt_sc, dat_sc, s_sc) = scr
    nc = t // CHUNK
    half = (SSD_CONV - 1) // 2
    nblk = SSD_NX + 2

    @pl.when((pl.program_id(0) == 0) & (pl.program_id(1) == 0))
    def _():
        pad_sc[:, 0:SSD_TOP, :] = jnp.zeros((nblk, SSD_TOP, LANES), F32)
        pad_sc[:, SSD_TOP + t:2 * SSD_TOP + t, :] = jnp.zeros((nblk, SSD_TOP, LANES), F32)

    pad_sc[0:SSD_NX, SSD_TOP:SSD_TOP + t, :] = x_ref[...]
    pad_sc[SSD_NX, SSD_TOP:SSD_TOP + t, :] = b_ref[0]
    pad_sc[SSD_NX + 1, SSD_TOP:SSD_TOP + t, :] = c_ref[0]

    for blk in range(nblk):
        if blk < SSD_NX:
            w_r, bias = wx_ref, bx_ref[blk]
            widx = blk
        elif blk == SSD_NX:
            w_r, bias, widx = wb_ref, bb_ref[0], 0
        else:
            w_r, bias, widx = wc_ref, bc_ref[0], 0
        for c in range(nc):
            r0 = c * CHUNK
            acc = jnp.broadcast_to(bias, (CHUNK, LANES))
            for k in range(SSD_CONV):
                acc = acc + w_r[widx, k:k + 1, :] * pad_sc[blk, r0 + SSD_TOP - half + k:r0 + SSD_TOP - half + k + CHUNK, :]
            acc = _silu(acc)
            if blk < SSD_NX:
                xc_sc[blk, r0:r0 + CHUNK, :] = acc
            elif blk == SSD_NX:
                bcv_sc[r0:r0 + CHUNK, :] = acc.astype(BF16)
                bt_sc[c] = acc.T.astype(BF16)
            else:
                ccv_sc[r0:r0 + CHUNK, :] = acc.astype(BF16)

    a_row = -jnp.exp(alog_ref[0]) * _LOG2E
    for c in range(nc):
        v = dt_ref[0, :, c * CHUNK:(c + 1) * CHUNK] + dtb_ref[0]
        dt = jnp.maximum(v, 0.0) + jnp.log1p(jnp.exp(-jnp.abs(v)))
        dtt_sc[c] = dt
        dat_sc[c] = dt * a_row

    row = lax.broadcasted_iota(jnp.int32, (CHUNK, CHUNK), 0)
    colm = lax.broadcasted_iota(jnp.int32, (CHUNK, CHUNK), 1)
    lower = colm <= row
    upper = colm >= row
    tri_l = jnp.where(lower, 1.0, 0.0).astype(BF16)
    tri_u = jnp.where(upper, 1.0, 0.0).astype(BF16)
    lane = lax.broadcasted_iota(jnp.int32, (CHUNK, LANES), 1)
    head_row = lax.broadcasted_iota(jnp.int32, (SSD_DTW, 1), 0)
    zero16 = jnp.zeros((SSD_DTW, CHUNK), F32)

    col0, w0, e0 = 0, SSD_R * LANES, SSD_R * LANES + SSD_GW
    zpad = jnp.zeros((CHUNK - SSD_NPIECE * SSD_DTW, CHUNK), F32)

    def prepare(dr, c):
        lo = dr * SSD_R
        tri_t = tri_l if dr else tri_u
        own = (head_row >= lo) & (head_row < lo + SSD_R)
        rows = slice(c * CHUNK, (c + 1) * CHUNK)
        dtt = dtt_sc[c]
        cum_t = _dot3_right(dat_sc[c], tri_t)
        edge = cum_t[:, 0:1] if dr else cum_t[:, CHUNK - 1:CHUNK]
        w2_t = dtt * jnp.exp2(edge - cum_t)
        ecum_t = jnp.exp2(cum_t)
        pieces = (list(_split3(jnp.where(own, cum_t, zero16)))
                  + list(_split3(jnp.where(own, w2_t, zero16))[:2])
                  + list(_split3(jnp.where(own, ecum_t, zero16))[:2]))
        staged = jnp.concatenate([p.astype(F32) for p in pieces] + [zpad], axis=0)
        spread = jnp.dot(staged.T.astype(BF16), bsel_ref[dr], preferred_element_type=F32)
        cmat = ccv_sc[rows, :]
        if dr == 0:
            cb = lax.dot_general(cmat, bcv_sc[rows, :], _NT, preferred_element_type=F32)
            cb_sc[c] = cb
        else:
            cb = cb_sc[c]
        rowterm = cum_t - jnp.log2(dtt)
        return dict(dr=dr, c=c, rows=rows, lo=lo, rowterm=rowterm, spread=spread, cmat=cmat, cb=cb)

    def scan_chunk(pre):
        dr, c, rows, lo = pre["dr"], pre["c"], pre["rows"], pre["lo"]
        rowterm, spread, cmat, cb = pre["rowterm"], pre["spread"], pre["cmat"], pre["cb"]
        mask = upper if dr else lower
        s_prev = s_sc[...]
        yoff = jnp.dot(cmat, s_prev.astype(BF16), preferred_element_type=F32)
        xd = []
        for q in range(SSD_NX):
            xblk = xc_sc[q, rows, :]
            xb16 = xblk.astype(BF16)
            mh = []
            for r in (2 * q, 2 * q + 1):
                seg = spread[:, col0 + r * LANES:col0 + (r + 1) * LANES] - rowterm[lo + r:lo + r + 1, :]
                lm = jnp.exp2(jnp.where(mask, seg, -jnp.inf))
                mh.append((cb * lm).astype(BF16))
            rhs = jnp.concatenate([jnp.where(lane < SSD_HEADDIM, xb16, jnp.zeros_like(xb16)),
                                   jnp.where(lane >= SSD_HEADDIM, xb16, jnp.zeros_like(xb16))], axis=0)
            y = jnp.dot(jnp.concatenate(mh, axis=1), rhs, preferred_element_type=F32)
            sl = slice(q * LANES, (q + 1) * LANES)
            y = y + yoff[:, sl] * spread[:, e0 + q * LANES:e0 + (q + 1) * LANES]
            if dr == 0:
                y_sc[q, rows, :] = y + xblk * dsum_ref[0, :, sl]
            else:
                y_sc[q, rows, :] = y_sc[q, rows, :] + y
            xd.append((xblk * spread[:, w0 + q * LANES:w0 + (q + 1) * LANES]).astype(BF16))
        st = jnp.dot(bt_sc[c], jnp.concatenate(xd, axis=1), preferred_element_type=F32)
        er = 0 if dr else CHUNK - 1
        cd = spread[er:er + 1, e0:e0 + SSD_GW]
        s_sc[...] = s_prev * cd + st

    order = [(0, c) for c in range(nc)] + [(1, c) for c in range(nc - 1, -1, -1)]
    pre = prepare(*order[0])
    for k, (dr, c) in enumerate(order):
        if c == (nc - 1 if dr else 0):
            if has_h0:
                s_sc[...] = h0_ref[0, dr, 0].T
            else:
                s_sc[...] = jnp.zeros_like(s_sc)
        nxt = prepare(*order[k + 1]) if k + 1 < len(order) else None
        scan_chunk(pre)
        pre = nxt
        if st_ref is not None and c == (0 if dr else nc - 1):
            st_ref[0, dr, 0] = s_sc[...].T

    rl = min(t, 256)
    for r0 in range(0, t, rl):
        yz = [y_sc[q, r0:r0 + rl, :] * _silu(z_ref[q, r0:r0 + rl, :]) for q in range(SSD_NX)]
        ms = sum(jnp.sum(a * a, axis=-1, keepdims=True) for a in yz) * (1.0 / SSD_GW)
        rstd = lax.rsqrt(ms + EPS)
        for q in range(SSD_NX):
            sl = slice(q * LANES, (q + 1) * LANES)
            y_ref[r0:r0 + rl, sl] = (yz[q] * rstd * gn_ref[0, :, sl]).astype(BF16)


def _ssd_core(zx, dtc, cw, cbias, dtb, alog, dsum, gn, h0, ybuf, t, row0, nseq):
    has_h0 = h0 is not None
    m = zx.shape[1]
    bsel = _ssd_spread_matrix()
    b0 = row0 // t
    nc = t // CHUNK
    nz = SSD_GROUPS * SSD_NX
    in_specs = [pl.BlockSpec((SSD_NX, t, LANES), lambda b, g: (g, b0 + b, 0)),
                pl.BlockSpec((SSD_NX, t, LANES), lambda b, g: (SSD_GROUPS + g, b0 + b, 0)),
                pl.BlockSpec((1, t, LANES), lambda b, g: (2 * nz + g, b0 + b, 0)),
                pl.BlockSpec((1, t, LANES), lambda b, g: (2 * nz + SSD_GROUPS + g, b0 + b, 0)),
                pl.BlockSpec((1, SSD_DTW, t), lambda b, g: (g, 0, b0 + b)),
                pl.BlockSpec((SSD_NX, 8, LANES), lambda b, g: (g, 0, 0)),
                pl.BlockSpec((1, 8, LANES), lambda b, g: (nz + g, 0, 0)),
                pl.BlockSpec((1, 8, LANES), lambda b, g: (nz + SSD_GROUPS + g, 0, 0)),
                pl.BlockSpec((SSD_NX, 1, LANES), lambda b, g: (g, 0, 0)),
                pl.BlockSpec((1, 1, LANES), lambda b, g: (nz + g, 0, 0)),
                pl.BlockSpec((1, 1, LANES), lambda b, g: (nz + SSD_GROUPS + g, 0, 0)),
                pl.BlockSpec((1, SSD_DTW, CHUNK), lambda b, g: (g, 0, 0)),
                pl.BlockSpec((1, SSD_DTW, CHUNK), lambda b, g: (g, 0, 0)),
                pl.BlockSpec((1, 1, SSD_GW), lambda b, g: (g, 0, 0)),
                pl.BlockSpec((1, 1, SSD_GW), lambda b, g: (g, 0, 0)),
                pl.BlockSpec(bsel.shape, lambda b, g: (0, 0, 0))]
    args = [zx, zx, zx, zx, dtc, cw, cw, cw, cbias, cbias, cbias, dtb, alog, dsum, gn, bsel]
    st_spec = pl.BlockSpec((1, 2, 1, SSD_GW, SSD_STATE), lambda b, g: (b, 0, g, 0, 0))
    y_spec = pl.BlockSpec((t, SSD_GW), lambda b, g: (b0 + b, g))
    y_shape = jax.ShapeDtypeStruct((m, SSD_GROUPS * SSD_GW), BF16)
    aliases = {}
    if has_h0:
        in_specs += [st_spec, pl.BlockSpec(memory_space=pl.ANY)]
        args += [h0, ybuf]
        aliases = {len(args) - 1: 0}
        out_specs, out_shape = y_spec, y_shape
    else:
        out_specs = [y_spec, st_spec]
        out_shape = [y_shape, jax.ShapeDtypeStruct((nseq, 2, SSD_GROUPS, SSD_GW, SSD_STATE), F32)]
    scratch = [pltpu.VMEM((SSD_NX + 2, t + 2 * SSD_TOP, LANES), F32),
               pltpu.VMEM((SSD_NX, t, LANES), F32),
               pltpu.VMEM((t, LANES), BF16),
               pltpu.VMEM((t, LANES), BF16),
               pltpu.VMEM((nc, CHUNK, CHUNK), BF16),
               pltpu.VMEM((nc, CHUNK, CHUNK), F32),
               pltpu.VMEM((SSD_NX, t, LANES), F32),
               pltpu.VMEM((nc, SSD_DTW, CHUNK), F32),
               pltpu.VMEM((nc, SSD_DTW, CHUNK), F32),
               pltpu.VMEM((SSD_STATE, SSD_GW), F32)]
    return pl.pallas_call(
        functools.partial(_ssd_core_kernel, t=t, has_h0=has_h0),
        grid=(nseq, SSD_GROUPS),
        in_specs=in_specs, out_specs=out_specs, out_shape=out_shape,
        input_output_aliases=aliases,
        scratch_shapes=scratch,
        compiler_params=_cparams(("arbitrary", "arbitrary")),
        name="ssd_core_h0" if has_h0 else "ssd_core",
    )(*args)


def _ssd_layer(x, mod4, P, layer, j, st0, npr, seq_p, seq_s):
    m, d = x.shape
    nbp, nbs = npr // seq_p, (m - npr) // seq_s
    heads = SSD_GROUPS * SSD_R
    inner = heads * SSD_HEADDIM
    gn_w = SSD_GROUPS * SSD_STATE
    nzx = 2 * inner + 2 * gn_w
    tm, tn = min(1024, seq_s), 1024
    cidx = _cond_index(tm, npr, seq_s)
    w_in_t = jnp.swapaxes(P["ssd_w_in"], 1, 2)
    zx, dtc = pl.pallas_call(
        functools.partial(_ssd_in_kernel, d=d, heads=heads),
        grid=(m // tm, nzx // tn),
        in_specs=[pl.BlockSpec((tm, d), lambda i, n: (i, 0)),
                  _mod_spec(layer, cidx, 6 * d, 2),
                  pl.BlockSpec((1, 1, d), lambda i, n: (layer, 0, 0)),
                  pl.BlockSpec((1, tn, d), lambda i, n: (j, n, 0)),
                  pl.BlockSpec((1, LANES, d), lambda i, n: (j, nzx // LANES, 0))],
        out_specs=[pl.BlockSpec((tn // LANES, tm, LANES), lambda i, n: (n, i, 0)),
                   pl.BlockSpec((SSD_GROUPS, SSD_DTW, tm), lambda i, n: (0, 0, i))],
        out_shape=[jax.ShapeDtypeStruct((nzx // LANES, m, LANES), F32),
                   jax.ShapeDtypeStruct((SSD_GROUPS, SSD_DTW, m), F32)],
        scratch_shapes=[pltpu.VMEM((tm, d), BF16)],
        compiler_params=_cparams(("arbitrary", "arbitrary")),
        name="ssd_in",
    )(x, mod4, P["g_norm1"].reshape(-1, 1, d), w_in_t, w_in_t)

    def compact(a):
        a = a.reshape(2, SSD_GROUPS, SSD_R).transpose(1, 0, 2).reshape(SSD_GROUPS, SSD_DTW, 1)
        return jnp.broadcast_to(a, (SSD_GROUPS, SSD_DTW, CHUNK))

    ncb = (inner + 2 * gn_w) // LANES
    cw = jnp.pad(P["ssd_w_conv"][j], ((0, 8 - SSD_CONV), (0, 0))).reshape(8, ncb, LANES).transpose(1, 0, 2)
    cbias = P["ssd_b_conv"][j].reshape(ncb, 1, LANES)
    dsum = jnp.repeat(P["ssd_d"][j][0] + P["ssd_d"][j][1], SSD_HEADDIM).reshape(SSD_GROUPS, 1, SSD_GW)
    gn = P["ssd_g_norm"][j].reshape(SSD_GROUPS, 1, SSD_GW)
    common = (zx, dtc, cw, cbias, compact(P["ssd_dt_bias"][j]), compact(P["ssd_a_log"][j]), dsum, gn)
    yp, st = _ssd_core(*common, None, None, seq_p, 0, nbp)
    h0 = st0.reshape(nbs, 2, SSD_GROUPS, SSD_GW, SSD_STATE)
    y = _ssd_core(*common, h0, yp, seq_s, npr, nbs)
    return y, st.reshape(nbp, 2, heads, SSD_HEADDIM, SSD_STATE)


def kernel(x_prompt, x_sample, cache_ckv, cache_kpe, state_ssm, c, c_ctx, w_ada, b_ada, g_norm1, g_norm2,
           mla_w_dq, mla_g_q, mla_w_uq, mla_w_dkv, mla_g_kv, mla_w_ukv, mla_g_qn, mla_g_kn, mla_w_o,
           cv_w_pw1, cv_b_pw1, cv_w_dw, cv_b_dw, cv_g_ln, cv_b_ln, cv_w_pw2, cv_b_pw2,
           ssd_w_in, ssd_w_conv, ssd_b_conv, ssd_dt_bias, ssd_a_log, ssd_d, ssd_g_norm, ssd_w_out,
           ffn_w_in, ffn_w_out):
    P = dict(g_norm1=g_norm1, g_norm2=g_norm2,
             mla_w_dq=mla_w_dq, mla_g_q=mla_g_q, mla_w_uq=mla_w_uq, mla_w_dkv=mla_w_dkv, mla_g_kv=mla_g_kv,
             mla_w_ukv=mla_w_ukv, mla_g_qn=mla_g_qn, mla_g_kn=mla_g_kn, mla_w_o=mla_w_o,
             cv_w_pw1=cv_w_pw1, cv_b_pw1=cv_b_pw1, cv_w_dw=cv_w_dw, cv_b_dw=cv_b_dw, cv_g_ln=cv_g_ln,
             cv_b_ln=cv_b_ln, cv_w_pw2=cv_w_pw2, cv_b_pw2=cv_b_pw2,
             ssd_w_in=ssd_w_in, ssd_w_conv=ssd_w_conv, ssd_b_conv=ssd_b_conv, ssd_dt_bias=ssd_dt_bias,
             ssd_a_log=ssd_a_log, ssd_d=ssd_d, ssd_g_norm=ssd_g_norm, ssd_w_out=ssd_w_out)
    nbp, seq_p, d = x_prompt.shape
    nbs, seq_s, _ = x_sample.shape
    depth = w_ada.shape[0]
    npr = nbp * seq_p
    x = (x_prompt.reshape(npr, d), x_sample.reshape(nbs * seq_s, d))
    cond8 = jnp.concatenate([c_ctx[None, :], c, jnp.zeros((8 - 1 - nbs, d), F32)], axis=0)
    mod4 = _ada(cond8, w_ada, b_ada).reshape(depth, 8, 1, 6 * d)
    ffn_tm = min(1024, seq_s)
    ffn_cidx = _cond_index(ffn_tm, npr, seq_s)
    ckvs, kpes, ssms = [], [], []
    for i in range(depth):
        kind, j = i % N_MIXERS, i // N_MIXERS
        if kind != 0 and isinstance(x, tuple):
            x = jnp.concatenate(x, axis=0)
        y = w_o = None
        if kind == 0:
            x, ckv, kpe = _mla_layer(x, mod4, P, i, j, cache_ckv[:, j], cache_kpe[:, j], npr, seq_p, seq_s)
            ckvs.append(ckv)
            kpes.append(kpe)
        elif kind == 1:
            x = _conv_layer(x, mod4, P, i, j, npr, seq_p, seq_s)
        else:
            y, st = _ssd_layer(x, mod4, P, i, j, state_ssm[:, j], npr, seq_p, seq_s)
            w_o = ssd_w_out[j].astype(BF16)
            ssms.append(st)
        x = _ffn(x, mod4, g_norm2, ffn_w_in, ffn_w_out, i, ffn_cidx, ffn_tm, 256, npr,
                 y=y, w_o=w_o, split=(i == depth - 1))
    return (x[0].reshape(nbp, seq_p, d), x[1].reshape(nbs, seq_s, d),
            jnp.stack(ckvs, axis=1), jnp.stack(kpes, axis=1), jnp.stack(ssms, axis=1))
```

```python
import functools

import jax
import jax.numpy as jnp
import numpy as np
from jax import lax
from jax.experimental import pallas as pl
from jax.experimental.pallas import tpu as pltpu

F32 = jnp.float32
BF16 = jnp.bfloat16

EPS = 1e-6
ROPE_BASE = 10000.0
GRID_W = 64
N_MIXERS = 3

LANES = 128
VMEM_LIMIT_BYTES = 56 * 1024 * 1024

MLA_HEADS = 16
QK_NOPE = 64
QK_ROPE = 32
QK_DIM = QK_NOPE + QK_ROPE
V_HEAD = 64
HEAD_PAD = LANES
CONV_WIDTH = 31
SSD_HEADDIM = 64
SSD_GROUPS = 4
SSD_STATE = 128
SSD_CONV = 5
CHUNK = 128


def _cparams(sem):
    return pltpu.CompilerParams(dimension_semantics=sem, vmem_limit_bytes=VMEM_LIMIT_BYTES)


def _silu(x):
    return x * jax.nn.sigmoid(x)


def _normmod(x, g, sh, sc):
    ms = jnp.mean(x * x, axis=-1, keepdims=True)
    return (x * lax.rsqrt(ms + EPS)) * (g * (1.0 + sc)) + sh


def _cond_index(tm, n_prompt_rows, sample_seq):
    def f(i):
        r = i * tm
        return jnp.where(r < n_prompt_rows, 0, 1 + (r - n_prompt_rows) // sample_seq)
    return f


def _mod_spec(layer, cidx, d6, ngrid):
    if ngrid == 1:
        return pl.BlockSpec((1, 1, 1, d6), lambda i: (layer, cidx(i), 0, 0))
    return pl.BlockSpec((1, 1, 1, d6), lambda i, j: (layer, cidx(i), 0, 0))


def _ada_kernel(c_ref, w_ref, b_ref, o_ref):
    s = _silu(c_ref[...]).astype(BF16)
    o_ref[0] = jnp.dot(s, w_ref[0].astype(BF16), preferred_element_type=F32) + b_ref[0]


def _ada(cond8, w_ada, b_ada):
    depth, d, n = w_ada.shape
    tn = n // 4
    return pl.pallas_call(
        _ada_kernel,
        grid=(depth, n // tn),
        in_specs=[pl.BlockSpec((8, d), lambda l, j: (0, 0)),
                  pl.BlockSpec((1, d, tn), lambda l, j: (l, 0, j)),
                  pl.BlockSpec((1, 1, tn), lambda l, j: (l, 0, j))],
        out_specs=pl.BlockSpec((1, 8, tn), lambda l, j: (l, 0, j)),
        out_shape=jax.ShapeDtypeStruct((depth, 8, n), F32),
        compiler_params=_cparams(("arbitrary", "arbitrary")),
        name="ada",
    )(cond8, w_ada, b_ada.reshape(depth, 1, n))


FFN_SLOTS = 3


def _ffn_kernel(*refs, d, n_in_a, n_out_a, fused, layer, hid, th, ntiles):
    refs = list(refs)
    xa_ref = refs.pop(0)
    xb_ref = refs.pop(0) if n_in_a is not None else None
    y_ref, wo1_ref = (refs.pop(0), refs.pop(0)) if fused else (None, None)
    mod_ref, g_ref, win_hbm, wout_hbm = refs[:4]
    if n_out_a is None:
        o_ref, h_sc, wa_buf, wu_buf, wo_buf, sem = refs[4:]
        acc = o_ref
    else:
        oa_ref, ob_ref, h_sc, acc, wa_buf, wu_buf, wo_buf, sem = refs[4:]
    i = pl.program_id(0)
    nj = hid // th
    base = lax.rem(i * (nj % FFN_SLOTS), FFN_SLOTS)

    def slot_of(j):
        return lax.rem(base + j % FFN_SLOTS, FFN_SLOTS)

    def chunk_copies(j):
        c, s = j % nj, slot_of(j)
        return (pltpu.make_async_copy(win_hbm.at[layer, :, pl.ds(c * th, th)], wa_buf.at[s], sem.at[0, s]),
                pltpu.make_async_copy(win_hbm.at[layer, :, pl.ds(hid + c * th, th)], wu_buf.at[s], sem.at[1, s]),
                pltpu.make_async_copy(wout_hbm.at[layer, pl.ds(c * th, th), :], wo_buf.at[s], sem.at[2, s]))

    def start(j):
        for cp in chunk_copies(j):
            cp.start()

    def wait(j):
        for cp in chunk_copies(j):
            cp.wait()

    @pl.when(i == 0)
    def _():
        start(0)
        start(1)

    wait(0)
    wa0, wu0 = wa_buf[base].astype(BF16), wu_buf[base].astype(BF16)
    tm = acc.shape[0]
    rc = min(tm, 256)
    a_rows, u_rows = [], []
    for r0 in range(0, tm, rc):
        rows = slice(r0, r0 + rc)
        x = xa_ref[rows, :] if xb_ref is None else jnp.where(i < n_in_a, xa_ref[rows, :], xb_ref[rows, :])
        if fused:
            x = x + mod_ref[0, 0, :, 2 * d:3 * d] * jnp.dot(y_ref[rows, :], wo1_ref[...],
                                                             preferred_element_type=F32)
        acc[rows, :] = x
        hr = _normmod(x, g_ref[0], mod_ref[0, 0, :, 3 * d:4 * d], mod_ref[0, 0, :, 4 * d:5 * d]).astype(BF16)
        h_sc[rows, :] = hr
        a_rows.append(jnp.dot(hr, wa0, preferred_element_type=F32))
        u_rows.append(jnp.dot(hr, wu0, preferred_element_type=F32))
    au = (jnp.concatenate(a_rows, axis=0), jnp.concatenate(u_rows, axis=0))

    gate2 = mod_ref[0, 0, :, 5 * d:6 * d]
    h = h_sc[...]

    def up(j):
        s = slot_of(j)
        return (jnp.dot(h, wa_buf[s].astype(BF16), preferred_element_type=F32),
                jnp.dot(h, wu_buf[s].astype(BF16), preferred_element_type=F32))

    for j in range(nj):
        if j + 1 < nj:
            wait(j + 1)
        if j + 2 < nj:
            start(j + 2)
        else:
            @pl.when(i + 1 < ntiles)
            def _(j=j):
                start(j + 2)
        a, u = au
        t = (_silu(a) * u).astype(BF16)
        if j + 1 < nj:
            au = up(j + 1)
        acc[...] += gate2 * jnp.dot(t, wo_buf[slot_of(j)].astype(BF16), preferred_element_type=F32)

    if n_out_a is not None:
        @pl.when(i < n_out_a)
        def _():
            oa_ref[...] = acc[...]

        @pl.when(i >= n_out_a)
        def _():
            ob_ref[...] = acc[...]


def _ffn(x, mod4, g2, w_in, w_out, layer, cidx, tm, th, npr, y=None, w_o=None, split=False):
    pair = isinstance(x, tuple)
    d = x[0].shape[1] if pair else x.shape[1]
    m = (x[0].shape[0] + x[1].shape[0]) if pair else x.shape[0]
    hid = w_out.shape[1]
    nj = hid // th
    n_a = npr // tm
    fused = y is not None
    if pair:
        in_specs = [pl.BlockSpec((tm, d), lambda i: (jnp.minimum(i, n_a - 1), 0)),
                    pl.BlockSpec((tm, d), lambda i: (jnp.maximum(i - n_a, 0), 0))]
        args = list(x)
    else:
        in_specs = [pl.BlockSpec((tm, d), lambda i: (i, 0))]
        args = [x]
    if fused:
        k = y.shape[1]
        in_specs += [pl.BlockSpec((tm, k), lambda i: (i, 0)), pl.BlockSpec((k, d), lambda i: (0, 0))]
        args += [y, w_o]
    in_specs += [_mod_spec(layer, cidx, 6 * d, 1),
                 pl.BlockSpec((1, 1, d), lambda i: (layer, 0, 0)),
                 pl.BlockSpec(memory_space=pl.ANY),
                 pl.BlockSpec(memory_space=pl.ANY)]
    args += [mod4, g2.reshape(g2.shape[0], 1, d), w_in, w_out]
    scratch = [pltpu.VMEM((tm, d), BF16)]
    if split:
        out_specs = [pl.BlockSpec((tm, d), lambda i: (jnp.minimum(i, n_a - 1), 0)),
                     pl.BlockSpec((tm, d), lambda i: (jnp.maximum(i - n_a, 0), 0))]
        out_shape = [jax.ShapeDtypeStruct((npr, d), F32), jax.ShapeDtypeStruct((m - npr, d), F32)]
        scratch.append(pltpu.VMEM((tm, d), F32))
    else:
        out_specs = pl.BlockSpec((tm, d), lambda i: (i, 0))
        out_shape = jax.ShapeDtypeStruct((m, d), F32)
    scratch += [pltpu.VMEM((FFN_SLOTS, d, th), F32), pltpu.VMEM((FFN_SLOTS, d, th), F32),
                pltpu.VMEM((FFN_SLOTS, th, d), F32), pltpu.SemaphoreType.DMA((3, FFN_SLOTS))]
    assert nj >= 2 and hid % th == 0
    return pl.pallas_call(
        functools.partial(_ffn_kernel, d=d, n_in_a=n_a if pair else None, n_out_a=n_a if split else None,
                          fused=fused, layer=layer, hid=hid, th=th, ntiles=m // tm),
        grid=(m // tm,),
        in_specs=in_specs,
        out_specs=out_specs,
        out_shape=out_shape,
        scratch_shapes=scratch,
        compiler_params=_cparams(("arbitrary",)),
        name="ffn_proj" if fused else "ffn",
    )(*args)


def _conv_a_kernel(x_ref, mod_ref, g_ref, w_ref, b_ref, o_ref, w_sc, *, d):
    @pl.when(pl.program_id(0) == 0)
    def _():
        w_sc[...] = w_ref[0].astype(BF16)

    h = _normmod(x_ref[...], g_ref[0], mod_ref[0, 0, :, 0:d], mod_ref[0, 0, :, d:2 * d]).astype(BF16)
    y = jnp.dot(h, w_sc[...], preferred_element_type=F32) + b_ref[0]
    u = y[:, :d] * jax.nn.sigmoid(y[:, d:])
    for cb in range(d // LANES):
        o_ref[cb] = u[:, cb * LANES:(cb + 1) * LANES]


def _conv_b_kernel(*refs, d, t, rc, top):
    (x_ref, u_ref, mod_ref, wdw_ref, bdw_ref, gln_ref, bln_ref, w2_ref, b2_ref) = refs[:9]
    o_ref, upad, conv_sc, y_sc, w_sc = refs[-5:]
    nb = d // LANES
    half = (CONV_WIDTH - 1) // 2

    @pl.when(pl.program_id(0) == 0)
    def _():
        w_sc[...] = w2_ref[0].astype(BF16)
        upad[:, 0:top, :] = jnp.zeros((nb, top, LANES), F32)
        upad[:, top + t:2 * top + t, :] = jnp.zeros((nb, top, LANES), F32)

    upad[:, top:top + t, :] = u_ref[...]

    def conv_block(cb, carry):
        for r0 in range(0, t, rc):
            acc = jnp.broadcast_to(bdw_ref[cb], (rc, LANES))
            for k in range(CONV_WIDTH):
                acc = acc + wdw_ref[cb, pl.ds(k, 1), :] * upad[cb, pl.ds(r0 + top - half + k, rc), :]
            conv_sc[cb, pl.ds(r0, rc), :] = acc
        return carry

    lax.fori_loop(0, nb, conv_block, 0)

    rl = min(t, 256)
    for r0 in range(0, t, rl):
        s = jnp.zeros((rl, 1), F32)
        for cb in range(nb):
            s = s + jnp.sum(conv_sc[cb, r0:r0 + rl, :], axis=-1, keepdims=True)
        mean = s * (1.0 / d)
        v = jnp.zeros((rl, 1), F32)
        for cb in range(nb):
            c = conv_sc[cb, r0:r0 + rl, :] - mean
            v = v + jnp.sum(c * c, axis=-1, keepdims=True)
        rstd = lax.rsqrt(v * (1.0 / d) + EPS)
        for cb in range(nb):
            sl = slice(cb * LANES, (cb + 1) * LANES)
            y = (conv_sc[cb, r0:r0 + rl, :] - mean) * rstd * gln_ref[:, sl] + bln_ref[:, sl]
            y_sc[r0:r0 + rl, sl] = _silu(y).astype(BF16)

    gate = mod_ref[0, 0, :, 2 * d:3 * d]
    r = jnp.dot(y_sc[...], w_sc[...], preferred_element_type=F32) + b2_ref[0]
    o_ref[...] = x_ref[...] + gate * r


def _conv_b(x, u, mod4, wdw, bdw, gln, bln, w2, b2, layer, j, t, row0, nseq, cond0, cond_step, obuf):
    m, d = x.shape
    nb = d // LANES
    top = 16
    blk0 = row0 // t
    in_specs = [pl.BlockSpec((t, d), lambda i: (blk0 + i, 0)),
                pl.BlockSpec((nb, t, LANES), lambda i: (0, blk0 + i, 0)),
                pl.BlockSpec((1, 1, 1, 6 * d), lambda i: (layer, cond0 + cond_step * i, 0, 0)),
                pl.BlockSpec((nb, 32, LANES), lambda i: (0, 0, 0)),
                pl.BlockSpec((nb, 1, LANES), lambda i: (0, 0, 0)),
                pl.BlockSpec((1, d), lambda i: (0, 0)),
                pl.BlockSpec((1, d), lambda i: (0, 0)),
                pl.BlockSpec((1, d, d), lambda i: (j, 0, 0)),
                pl.BlockSpec((1, 1, d), lambda i: (j, 0, 0))]
    args = [x, u, mod4, wdw, bdw, gln, bln, w2, b2]
    aliases = {}
    if obuf is not None:
        in_specs.append(pl.BlockSpec(memory_space=pl.ANY))
        args.append(obuf)
        aliases = {len(args) - 1: 0}
    return pl.pallas_call(
        functools.partial(_conv_b_kernel, d=d, t=t, rc=64, top=top),
        grid=(nseq,),
        in_specs=in_specs,
        out_specs=pl.BlockSpec((t, d), lambda i: (blk0 + i, 0)),
        out_shape=jax.ShapeDtypeStruct((m, d), F32),
        input_output_aliases=aliases,
        scratch_shapes=[pltpu.VMEM((nb, t + 2 * top, LANES), F32),
                        pltpu.VMEM((nb, t, LANES), F32),
                        pltpu.VMEM((t, d), BF16),
                        pltpu.VMEM((d, d), BF16)],
        compiler_params=_cparams(("arbitrary",)),
        name="conv_b",
    )(*args)


def _conv_layer(x, mod4, P, layer, j, npr, seq_p, seq_s):
    m, d = x.shape
    nb = d // LANES
    tm = 256
    cidx = _cond_index(tm, npr, seq_s)
    u = pl.pallas_call(
        functools.partial(_conv_a_kernel, d=d),
        grid=(m // tm,),
        in_specs=[pl.BlockSpec((tm, d), lambda i: (i, 0)),
                  _mod_spec(layer, cidx, 6 * d, 1),
                  pl.BlockSpec((1, 1, d), lambda i: (layer, 0, 0)),
                  pl.BlockSpec((1, d, 2 * d), lambda i: (j, 0, 0)),
                  pl.BlockSpec((1, 1, 2 * d), lambda i: (j, 0, 0))],
        out_specs=pl.BlockSpec((nb, tm, LANES), lambda i: (0, i, 0)),
        out_shape=jax.ShapeDtypeStruct((nb, m, LANES), F32),
        scratch_shapes=[pltpu.VMEM((d, 2 * d), BF16)],
        compiler_params=_cparams(("arbitrary",)),
        name="conv_a",
    )(x, mod4, P["g_norm1"].reshape(-1, 1, d), P["cv_w_pw1"], P["cv_b_pw1"].reshape(-1, 1, 2 * d))
    wdw = jnp.pad(P["cv_w_dw"][j], ((0, 32 - CONV_WIDTH), (0, 0))).reshape(32, nb, LANES).transpose(1, 0, 2)
    bdw = P["cv_b_dw"][j].reshape(nb, 1, LANES)
    gln = P["cv_g_ln"][j].reshape(1, d)
    bln = P["cv_b_ln"][j].reshape(1, d)
    b2 = P["cv_b_pw2"].reshape(-1, 1, d)
    args = (x, u, mod4, wdw, bdw, gln, bln, P["cv_w_pw2"], b2, layer, j)
    op = _conv_b(*args, seq_p, 0, npr // seq_p, 0, 0, None)
    return _conv_b(*args, seq_s, npr, (m - npr) // seq_s, 1, 1, op)


_LOG2E = 1.4426950408889634
_ROT = QK_ROPE // 4


def _swap_index():
    idx = list(range(HEAD_PAD))
    for base in (QK_NOPE, QK_NOPE + 2 * _ROT):
        for l in range(_ROT):
            idx[base + l], idx[base + _ROT + l] = base + _ROT + l, base + l
    return jnp.asarray(idx, jnp.int32)


def _swap_cols(w):
    lane = jnp.arange(HEAD_PAD)
    rot = (lane >= QK_NOPE) & (lane < QK_DIM)
    return jnp.where(rot, jnp.take(w, _swap_index(), axis=-1), 0.0)


def _rope_tables(seq, gqn, gkn):
    rows = seq // GRID_W
    row = jnp.repeat(jnp.arange(rows, dtype=F32), GRID_W)
    col = jnp.tile(jnp.arange(GRID_W, dtype=F32), rows)
    axis = QK_ROPE // 2
    inv = ROPE_BASE ** (-jnp.arange(0, axis, 2, dtype=F32) / axis)
    ar, ac = row[:, None] * inv, col[:, None] * inv
    cr, sr, cc, sc = jnp.cos(ar), jnp.sin(ar), jnp.cos(ac), jnp.sin(ac)
    one = jnp.ones((seq, QK_NOPE), F32)
    zn = jnp.zeros((seq, QK_NOPE), F32)
    zp = jnp.zeros((seq, HEAD_PAD - QK_DIM), F32)
    c = jnp.concatenate([one, cr, cr, cc, cc, zp], axis=1)
    s = jnp.concatenate([zn, -sr, sr, -sc, sc, zp], axis=1)
    idx = _swap_index()
    return tuple(t for g in (gqn, gkn) for t in (c * g, s * jnp.take(g, idx, axis=-1)))


def _head_rstd(xh):
    ms = jnp.sum(xh * xh, axis=-1, keepdims=True) * (1.0 / QK_DIM)
    return lax.rsqrt(ms + EPS)


def _expand_kv(ckv_b, kpe, wuk_ref, wv_ref, gk, rot, k_ref, v_ref, v_transposed):
    kk = jnp.dot(ckv_b, wuk_ref[...], preferred_element_type=F32)
    if v_transposed:
        v = lax.dot_general(wv_ref[...], ckv_b, _NT, preferred_element_type=F32)
    else:
        v = jnp.dot(ckv_b, wv_ref[...], preferred_element_type=F32)
    v_ref[...] = v.astype(BF16)
    for h in range(MLA_HEADS):
        sl = slice(h * HEAD_PAD, (h + 1) * HEAD_PAD)
        kh = kk[:, sl] + kpe
        r = _head_rstd(kh)
        kh = (kh * gk) * r if rot is None else (kh * rot[0] + rot[1]) * r
        k_ref[:, sl] = kh.astype(BF16)


def _mla_proj_kernel(*refs, d, rope):
    (x_ref, mod_ref, g_ref, wcat_ref, gq_ref, wuq_ref, gkv_ref, wuk_ref, wv_ref, gqn_ref, gkn_ref) = refs[:11]
    if rope:
        cq_ref, sq_ref, ck_ref, sk_ref = refs[11:15]
        q_ref, k_ref, v_ref = refs[15:18]
    else:
        q_ref, k_ref, v_ref, ckv_ref, kpe_ref = refs[11:16]
    h = _normmod(x_ref[...], g_ref[0], mod_ref[0, 0, :, 0:d], mod_ref[0, 0, :, d:2 * d]).astype(BF16)
    t = jnp.dot(h, wcat_ref[...], preferred_element_type=F32)
    nq = gq_ref.shape[1]
    nkv = gkv_ref.shape[1]
    hq = MLA_HEADS * HEAD_PAD
    qd, craw, kpe = t[:, :nq], t[:, nq:nq + nkv], t[:, nq + nkv:nq + nkv + HEAD_PAD]
    cq = (qd * lax.rsqrt(jnp.mean(qd * qd, axis=-1, keepdims=True) + EPS) * gq_ref[...]).astype(BF16)
    ckv = craw * lax.rsqrt(jnp.mean(craw * craw, axis=-1, keepdims=True) + EPS) * gkv_ref[...]
    if not rope:
        ckv_ref[...] = ckv
        kpe_ref[...] = kpe
    q = jnp.dot(cq, wuq_ref[...], preferred_element_type=F32)
    if rope:
        ct, st = cq_ref[...], sq_ref[...]
    else:
        gqn = gqn_ref[...]
    for hd in range(MLA_HEADS):
        sl = slice(hd * HEAD_PAD, (hd + 1) * HEAD_PAD)
        qh = q[:, sl]
        r = _head_rstd(qh)
        qh = (qh * ct + q[:, hq + hd * HEAD_PAD:hq + (hd + 1) * HEAD_PAD] * st) * r if rope else (qh * gqn) * r
        q_ref[:, sl] = qh.astype(BF16)
    rot = None
    if rope:
        kpe2 = t[:, nq + nkv + HEAD_PAD:]
        rot = (ck_ref[...], kpe2 * sk_ref[...])
    _expand_kv(ckv.astype(BF16), kpe, wuk_ref, wv_ref, gkn_ref[...], rot, k_ref, v_ref, rope)


def _kv_ctx_kernel(ckv_ref, kpe_ref, wuk_ref, wv_ref, gkn_ref, k_ref, v_ref):
    _expand_kv(ckv_ref[...].astype(BF16), kpe_ref[...], wuk_ref, wv_ref, gkn_ref[...], None, k_ref, v_ref, True)


_NT = (((1,), (1,)), ((), ()))
ONES_ROWS = 16


def _attn_prompt_kernel(q_ref, k_ref, v_ref, x_ref, mod_ref, wo_ref, o_ref, a_sc, *, d):
    t = q_ref.shape[0]
    lane = lax.broadcasted_iota(jnp.int32, (t, LANES), 1)
    for hp in range(MLA_HEADS // 2):
        vp = v_ref[:, hp * LANES:(hp + 1) * LANES]
        outs = []
        for hh in range(2):
            sl = slice((2 * hp + hh) * HEAD_PAD, (2 * hp + hh + 1) * HEAD_PAD)
            s = lax.dot_general(q_ref[:, sl], k_ref[:, sl], _NT, preferred_element_type=F32)
            e = jnp.exp2(s - s.max(axis=-1, keepdims=True))
            l = jnp.sum(e, axis=-1, keepdims=True)
            outs.append(jnp.dot(e.astype(BF16), vp, preferred_element_type=F32) * (1.0 / l))
        a_sc[:, hp * LANES:(hp + 1) * LANES] = jnp.where(lane < V_HEAD, outs[0], outs[1]).astype(BF16)
    gate = mod_ref[0, 0, :, 2 * d:3 * d]
    o_ref[...] = x_ref[...] + gate * jnp.dot(a_sc[...], wo_ref[...], preferred_element_type=F32)


def _attn_sample_kernel(q_ref, kc_ref, vtc_ref, kl_ref, vtl_ref, x_ref, mod_ref, wo_ref, obuf_ref, o_ref, *, d):
    del obuf_ref
    nh = q_ref.shape[1] // HEAD_PAD

    def scores(hh):
        sl = slice(hh * HEAD_PAD, (hh + 1) * HEAD_PAD)
        qh = q_ref[:, sl]
        return [lax.dot_general(k_ref[:, sl], qh, _NT, preferred_element_type=F32)
                for k_ref in (kc_ref, kl_ref)]

    halves = []
    nxt = scores(0)
    for hh in range(nh):
        sts = nxt
        if hh + 1 < nh:
            nxt = scores(hh + 1)
        m = jnp.max(sts[0], axis=0, keepdims=True)
        for st in sts[1:]:
            m = jnp.maximum(m, jnp.max(st, axis=0, keepdims=True))
        acc = None
        for st, vt_ref in zip(sts, (vtc_ref, vtl_ref)):
            e = jnp.exp2(st - m).astype(BF16)
            vt = vt_ref[hh * V_HEAD:(hh + 1) * V_HEAD, :]
            lhs = jnp.concatenate([vt, jnp.ones((ONES_ROWS, vt.shape[1]), BF16)], axis=0)
            pv = jnp.dot(lhs, e, preferred_element_type=F32)
            acc = pv if acc is None else acc + pv
        halves.append(acc[0:V_HEAD] * (1.0 / acc[V_HEAD:V_HEAD + 1]))
    attn = jnp.concatenate(halves, axis=0).T.astype(BF16)
    gate = mod_ref[0, 0, :, 2 * d:3 * d]
    o_ref[...] = x_ref[...] + gate * jnp.dot(attn, wo_ref[...], preferred_element_type=F32)


def _mla_weights(P, j):
    d = P["mla_w_dq"].shape[1]
    nq = P["mla_w_dq"].shape[2]
    nkv = P["mla_g_kv"].shape[1]
    pad = HEAD_PAD - QK_DIM
    w_dkv = P["mla_w_dkv"][j]
    w_kpe = jnp.pad(w_dkv[:, nkv:], ((0, 0), (QK_NOPE, pad)))
    wcat = jnp.concatenate([P["mla_w_dq"][j], w_dkv[:, :nkv], w_kpe], axis=1).astype(BF16)
    wuq = jnp.pad(P["mla_w_uq"][j].reshape(nq, MLA_HEADS, QK_DIM), ((0, 0), (0, 0), (0, pad)))
    wuq = wuq.reshape(nq, MLA_HEADS * HEAD_PAD).astype(BF16)
    wukv = P["mla_w_ukv"][j].reshape(nkv, MLA_HEADS, QK_NOPE + V_HEAD)
    wuk = jnp.pad(wukv[:, :, :QK_NOPE], ((0, 0), (0, 0), (0, HEAD_PAD - QK_NOPE)))
    wuk = wuk.reshape(nkv, MLA_HEADS * HEAD_PAD).astype(BF16)
    wv = wukv[:, :, QK_NOPE:].reshape(nkv, MLA_HEADS * V_HEAD).astype(BF16)
    gq = P["mla_g_q"][j].reshape(1, nq)
    gkv = P["mla_g_kv"][j].reshape(1, nkv)
    gqn = (jnp.pad(P["mla_g_qn"][j], (0, pad)) * (QK_DIM ** -0.5 * _LOG2E)).reshape(1, HEAD_PAD)
    gkn = jnp.pad(P["mla_g_kn"][j], (0, pad)).reshape(1, HEAD_PAD)
    wuq2 = _swap_cols(wuq.reshape(nq, MLA_HEADS, HEAD_PAD)).reshape(nq, MLA_HEADS * HEAD_PAD)
    wuq_r = jnp.concatenate([wuq, wuq2], axis=1)
    wcat_r = jnp.concatenate([wcat, _swap_cols(w_kpe).astype(BF16)], axis=1)
    return (wcat, gq, wuq, gkv, wuk, wv, gqn, gkn), (wcat_r, wuq_r, wv.T)


def _full(shape):
    return pl.BlockSpec(shape, lambda *_: (0,) * len(shape))


def _mla_proj(x, xblk0, mod4, g1, W, layer, row0, nrows, cidx, tm, rope_args):
    d = x.shape[1]
    wcat, gq, wuq, gkv, wuk, wv, gqn, gkn = W
    blk0 = row0 // tm
    hq = MLA_HEADS * HEAD_PAD
    hv = MLA_HEADS * V_HEAD
    nkv = gkv.shape[1]
    rope = rope_args is not None
    in_specs = [pl.BlockSpec((tm, d), lambda i: (xblk0 + i, 0)),
                pl.BlockSpec((1, 1, 1, 6 * d), lambda i: (layer, cidx(blk0 + i), 0, 0)),
                pl.BlockSpec((1, 1, d), lambda i: (layer, 0, 0)),
                _full(wcat.shape), _full(gq.shape), _full(wuq.shape), _full(gkv.shape), _full(wuk.shape),
                _full(wv.shape), _full(gqn.shape), _full(gkn.shape)]
    args = [x, mod4, g1, wcat, gq, wuq, gkv, wuk, wv, gqn, gkn]
    out_specs = [pl.BlockSpec((tm, hq), lambda i: (i, 0)),
                 pl.BlockSpec((tm, hq), lambda i: (i, 0))]
    out_shape = [jax.ShapeDtypeStruct((nrows, hq), BF16),
                 jax.ShapeDtypeStruct((nrows, hq), BF16)]
    if rope:
        wcat_r, wuq_r, wvt, tabs = rope_args
        per = tabs[0].shape[0] // tm
        in_specs[3], in_specs[5], in_specs[8] = _full(wcat_r.shape), _full(wuq_r.shape), _full(wvt.shape)
        args[3], args[5], args[8] = wcat_r, wuq_r, wvt
        in_specs += [pl.BlockSpec((tm, HEAD_PAD), lambda i: (i % per, 0))] * 4
        args += list(tabs)
        out_specs += [pl.BlockSpec((hv, tm), lambda i: (0, i))]
        out_shape += [jax.ShapeDtypeStruct((hv, nrows), BF16)]
    else:
        out_specs += [pl.BlockSpec((tm, hv), lambda i: (i, 0)),
                      pl.BlockSpec((tm, nkv), lambda i: (i, 0)), pl.BlockSpec((tm, HEAD_PAD), lambda i: (i, 0))]
        out_shape += [jax.ShapeDtypeStruct((nrows, hv), BF16),
                      jax.ShapeDtypeStruct((nrows, nkv), F32), jax.ShapeDtypeStruct((nrows, HEAD_PAD), F32)]
    return pl.pallas_call(
        functools.partial(_mla_proj_kernel, d=d, rope=rope),
        grid=(nrows // tm,),
        in_specs=in_specs, out_specs=out_specs, out_shape=out_shape,
        compiler_params=_cparams(("arbitrary",)),
        name="mla_proj_rope" if rope else "mla_proj",
    )(*args)


def _mla_layer(x, mod4, P, layer, j, cache_ckv, cache_kpe, npr, seq_p, seq_s):
    xa, xb = x if isinstance(x, tuple) else (x, x)
    d = xa.shape[1]
    nrs = xb.shape[0] if isinstance(x, tuple) else xb.shape[0] - npr
    m = npr + nrs
    nbp, nbs = npr // seq_p, nrs // seq_s
    past = cache_ckv.shape[1]
    W, (wcat_r, wuq_r, wvt) = _mla_weights(P, j)
    wuk, gqn, gkn = W[4], W[6], W[7]
    g1 = P["g_norm1"].reshape(-1, 1, d)
    hq = MLA_HEADS * HEAD_PAD
    hv = MLA_HEADS * V_HEAD
    tm = 256
    cidx = _cond_index(tm, npr, seq_s)
    xb0 = 0 if isinstance(x, tuple) else npr // tm
    tabs = _rope_tables(seq_s, gqn[0], gkn[0])
    qp, kp, vp, ckv, kpe128 = _mla_proj(xa, 0, mod4, g1, W, layer, 0, npr, cidx, tm, None)
    qs, ks, vts = _mla_proj(xb, xb0, mod4, g1, W, layer, npr, nrs, cidx, tm, (wcat_r, wuq_r, wvt, tabs))
    cc = cache_ckv.reshape(nbs * past, -1)
    ck = jnp.pad(cache_kpe.reshape(nbs * past, -1), ((0, 0), (QK_NOPE, HEAD_PAD - QK_DIM)))
    kc, vtc = pl.pallas_call(
        _kv_ctx_kernel,
        grid=(nbs,),
        in_specs=[pl.BlockSpec((past, cc.shape[1]), lambda i: (i, 0)),
                  pl.BlockSpec((past, HEAD_PAD), lambda i: (i, 0)),
                  _full(wuk.shape), _full(wvt.shape), _full(gkn.shape)],
        out_specs=[pl.BlockSpec((past, hq), lambda i: (i, 0)), pl.BlockSpec((hv, past), lambda i: (0, i))],
        out_shape=[jax.ShapeDtypeStruct((nbs * past, hq), BF16), jax.ShapeDtypeStruct((hv, nbs * past), BF16)],
        compiler_params=_cparams(("arbitrary",)),
        name="mla_kv_ctx",
    )(cc, ck, wuk, wvt, gkn)
    wo = P["mla_w_o"][j].astype(BF16)
    xp0 = 0
    xs0 = 0 if isinstance(x, tuple) else npr
    xn_p = pl.pallas_call(
        functools.partial(_attn_prompt_kernel, d=d),
        grid=(nbp,),
        in_specs=[pl.BlockSpec((seq_p, hq), lambda b: (b, 0)),
                  pl.BlockSpec((seq_p, hq), lambda b: (b, 0)),
                  pl.BlockSpec((seq_p, hv), lambda b: (b, 0)),
                  pl.BlockSpec((seq_p, d), lambda b: (xp0 // seq_p + b, 0)),
                  pl.BlockSpec((1, 1, 1, 6 * d), lambda b: (layer, 0, 0, 0)),
                  _full(wo.shape)],
        out_specs=pl.BlockSpec((seq_p, d), lambda b: (b, 0)),
        out_shape=jax.ShapeDtypeStruct((m, d), F32),
        scratch_shapes=[pltpu.VMEM((seq_p, hv), BF16)],
        compiler_params=_cparams(("arbitrary",)),
        name="attn_prompt",
    )(qp, kp, vp, xa, mod4, wo)
    tq = min(seq_s, 512)
    nqt = seq_s // tq
    o0 = npr // tq
    xn = pl.pallas_call(
        functools.partial(_attn_sample_kernel, d=d),
        grid=(nbs, nqt),
        in_specs=[pl.BlockSpec((tq, hq), lambda b, qi: (b * nqt + qi, 0)),
                  pl.BlockSpec((past, hq), lambda b, qi: (b, 0)),
                  pl.BlockSpec((hv, past), lambda b, qi: (0, b)),
                  pl.BlockSpec((seq_s, hq), lambda b, qi: (b, 0)),
                  pl.BlockSpec((hv, seq_s), lambda b, qi: (0, b)),
                  pl.BlockSpec((tq, d), lambda b, qi: (xs0 // tq + b * nqt + qi, 0)),
                  pl.BlockSpec((1, 1, 1, 6 * d), lambda b, qi: (layer, 1 + b, 0, 0)),
                  _full(wo.shape),
                  pl.BlockSpec(memory_space=pl.ANY)],
        out_specs=pl.BlockSpec((tq, d), lambda b, qi: (o0 + b * nqt + qi, 0)),
        out_shape=jax.ShapeDtypeStruct((m, d), F32),
        input_output_aliases={8: 0},
        compiler_params=_cparams(("arbitrary", "arbitrary")),
        name="attn_sample",
    )(qs, kc, vtc, ks, vts, xb, mod4, wo, xn_p)
    new_ckv = ckv.reshape(nbp, seq_p, -1)
    new_kpe = kpe128[:, QK_NOPE:QK_DIM].reshape(nbp, seq_p, QK_ROPE)
    return xn, new_ckv, new_kpe


SSD_R = 8
SSD_GW = SSD_R * SSD_HEADDIM
SSD_NX = SSD_GW // LANES
SSD_DTW = 2 * SSD_R
SSD_TOP = 8


def _ssd_in_kernel(x_ref, mod_ref, g_ref, w_ref, wdt_ref, zx_ref, dt_ref, h_sc, *, d, heads):
    @pl.when(pl.program_id(1) == 0)
    def _():
        h = _normmod(x_ref[...], g_ref[0], mod_ref[0, 0, :, 0:d], mod_ref[0, 0, :, d:2 * d]).astype(BF16)
        h_sc[...] = h
        row = lax.broadcasted_iota(jnp.int32, wdt_ref.shape[1:], 0)
        wdt = jnp.where(row < 2 * heads, wdt_ref[0], 0.0).astype(BF16)
        dtt = lax.dot_general(wdt, h, _NT, preferred_element_type=F32)
        for g in range(SSD_GROUPS):
            dt_ref[g, 0:SSD_R, :] = dtt[g * SSD_R:(g + 1) * SSD_R, :]
            dt_ref[g, SSD_R:SSD_DTW, :] = dtt[heads + g * SSD_R:heads + (g + 1) * SSD_R, :]

    r = lax.dot_general(h_sc[...], w_ref[0].astype(BF16), _NT, preferred_element_type=F32)
    for c in range(r.shape[1] // LANES):
        zx_ref[c] = r[:, c * LANES:(c + 1) * LANES]


def _split3(x):
    hi = x.astype(BF16)
    r1 = x - hi.astype(F32)
    mid = r1.astype(BF16)
    lo = (r1 - mid.astype(F32)).astype(BF16)
    return hi, mid, lo


def _dot3_right(x, tri):
    return sum(jnp.dot(p, tri, preferred_element_type=F32) for p in _split3(x))


SSD_NPIECE = 7


def _ssd_spread_matrix():
    sel = np.zeros((2, CHUNK, SSD_R * LANES + 2 * SSD_GW), np.float32)
    for dr in range(2):
        for r in range(SSD_R):
            h = dr * SSD_R + r
            for p in range(3):
                sel[dr, p * SSD_DTW + h, r * LANES:(r + 1) * LANES] = 1.0
            for p in range(3, 5):
                sel[dr, p * SSD_DTW + h, SSD_R * LANES + r * SSD_HEADDIM:SSD_R * LANES + (r + 1) * SSD_HEADDIM] = 1.0
            for p in range(5, 7):
                c0 = SSD_R * LANES + SSD_GW
                sel[dr, p * SSD_DTW + h, c0 + r * SSD_HEADDIM:c0 + (r + 1) * SSD_HEADDIM] = 1.0
    return jnp.asarray(sel, BF16)


def _ssd_core_kernel(*refs, t, has_h0):
    (z_ref, x_ref, b_ref, c_ref, dt_ref, wx_ref, wb_ref, wc_ref, bx_ref, bb_ref, bc_ref,
     dtb_ref, alog_ref, dsum_ref, gn_ref, bsel_ref) = refs[:16]
    if has_h0:
        h0_ref, y_ref = refs[16], refs[18]
        st_ref = None
        scr = refs[19:]
    else:
        y_ref, st_ref = refs[16:18]
        h0_ref = None
        scr = refs[18:]
    (pad_sc, xc_sc, bcv_sc, ccv_sc, bt_sc, cb_sc, y_sc, dtt_sc, dat_sc, s_sc) = scr
    nc = t // CHUNK
    half = (SSD_CONV - 1) // 2
    nblk = SSD_NX + 2

    @pl.when((pl.program_id(0) == 0) & (pl.program_id(1) == 0))
    def _():
        pad_sc[:, 0:SSD_TOP, :] = jnp.zeros((nblk, SSD_TOP, LANES), F32)
        pad_sc[:, SSD_TOP + t:2 * SSD_TOP + t, :] = jnp.zeros((nblk, SSD_TOP, LANES), F32)

    pad_sc[0:SSD_NX, SSD_TOP:SSD_TOP + t, :] = x_ref[...]
    pad_sc[SSD_NX, SSD_TOP:SSD_TOP + t, :] = b_ref[0]
    pad_sc[SSD_NX + 1, SSD_TOP:SSD_TOP + t, :] = c_ref[0]

    def conv_chunk(c):
        r0 = c * CHUNK
        for blk in range(nblk):
            if blk < SSD_NX:
                w_r, bias, widx = wx_ref, bx_ref[blk], blk
            elif blk == SSD_NX:
                w_r, bias, widx = wb_ref, bb_ref[0], 0
            else:
                w_r, bias, widx = wc_ref, bc_ref[0], 0
            acc = jnp.broadcast_to(bias, (CHUNK, LANES))
            for k in range(SSD_CONV):
                acc = acc + w_r[widx, k:k + 1, :] * pad_sc[blk, r0 + SSD_TOP - half + k:r0 + SSD_TOP - half + k + CHUNK, :]
            acc = _silu(acc)
            if blk < SSD_NX:
                xc_sc[blk, r0:r0 + CHUNK, :] = acc
            elif blk == SSD_NX:
                bcv_sc[r0:r0 + CHUNK, :] = acc.astype(BF16)
                bt_sc[c] = acc.T.astype(BF16)
            else:
                ccv_sc[r0:r0 + CHUNK, :] = acc.astype(BF16)

    def gate_chunk(c):
        rows = slice(c * CHUNK, (c + 1) * CHUNK)
        yz = [y_sc[q, rows, :] * _silu(z_ref[q, rows, :]) for q in range(SSD_NX)]
        ms = sum(jnp.sum(a * a, axis=-1, keepdims=True) for a in yz) * (1.0 / SSD_GW)
        rstd = lax.rsqrt(ms + EPS)
        for q in range(SSD_NX):
            sl = slice(q * LANES, (q + 1) * LANES)
            y_ref[rows, sl] = (yz[q] * rstd * gn_ref[0, :, sl]).astype(BF16)

    a_row = -jnp.exp(alog_ref[0]) * _LOG2E
    for c in range(nc):
        v = dt_ref[0, :, c * CHUNK:(c + 1) * CHUNK] + dtb_ref[0]
        dt = jnp.maximum(v, 0.0) + jnp.log1p(jnp.exp(-jnp.abs(v)))
        dtt_sc[c] = dt
        dat_sc[c] = dt * a_row

    row = lax.broadcasted_iota(jnp.int32, (CHUNK, CHUNK), 0)
    colm = lax.broadcasted_iota(jnp.int32, (CHUNK, CHUNK), 1)
    lower = colm <= row
    upper = colm >= row
    tri_l = jnp.where(lower, 1.0, 0.0).astype(BF16)
    tri_u = jnp.where(upper, 1.0, 0.0).astype(BF16)
    lane = lax.broadcasted_iota(jnp.int32, (CHUNK, LANES), 1)
    head_row = lax.broadcasted_iota(jnp.int32, (SSD_DTW, 1), 0)
    zero16 = jnp.zeros((SSD_DTW, CHUNK), F32)

    col0, w0, e0 = 0, SSD_R * LANES, SSD_R * LANES + SSD_GW
    zpad = jnp.zeros((CHUNK - SSD_NPIECE * SSD_DTW, CHUNK), F32)

    def prepare(dr, c):
        lo = dr * SSD_R
        tri_t = tri_l if dr else tri_u
        own = (head_row >= lo) & (head_row < lo + SSD_R)
        rows = slice(c * CHUNK, (c + 1) * CHUNK)
        dtt = dtt_sc[c]
        cum_t = _dot3_right(dat_sc[c], tri_t)
        edge = cum_t[:, 0:1] if dr else cum_t[:, CHUNK - 1:CHUNK]
        w2_t = dtt * jnp.exp2(edge - cum_t)
        ecum_t = jnp.exp2(cum_t)
        pieces = (list(_split3(jnp.where(own, cum_t, zero16)))
                  + list(_split3(jnp.where(own, w2_t, zero16))[:2])
                  + list(_split3(jnp.where(own, ecum_t, zero16))[:2]))
        staged = jnp.concatenate([p.astype(F32) for p in pieces] + [zpad], axis=0)
        spread = jnp.dot(staged.T.astype(BF16), bsel_ref[dr], preferred_element_type=F32)
        cmat = ccv_sc[rows, :]
        if dr == 0:
            cb = lax.dot_general(cmat, bcv_sc[rows, :], _NT, preferred_element_type=F32)
            cb_sc[c] = cb
        else:
            cb = cb_sc[c]
        rowterm = cum_t - jnp.log2(dtt)
        return dict(dr=dr, c=c, rows=rows, lo=lo, rowterm=rowterm, spread=spread, cmat=cmat, cb=cb)

    def scan_chunk(pre):
        dr, c, rows, lo = pre["dr"], pre["c"], pre["rows"], pre["lo"]
        rowterm, spread, cmat, cb = pre["rowterm"], pre["spread"], pre["cmat"], pre["cb"]
        mask = upper if dr else lower
        s_prev = s_sc[...]
        yoff = jnp.dot(cmat, s_prev.astype(BF16), preferred_element_type=F32)
        xd = []
        for q in range(SSD_NX):
            xblk = xc_sc[q, rows, :]
            xb16 = xblk.astype(BF16)
            mh = []
            for r in (2 * q, 2 * q + 1):
                seg = spread[:, col0 + r * LANES:col0 + (r + 1) * LANES] - rowterm[lo + r:lo + r + 1, :]
                lm = jnp.exp2(jnp.where(mask, seg, -jnp.inf))
                mh.append((cb * lm).astype(BF16))
            rhs = jnp.concatenate([jnp.where(lane < SSD_HEADDIM, xb16, jnp.zeros_like(xb16)),
                                   jnp.where(lane >= SSD_HEADDIM, xb16, jnp.zeros_like(xb16))], axis=0)
            y = jnp.dot(jnp.concatenate(mh, axis=1), rhs, preferred_element_type=F32)
            sl = slice(q * LANES, (q + 1) * LANES)
            y = y + yoff[:, sl] * spread[:, e0 + q * LANES:e0 + (q + 1) * LANES]
            if dr == 0:
                y_sc[q, rows, :] = y + xblk * dsum_ref[0, :, sl]
            else:
                y_sc[q, rows, :] = y_sc[q, rows, :] + y
            xd.append((xblk * spread[:, w0 + q * LANES:w0 + (q + 1) * LANES]).astype(BF16))
        st = jnp.dot(bt_sc[c], jnp.concatenate(xd, axis=1), preferred_element_type=F32)
        er = 0 if dr else CHUNK - 1
        cd = spread[er:er + 1, e0:e0 + SSD_GW]
        s_sc[...] = s_prev * cd + st

    order = [(0, c) for c in range(nc)] + [(1, c) for c in range(nc - 1, -1, -1)]
    conv_chunk(0)
    pre = prepare(*order[0])
    for k, (dr, c) in enumerate(order):
        if c == (nc - 1 if dr else 0):
            if has_h0:
                s_sc[...] = h0_ref[0, dr, 0].T
            else:
                s_sc[...] = jnp.zeros_like(s_sc)
        if dr == 0 and c + 1 < nc:
            conv_chunk(c + 1)
        nxt = prepare(*order[k + 1]) if k + 1 < len(order) else None
        scan_chunk(pre)
        pre = nxt
        if dr == 1:
            gate_chunk(c)
        if st_ref is not None and c == (0 if dr else nc - 1):
            st_ref[0, dr, 0] = s_sc[...].T


def _ssd_core(zx, dtc, cw, cbias, dtb, alog, dsum, gn, h0, ybuf, t, row0, nseq):
    has_h0 = h0 is not None
    m = zx.shape[1]
    bsel = _ssd_spread_matrix()
    b0 = row0 // t
    nc = t // CHUNK
    nz = SSD_GROUPS * SSD_NX
    in_specs = [pl.BlockSpec((SSD_NX, t, LANES), lambda b, g: (g, b0 + b, 0)),
                pl.BlockSpec((SSD_NX, t, LANES), lambda b, g: (SSD_GROUPS + g, b0 + b, 0)),
                pl.BlockSpec((1, t, LANES), lambda b, g: (2 * nz + g, b0 + b, 0)),
                pl.BlockSpec((1, t, LANES), lambda b, g: (2 * nz + SSD_GROUPS + g, b0 + b, 0)),
                pl.BlockSpec((1, SSD_DTW, t), lambda b, g: (g, 0, b0 + b)),
                pl.BlockSpec((SSD_NX, 8, LANES), lambda b, g: (g, 0, 0)),
                pl.BlockSpec((1, 8, LANES), lambda b, g: (nz + g, 0, 0)),
                pl.BlockSpec((1, 8, LANES), lambda b, g: (nz + SSD_GROUPS + g, 0, 0)),
                pl.BlockSpec((SSD_NX, 1, LANES), lambda b, g: (g, 0, 0)),
                pl.BlockSpec((1, 1, LANES), lambda b, g: (nz + g, 0, 0)),
                pl.BlockSpec((1, 1, LANES), lambda b, g: (nz + SSD_GROUPS + g, 0, 0)),
                pl.BlockSpec((1, SSD_DTW, CHUNK), lambda b, g: (g, 0, 0)),
                pl.BlockSpec((1, SSD_DTW, CHUNK), lambda b, g: (g, 0, 0)),
                pl.BlockSpec((1, 1, SSD_GW), lambda b, g: (g, 0, 0)),
                pl.BlockSpec((1, 1, SSD_GW), lambda b, g: (g, 0, 0)),
                pl.BlockSpec(bsel.shape, lambda b, g: (0, 0, 0))]
    args = [zx, zx, zx, zx, dtc, cw, cw, cw, cbias, cbias, cbias, dtb, alog, dsum, gn, bsel]
    st_spec = pl.BlockSpec((1, 2, 1, SSD_GW, SSD_STATE), lambda b, g: (b, 0, g, 0, 0))
    y_spec = pl.BlockSpec((t, SSD_GW), lambda b, g: (b0 + b, g))
    y_shape = jax.ShapeDtypeStruct((m, SSD_GROUPS * SSD_GW), BF16)
    aliases = {}
    if has_h0:
        in_specs += [st_spec, pl.BlockSpec(memory_space=pl.ANY)]
        args += [h0, ybuf]
        aliases = {len(args) - 1: 0}
        out_specs, out_shape = y_spec, y_shape
    else:
        out_specs = [y_spec, st_spec]
        out_shape = [y_shape, jax.ShapeDtypeStruct((nseq, 2, SSD_GROUPS, SSD_GW, SSD_STATE), F32)]
    scratch = [pltpu.VMEM((SSD_NX + 2, t + 2 * SSD_TOP, LANES), F32),
               pltpu.VMEM((SSD_NX, t, LANES), F32),
               pltpu.VMEM((t, LANES), BF16),
               pltpu.VMEM((t, LANES), BF16),
               pltpu.VMEM((nc, CHUNK, CHUNK), BF16),
               pltpu.VMEM((nc, CHUNK, CHUNK), F32),
               pltpu.VMEM((SSD_NX, t, LANES), F32),
               pltpu.VMEM((nc, SSD_DTW, CHUNK), F32),
               pltpu.VMEM((nc, SSD_DTW, CHUNK), F32),
               pltpu.VMEM((SSD_STATE, SSD_GW), F32)]
    return pl.pallas_call(
        functools.partial(_ssd_core_kernel, t=t, has_h0=has_h0),
        grid=(nseq, SSD_GROUPS),
        in_specs=in_specs, out_specs=out_specs, out_shape=out_shape,
        input_output_aliases=aliases,
        scratch_shapes=scratch,
        compiler_params=_cparams(("arbitrary", "arbitrary")),
        name="ssd_core_h0" if has_h0 else "ssd_core",
    )(*args)


def _ssd_layer(x, mod4, P, layer, j, st0, npr, seq_p, seq_s):
    m, d = x.shape
    nbp, nbs = npr // seq_p, (m - npr) // seq_s
    heads = SSD_GROUPS * SSD_R
    inner = heads * SSD_HEADDIM
    gn_w = SSD_GROUPS * SSD_STATE
    nzx = 2 * inner + 2 * gn_w
    tm, tn = min(1024, seq_s), 1024
    cidx = _cond_index(tm, npr, seq_s)
    w_in_t = jnp.swapaxes(P["ssd_w_in"], 1, 2)
    zx, dtc = pl.pallas_call(
        functools.partial(_ssd_in_kernel, d=d, heads=heads),
        grid=(m // tm, nzx // tn),
        in_specs=[pl.BlockSpec((tm, d), lambda i, n: (i, 0)),
                  _mod_spec(layer, cidx, 6 * d, 2),
                  pl.BlockSpec((1, 1, d), lambda i, n: (layer, 0, 0)),
                  pl.BlockSpec((1, tn, d), lambda i, n: (j, n, 0)),
                  pl.BlockSpec((1, LANES, d), lambda i, n: (j, nzx // LANES, 0))],
        out_specs=[pl.BlockSpec((tn // LANES, tm, LANES), lambda i, n: (n, i, 0)),
                   pl.BlockSpec((SSD_GROUPS, SSD_DTW, tm), lambda i, n: (0, 0, i))],
        out_shape=[jax.ShapeDtypeStruct((nzx // LANES, m, LANES), F32),
                   jax.ShapeDtypeStruct((SSD_GROUPS, SSD_DTW, m), F32)],
        scratch_shapes=[pltpu.VMEM((tm, d), BF16)],
        compiler_params=_cparams(("arbitrary", "arbitrary")),
        name="ssd_in",
    )(x, mod4, P["g_norm1"].reshape(-1, 1, d), w_in_t, w_in_t)

    def compact(a):
        a = a.reshape(2, SSD_GROUPS, SSD_R).transpose(1, 0, 2).reshape(SSD_GROUPS, SSD_DTW, 1)
        return jnp.broadcast_to(a, (SSD_GROUPS, SSD_DTW, CHUNK))

    ncb = (inner + 2 * gn_w) // LANES
    cw = jnp.pad(P["ssd_w_conv"][j], ((0, 8 - SSD_CONV), (0, 0))).reshape(8, ncb, LANES).transpose(1, 0, 2)
    cbias = P["ssd_b_conv"][j].reshape(ncb, 1, LANES)
    dsum = jnp.repeat(P["ssd_d"][j][0] + P["ssd_d"][j][1], SSD_HEADDIM).reshape(SSD_GROUPS, 1, SSD_GW)
    gn = P["ssd_g_norm"][j].reshape(SSD_GROUPS, 1, SSD_GW)
    common = (zx, dtc, cw, cbias, compact(P["ssd_dt_bias"][j]), compact(P["ssd_a_log"][j]), dsum, gn)
    yp, st = _ssd_core(*common, None, None, seq_p, 0, nbp)
    h0 = st0.reshape(nbs, 2, SSD_GROUPS, SSD_GW, SSD_STATE)
    y = _ssd_core(*common, h0, yp, seq_s, npr, nbs)
    return y, st.reshape(nbp, 2, heads, SSD_HEADDIM, SSD_STATE)


def kernel(x_prompt, x_sample, cache_ckv, cache_kpe, state_ssm, c, c_ctx, w_ada, b_ada, g_norm1, g_norm2,
           mla_w_dq, mla_g_q, mla_w_uq, mla_w_dkv, mla_g_kv, mla_w_ukv, mla_g_qn, mla_g_kn, mla_w_o,
           cv_w_pw1, cv_b_pw1, cv_w_dw, cv_b_dw, cv_g_ln, cv_b_ln, cv_w_pw2, cv_b_pw2,
           ssd_w_in, ssd_w_conv, ssd_b_conv, ssd_dt_bias, ssd_a_log, ssd_d, ssd_g_norm, ssd_w_out,
           ffn_w_in, ffn_w_out):
    P = dict(g_norm1=g_norm1, g_norm2=g_norm2,
             mla_w_dq=mla_w_dq, mla_g_q=mla_g_q, mla_w_uq=mla_w_uq, mla_w_dkv=mla_w_dkv, mla_g_kv=mla_g_kv,
             mla_w_ukv=mla_w_ukv, mla_g_qn=mla_g_qn, mla_g_kn=mla_g_kn, mla_w_o=mla_w_o,
             cv_w_pw1=cv_w_pw1, cv_b_pw1=cv_b_pw1, cv_w_dw=cv_w_dw, cv_b_dw=cv_b_dw, cv_g_ln=cv_g_ln,
             cv_b_ln=cv_b_ln, cv_w_pw2=cv_w_pw2, cv_b_pw2=cv_b_pw2,
             ssd_w_in=ssd_w_in, ssd_w_conv=ssd_w_conv, ssd_b_conv=ssd_b_conv, ssd_dt_bias=ssd_dt_bias,
             ssd_a_log=ssd_a_log, ssd_d=ssd_d, ssd_g_norm=ssd_g_norm, ssd_w_out=ssd_w_out)
    nbp, seq_p, d = x_prompt.shape
    nbs, seq_s, _ = x_sample.shape
    depth = w_ada.shape[0]
    npr = nbp * seq_p
    x = (x_prompt.reshape(npr, d), x_sample.reshape(nbs * seq_s, d))
    cond8 = jnp.concatenate([c_ctx[None, :], c, jnp.zeros((8 - 1 - nbs, d), F32)], axis=0)
    mod4 = _ada(cond8, w_ada, b_ada).reshape(depth, 8, 1, 6 * d)
    ffn_tm = min(1024, seq_s)
    ffn_cidx = _cond_index(ffn_tm, npr, seq_s)
    ckvs, kpes, ssms = [], [], []
    for i in range(depth):
        kind, j = i % N_MIXERS, i // N_MIXERS
        if kind != 0 and isinstance(x, tuple):
            x = jnp.concatenate(x, axis=0)
        y = w_o = None
        if kind == 0:
            x, ckv, kpe = _mla_layer(x, mod4, P, i, j, cache_ckv[:, j], cache_kpe[:, j], npr, seq_p, seq_s)
            ckvs.append(ckv)
            kpes.append(kpe)
        elif kind == 1:
            x = _conv_layer(x, mod4, P, i, j, npr, seq_p, seq_s)
        else:
            y, st = _ssd_layer(x, mod4, P, i, j, state_ssm[:, j], npr, seq_p, seq_s)
            w_o = ssd_w_out[j].astype(BF16)
            ssms.append(st)
        x = _ffn(x, mod4, g_norm2, ffn_w_in, ffn_w_out, i, ffn_cidx, ffn_tm, 256, npr,
                 y=y, w_o=w_o, split=(i == depth - 1))
    return (x[0].reshape(nbp, seq_p, d), x[1].reshape(nbs, seq_s, d),
            jnp.stack(ckvs, axis=1), jnp.stack(kpes, axis=1), jnp.stack(ssms, axis=1))
```

```python
import functools

import jax
import jax.numpy as jnp
import numpy as np
from jax import lax
from jax.experimental import pallas as pl
from jax.experimental.pallas import tpu as pltpu

F32 = jnp.float32
BF16 = jnp.bfloat16

EPS = 1e-6
ROPE_BASE = 10000.0
GRID_W = 64
N_MIXERS = 3

LANES = 128
VMEM_LIMIT_BYTES = 56 * 1024 * 1024

MLA_HEADS = 16
QK_NOPE = 64
QK_ROPE = 32
QK_DIM = QK_NOPE + QK_ROPE
V_HEAD = 64
HEAD_PAD = LANES
CONV_WIDTH = 31
SSD_HEADDIM = 64
SSD_GROUPS = 4
SSD_STATE = 128
SSD_CONV = 5
CHUNK = 128


def _cparams(sem):
    return pltpu.CompilerParams(dimension_semantics=sem, vmem_limit_bytes=VMEM_LIMIT_BYTES)


def _silu(x):
    return x * jax.nn.sigmoid(x)


def _normmod(x, g, sh, sc):
    ms = jnp.mean(x * x, axis=-1, keepdims=True)
    return (x * lax.rsqrt(ms + EPS)) * (g * (1.0 + sc)) + sh


def _cond_index(tm, n_prompt_rows, sample_seq):
    def f(i):
        r = i * tm
        return jnp.where(r < n_prompt_rows, 0, 1 + (r - n_prompt_rows) // sample_seq)
    return f


def _mod_spec(layer, cidx, d6, ngrid):
    if ngrid == 1:
        return pl.BlockSpec((1, 1, 1, d6), lambda i: (layer, cidx(i), 0, 0))
    return pl.BlockSpec((1, 1, 1, d6), lambda i, j: (layer, cidx(i), 0, 0))


def _ada_kernel(c_ref, w_ref, b_ref, o_ref):
    s = _silu(c_ref[...]).astype(BF16)
    o_ref[0] = jnp.dot(s, w_ref[0].astype(BF16), preferred_element_type=F32) + b_ref[0]


def _ada(cond8, w_ada, b_ada):
    depth, d, n = w_ada.shape
    tn = n // 4
    return pl.pallas_call(
        _ada_kernel,
        grid=(depth, n // tn),
        in_specs=[pl.BlockSpec((8, d), lambda l, j: (0, 0)),
                  pl.BlockSpec((1, d, tn), lambda l, j: (l, 0, j)),
                  pl.BlockSpec((1, 1, tn), lambda l, j: (l, 0, j))],
        out_specs=pl.BlockSpec((1, 8, tn), lambda l, j: (l, 0, j)),
        out_shape=jax.ShapeDtypeStruct((depth, 8, n), F32),
        compiler_params=_cparams(("arbitrary", "arbitrary")),
        name="ada",
    )(cond8, w_ada, b_ada.reshape(depth, 1, n))


FFN_SLOTS = 3


def _ffn_kernel(*refs, d, n_in_a, n_out_a, fused, layer, hid, th, ntiles):
    refs = list(refs)
    xa_ref = refs.pop(0)
    xb_ref = refs.pop(0) if n_in_a is not None else None
    y_ref, wo1_ref = (refs.pop(0), refs.pop(0)) if fused else (None, None)
    mod_ref, g_ref, win_hbm, wout_hbm = refs[:4]
    if n_out_a is None:
        o_ref, h_sc, wa_buf, wu_buf, wo_buf, sem = refs[4:]
        acc = o_ref
    else:
        oa_ref, ob_ref, h_sc, acc, wa_buf, wu_buf, wo_buf, sem = refs[4:]
    i = pl.program_id(0)
    nj = hid // th
    base = lax.rem(i * (nj % FFN_SLOTS), FFN_SLOTS)

    def slot_of(j):
        return lax.rem(base + j % FFN_SLOTS, FFN_SLOTS)

    def chunk_copies(j):
        c, s = j % nj, slot_of(j)
        return (pltpu.make_async_copy(win_hbm.at[layer, :, pl.ds(c * th, th)], wa_buf.at[s], sem.at[0, s]),
                pltpu.make_async_copy(win_hbm.at[layer, :, pl.ds(hid + c * th, th)], wu_buf.at[s], sem.at[1, s]),
                pltpu.make_async_copy(wout_hbm.at[layer, pl.ds(c * th, th), :], wo_buf.at[s], sem.at[2, s]))

    def start(j):
        for cp in chunk_copies(j):
            cp.start()

    def wait(j):
        for cp in chunk_copies(j):
            cp.wait()

    @pl.when(i == 0)
    def _():
        start(0)
        start(1)

    wait(0)
    wa0, wu0 = wa_buf[base].astype(BF16), wu_buf[base].astype(BF16)
    tm = acc.shape[0]
    rc = min(tm, 256)
    a_rows, u_rows = [], []
    for r0 in range(0, tm, rc):
        rows = slice(r0, r0 + rc)
        x = xa_ref[rows, :] if xb_ref is None else jnp.where(i < n_in_a, xa_ref[rows, :], xb_ref[rows, :])
        if fused:
            x = x + mod_ref[0, 0, :, 2 * d:3 * d] * jnp.dot(y_ref[rows, :], wo1_ref[...],
                                                             preferred_element_type=F32)
        acc[rows, :] = x
        hr = _normmod(x, g_ref[0], mod_ref[0, 0, :, 3 * d:4 * d], mod_ref[0, 0, :, 4 * d:5 * d]).astype(BF16)
        h_sc[rows, :] = hr
        a_rows.append(jnp.dot(hr, wa0, preferred_element_type=F32))
        u_rows.append(jnp.dot(hr, wu0, preferred_element_type=F32))
    au = (jnp.concatenate(a_rows, axis=0), jnp.concatenate(u_rows, axis=0))

    gate2 = mod_ref[0, 0, :, 5 * d:6 * d]
    h = h_sc[...]

    def up(j):
        s = slot_of(j)
        return (jnp.dot(h, wa_buf[s].astype(BF16), preferred_element_type=F32),
                jnp.dot(h, wu_buf[s].astype(BF16), preferred_element_type=F32))

    for j in range(nj):
        if j + 1 < nj:
            wait(j + 1)
        if j + 2 < nj:
            start(j + 2)
        else:
            @pl.when(i + 1 < ntiles)
            def _(j=j):
                start(j + 2)
        a, u = au
        t = (_silu(a) * u).astype(BF16)
        if j + 1 < nj:
            au = up(j + 1)
        acc[...] += gate2 * jnp.dot(t, wo_buf[slot_of(j)].astype(BF16), preferred_element_type=F32)

    if n_out_a is not None:
        @pl.when(i < n_out_a)
        def _():
            oa_ref[...] = acc[...]

        @pl.when(i >= n_out_a)
        def _():
            ob_ref[...] = acc[...]


def _ffn(x, mod4, g2, w_in, w_out, layer, cidx, tm, th, npr, y=None, w_o=None, split=False):
    pair = isinstance(x, tuple)
    d = x[0].shape[1] if pair else x.shape[1]
    m = (x[0].shape[0] + x[1].shape[0]) if pair else x.shape[0]
    hid = w_out.shape[1]
    nj = hid // th
    n_a = npr // tm
    fused = y is not None
    if pair:
        in_specs = [pl.BlockSpec((tm, d), lambda i: (jnp.minimum(i, n_a - 1), 0)),
                    pl.BlockSpec((tm, d), lambda i: (jnp.maximum(i - n_a, 0), 0))]
        args = list(x)
    else:
        in_specs = [pl.BlockSpec((tm, d), lambda i: (i, 0))]
        args = [x]
    if fused:
        k = y.shape[1]
        in_specs += [pl.BlockSpec((tm, k), lambda i: (i, 0)), pl.BlockSpec((k, d), lambda i: (0, 0))]
        args += [y, w_o]
    in_specs += [_mod_spec(layer, cidx, 6 * d, 1),
                 pl.BlockSpec((1, 1, d), lambda i: (layer, 0, 0)),
                 pl.BlockSpec(memory_space=pl.ANY),
                 pl.BlockSpec(memory_space=pl.ANY)]
    args += [mod4, g2.reshape(g2.shape[0], 1, d), w_in, w_out]
    scratch = [pltpu.VMEM((tm, d), BF16)]
    if split:
        out_specs = [pl.BlockSpec((tm, d), lambda i: (jnp.minimum(i, n_a - 1), 0)),
                     pl.BlockSpec((tm, d), lambda i: (jnp.maximum(i - n_a, 0), 0))]
        out_shape = [jax.ShapeDtypeStruct((npr, d), F32), jax.ShapeDtypeStruct((m - npr, d), F32)]
        scratch.append(pltpu.VMEM((tm, d), F32))
    else:
        out_specs = pl.BlockSpec((tm, d), lambda i: (i, 0))
        out_shape = jax.ShapeDtypeStruct((m, d), F32)
    scratch += [pltpu.VMEM((FFN_SLOTS, d, th), F32), pltpu.VMEM((FFN_SLOTS, d, th), F32),
                pltpu.VMEM((FFN_SLOTS, th, d), F32), pltpu.SemaphoreType.DMA((3, FFN_SLOTS))]
    assert nj >= 2 and hid % th == 0
    return pl.pallas_call(
        functools.partial(_ffn_kernel, d=d, n_in_a=n_a if pair else None, n_out_a=n_a if split else None,
                          fused=fused, layer=layer, hid=hid, th=th, ntiles=m // tm),
        grid=(m // tm,),
        in_specs=in_specs,
        out_specs=out_specs,
        out_shape=out_shape,
        scratch_shapes=scratch,
        compiler_params=_cparams(("arbitrary",)),
        name="ffn_proj" if fused else "ffn",
    )(*args)


def _conv_a_kernel(x_ref, mod_ref, g_ref, w_ref, b_ref, o_ref, w_sc, *, d):
    @pl.when(pl.program_id(0) == 0)
    def _():
        w_sc[...] = w_ref[0].astype(BF16)

    h = _normmod(x_ref[...], g_ref[0], mod_ref[0, 0, :, 0:d], mod_ref[0, 0, :, d:2 * d]).astype(BF16)
    y = jnp.dot(h, w_sc[...], preferred_element_type=F32) + b_ref[0]
    u = y[:, :d] * jax.nn.sigmoid(y[:, d:])
    for cb in range(d // LANES):
        o_ref[cb] = u[:, cb * LANES:(cb + 1) * LANES]


def _conv_b_kernel(*refs, d, t, rc, top):
    (x_ref, u_ref, mod_ref, wdw_ref, bdw_ref, gln_ref, bln_ref, w2_ref, b2_ref) = refs[:9]
    o_ref, upad, conv_sc, y_sc, w_sc = refs[-5:]
    nb = d // LANES
    half = (CONV_WIDTH - 1) // 2

    @pl.when(pl.program_id(0) == 0)
    def _():
        w_sc[...] = w2_ref[0].astype(BF16)
        upad[:, 0:top, :] = jnp.zeros((nb, top, LANES), F32)
        upad[:, top + t:2 * top + t, :] = jnp.zeros((nb, top, LANES), F32)

    upad[:, top:top + t, :] = u_ref[...]

    def conv_block(cb, carry):
        for r0 in range(0, t, rc):
            acc = jnp.broadcast_to(bdw_ref[cb], (rc, LANES))
            for k in range(CONV_WIDTH):
                acc = acc + wdw_ref[cb, pl.ds(k, 1), :] * upad[cb, pl.ds(r0 + top - half + k, rc), :]
            conv_sc[cb, pl.ds(r0, rc), :] = acc
        return carry

    lax.fori_loop(0, nb, conv_block, 0)

    rl = min(t, 256)
    for r0 in range(0, t, rl):
        s = jnp.zeros((rl, 1), F32)
        for cb in range(nb):
            s = s + jnp.sum(conv_sc[cb, r0:r0 + rl, :], axis=-1, keepdims=True)
        mean = s * (1.0 / d)
        v = jnp.zeros((rl, 1), F32)
        for cb in range(nb):
            c = conv_sc[cb, r0:r0 + rl, :] - mean
            v = v + jnp.sum(c * c, axis=-1, keepdims=True)
        rstd = lax.rsqrt(v * (1.0 / d) + EPS)
        for cb in range(nb):
            sl = slice(cb * LANES, (cb + 1) * LANES)
            y = (conv_sc[cb, r0:r0 + rl, :] - mean) * rstd * gln_ref[:, sl] + bln_ref[:, sl]
            y_sc[r0:r0 + rl, sl] = _silu(y).astype(BF16)

    gate = mod_ref[0, 0, :, 2 * d:3 * d]
    r = jnp.dot(y_sc[...], w_sc[...], preferred_element_type=F32) + b2_ref[0]
    o_ref[...] = x_ref[...] + gate * r


def _conv_b(x, u, mod4, wdw, bdw, gln, bln, w2, b2, layer, j, t, row0, nseq, cond0, cond_step, obuf):
    m, d = x.shape
    nb = d // LANES
    top = 16
    blk0 = row0 // t
    in_specs = [pl.BlockSpec((t, d), lambda i: (blk0 + i, 0)),
                pl.BlockSpec((nb, t, LANES), lambda i: (0, blk0 + i, 0)),
                pl.BlockSpec((1, 1, 1, 6 * d), lambda i: (layer, cond0 + cond_step * i, 0, 0)),
                pl.BlockSpec((nb, 32, LANES), lambda i: (0, 0, 0)),
                pl.BlockSpec((nb, 1, LANES), lambda i: (0, 0, 0)),
                pl.BlockSpec((1, d), lambda i: (0, 0)),
                pl.BlockSpec((1, d), lambda i: (0, 0)),
                pl.BlockSpec((1, d, d), lambda i: (j, 0, 0)),
                pl.BlockSpec((1, 1, d), lambda i: (j, 0, 0))]
    args = [x, u, mod4, wdw, bdw, gln, bln, w2, b2]
    aliases = {}
    if obuf is not None:
        in_specs.append(pl.BlockSpec(memory_space=pl.ANY))
        args.append(obuf)
        aliases = {len(args) - 1: 0}
    return pl.pallas_call(
        functools.partial(_conv_b_kernel, d=d, t=t, rc=64, top=top),
        grid=(nseq,),
        in_specs=in_specs,
        out_specs=pl.BlockSpec((t, d), lambda i: (blk0 + i, 0)),
        out_shape=jax.ShapeDtypeStruct((m, d), F32),
        input_output_aliases=aliases,
        scratch_shapes=[pltpu.VMEM((nb, t + 2 * top, LANES), F32),
                        pltpu.VMEM((nb, t, LANES), F32),
                        pltpu.VMEM((t, d), BF16),
                        pltpu.VMEM((d, d), BF16)],
        compiler_params=_cparams(("arbitrary",)),
        name="conv_b",
    )(*args)


def _conv_layer(x, mod4, P, layer, j, npr, seq_p, seq_s):
    m, d = x.shape
    nb = d // LANES
    tm = min(512, seq_s)
    cidx = _cond_index(tm, npr, seq_s)
    u = pl.pallas_call(
        functools.partial(_conv_a_kernel, d=d),
        grid=(m // tm,),
        in_specs=[pl.BlockSpec((tm, d), lambda i: (i, 0)),
                  _mod_spec(layer, cidx, 6 * d, 1),
                  pl.BlockSpec((1, 1, d), lambda i: (layer, 0, 0)),
                  pl.BlockSpec((1, d, 2 * d), lambda i: (j, 0, 0)),
                  pl.BlockSpec((1, 1, 2 * d), lambda i: (j, 0, 0))],
        out_specs=pl.BlockSpec((nb, tm, LANES), lambda i: (0, i, 0)),
        out_shape=jax.ShapeDtypeStruct((nb, m, LANES), F32),
        scratch_shapes=[pltpu.VMEM((d, 2 * d), BF16)],
        compiler_params=_cparams(("arbitrary",)),
        name="conv_a",
    )(x, mod4, P["g_norm1"].reshape(-1, 1, d), P["cv_w_pw1"], P["cv_b_pw1"].reshape(-1, 1, 2 * d))
    wdw = jnp.pad(P["cv_w_dw"][j], ((0, 32 - CONV_WIDTH), (0, 0))).reshape(32, nb, LANES).transpose(1, 0, 2)
    bdw = P["cv_b_dw"][j].reshape(nb, 1, LANES)
    gln = P["cv_g_ln"][j].reshape(1, d)
    bln = P["cv_b_ln"][j].reshape(1, d)
    b2 = P["cv_b_pw2"].reshape(-1, 1, d)
    args = (x, u, mod4, wdw, bdw, gln, bln, P["cv_w_pw2"], b2, layer, j)
    op = _conv_b(*args, seq_p, 0, npr // seq_p, 0, 0, None)
    return _conv_b(*args, seq_s, npr, (m - npr) // seq_s, 1, 1, op)


_LOG2E = 1.4426950408889634
_ROT = QK_ROPE // 4


def _swap_index():
    idx = list(range(HEAD_PAD))
    for base in (QK_NOPE, QK_NOPE + 2 * _ROT):
        for l in range(_ROT):
            idx[base + l], idx[base + _ROT + l] = base + _ROT + l, base + l
    return jnp.asarray(idx, jnp.int32)


def _swap_cols(w):
    lane = jnp.arange(HEAD_PAD)
    rot = (lane >= QK_NOPE) & (lane < QK_DIM)
    return jnp.where(rot, jnp.take(w, _swap_index(), axis=-1), 0.0)


def _rope_tables(seq, gqn, gkn):
    rows = seq // GRID_W
    row = jnp.repeat(jnp.arange(rows, dtype=F32), GRID_W)
    col = jnp.tile(jnp.arange(GRID_W, dtype=F32), rows)
    axis = QK_ROPE // 2
    inv = ROPE_BASE ** (-jnp.arange(0, axis, 2, dtype=F32) / axis)
    ar, ac = row[:, None] * inv, col[:, None] * inv
    cr, sr, cc, sc = jnp.cos(ar), jnp.sin(ar), jnp.cos(ac), jnp.sin(ac)
    one = jnp.ones((seq, QK_NOPE), F32)
    zn = jnp.zeros((seq, QK_NOPE), F32)
    zp = jnp.zeros((seq, HEAD_PAD - QK_DIM), F32)
    c = jnp.concatenate([one, cr, cr, cc, cc, zp], axis=1)
    s = jnp.concatenate([zn, -sr, sr, -sc, sc, zp], axis=1)
    idx = _swap_index()
    return tuple(t for g in (gqn, gkn) for t in (c[None] * g, s[None] * jnp.take(g, idx, axis=-1)))


def _head_rstd(xh):
    ms = jnp.sum(xh * xh, axis=-1, keepdims=True) * (1.0 / QK_DIM)
    return lax.rsqrt(ms + EPS)


def _expand_kv(ckv_b, kpe, wuk_ref, wv_ref, gk, rot, k_ref, v_ref, v_transposed):
    kk = jnp.dot(ckv_b, wuk_ref[0], preferred_element_type=F32)
    if v_transposed:
        v = lax.dot_general(wv_ref[0], ckv_b, _NT, preferred_element_type=F32)
    else:
        v = jnp.dot(ckv_b, wv_ref[0], preferred_element_type=F32)
    v_ref[...] = v.astype(BF16)
    for h in range(MLA_HEADS):
        sl = slice(h * HEAD_PAD, (h + 1) * HEAD_PAD)
        kh = kk[:, sl] + kpe
        r = _head_rstd(kh)
        kh = (kh * gk) * r if rot is None else (kh * rot[0] + rot[1]) * r
        k_ref[:, sl] = kh.astype(BF16)


def _mla_proj_kernel(*refs, d, rope):
    (x_ref, mod_ref, g_ref, wcat_ref, gq_ref, wuq_ref, gkv_ref, wuk_ref, wv_ref, gqn_ref, gkn_ref) = refs[:11]
    if rope:
        cq_ref, sq_ref, ck_ref, sk_ref = refs[11:15]
        q_ref, k_ref, v_ref = refs[15:18]
    else:
        q_ref, k_ref, v_ref, ckv_ref, kpe_ref = refs[11:16]
    h = _normmod(x_ref[...], g_ref[0], mod_ref[0, 0, :, 0:d], mod_ref[0, 0, :, d:2 * d]).astype(BF16)
    t = jnp.dot(h, wcat_ref[0], preferred_element_type=F32)
    nq = gq_ref.shape[2]
    nkv = gkv_ref.shape[2]
    hq = MLA_HEADS * HEAD_PAD
    qd, craw, kpe = t[:, :nq], t[:, nq:nq + nkv], t[:, nq + nkv:nq + nkv + HEAD_PAD]
    cq = (qd * lax.rsqrt(jnp.mean(qd * qd, axis=-1, keepdims=True) + EPS) * gq_ref[0]).astype(BF16)
    ckv = craw * lax.rsqrt(jnp.mean(craw * craw, axis=-1, keepdims=True) + EPS) * gkv_ref[0]
    if not rope:
        ckv_ref[...] = ckv
        kpe_ref[...] = kpe
    q = jnp.dot(cq, wuq_ref[0], preferred_element_type=F32)
    if rope:
        ct, st = cq_ref[0], sq_ref[0]
    else:
        gqn = gqn_ref[0]
    for hd in range(MLA_HEADS):
        sl = slice(hd * HEAD_PAD, (hd + 1) * HEAD_PAD)
        qh = q[:, sl]
        r = _head_rstd(qh)
        qh = (qh * ct + q[:, hq + hd * HEAD_PAD:hq + (hd + 1) * HEAD_PAD] * st) * r if rope else (qh * gqn) * r
        q_ref[:, sl] = qh.astype(BF16)
    rot = None
    if rope:
        kpe2 = t[:, nq + nkv + HEAD_PAD:]
        rot = (ck_ref[0], kpe2 * sk_ref[0])
    _expand_kv(ckv.astype(BF16), kpe, wuk_ref, wv_ref, gkn_ref[0], rot, k_ref, v_ref, rope)


def _kv_ctx_kernel(ckv_ref, kpe_ref, wuk_ref, wv_ref, gkn_ref, k_ref, v_ref):
    _expand_kv(ckv_ref[...].astype(BF16), kpe_ref[...], wuk_ref, wv_ref, gkn_ref[0], None, k_ref, v_ref, True)


_NT = (((1,), (1,)), ((), ()))


def _attn_prompt_kernel(q_ref, k_ref, v_ref, x_ref, mod_ref, wo_ref, o_ref, a_sc, *, d):
    t = q_ref.shape[0]
    lane = lax.broadcasted_iota(jnp.int32, (t, LANES), 1)
    for hp in range(MLA_HEADS // 2):
        vp = v_ref[:, hp * LANES:(hp + 1) * LANES]
        outs = []
        for hh in range(2):
            sl = slice((2 * hp + hh) * HEAD_PAD, (2 * hp + hh + 1) * HEAD_PAD)
            s = lax.dot_general(q_ref[:, sl], k_ref[:, sl], _NT, preferred_element_type=F32)
            e = jnp.exp2(s - s.max(axis=-1, keepdims=True))
            l = jnp.sum(e, axis=-1, keepdims=True)
            outs.append(jnp.dot(e.astype(BF16), vp, preferred_element_type=F32) * (1.0 / l))
        a_sc[:, hp * LANES:(hp + 1) * LANES] = jnp.where(lane < V_HEAD, outs[0], outs[1]).astype(BF16)
    gate = mod_ref[0, 0, :, 2 * d:3 * d]
    o_ref[...] = x_ref[...] + gate * jnp.dot(a_sc[...], wo_ref[0], preferred_element_type=F32)


def _attn_sample_kernel(q_ref, kc_ref, vtc_ref, kl_ref, vtl_ref, x_ref, mod_ref, wo_ref, obuf_ref, o_ref, *, d):
    del obuf_ref
    nh = q_ref.shape[1] // HEAD_PAD
    row = lax.broadcasted_iota(jnp.int32, (LANES, 1), 0)

    def scores(hh):
        sl = slice(hh * HEAD_PAD, (hh + 1) * HEAD_PAD)
        qh = q_ref[:, sl]
        return [lax.dot_general(k_ref[:, sl], qh, _NT, preferred_element_type=F32)
                for k_ref in (kc_ref, kl_ref)]

    halves = []
    nxt = scores(0)
    for hh in range(nh):
        sts = nxt
        if hh + 1 < nh:
            nxt = scores(hh + 1)
        m = jnp.max(sts[0], axis=0, keepdims=True)
        for st in sts[1:]:
            m = jnp.maximum(m, jnp.max(st, axis=0, keepdims=True))
        pr = slice((hh // 2) * LANES, (hh // 2 + 1) * LANES)
        own = (row < V_HEAD) if hh % 2 == 0 else (row >= V_HEAD)
        acc = None
        for st, vt_ref in zip(sts, (vtc_ref, vtl_ref)):
            e = jnp.exp2(st - m).astype(BF16)
            vt = vt_ref[pr, :]
            lhs = jnp.where(own, vt, jnp.ones_like(vt))
            pv = jnp.dot(lhs, e, preferred_element_type=F32)
            acc = pv if acc is None else acc + pv
        if hh % 2 == 0:
            halves.append(acc[0:V_HEAD] * (1.0 / acc[V_HEAD:V_HEAD + 1]))
        else:
            halves.append(acc[V_HEAD:2 * V_HEAD] * (1.0 / acc[0:1]))
    attn = jnp.concatenate(halves, axis=0).T.astype(BF16)
    gate = mod_ref[0, 0, :, 2 * d:3 * d]
    o_ref[...] = x_ref[...] + gate * jnp.dot(attn, wo_ref[0], preferred_element_type=F32)


def _mla_weights(P, seq_s):
    n = P["mla_w_dq"].shape[0]
    nq = P["mla_w_dq"].shape[2]
    nkv = P["mla_g_kv"].shape[1]
    pad = HEAD_PAD - QK_DIM
    w_dkv = P["mla_w_dkv"]
    w_kpe = jnp.pad(w_dkv[:, :, nkv:], ((0, 0), (0, 0), (QK_NOPE, pad)))
    wcat = jnp.concatenate([P["mla_w_dq"], w_dkv[:, :, :nkv], w_kpe], axis=2).astype(BF16)
    wuq = jnp.pad(P["mla_w_uq"].reshape(n, nq, MLA_HEADS, QK_DIM), ((0, 0), (0, 0), (0, 0), (0, pad)))
    wuq = wuq.reshape(n, nq, MLA_HEADS * HEAD_PAD).astype(BF16)
    wukv = P["mla_w_ukv"].reshape(n, nkv, MLA_HEADS, QK_NOPE + V_HEAD)
    wuk = jnp.pad(wukv[..., :QK_NOPE], ((0, 0), (0, 0), (0, 0), (0, HEAD_PAD - QK_NOPE)))
    wuk = wuk.reshape(n, nkv, MLA_HEADS * HEAD_PAD).astype(BF16)
    wv = wukv[..., QK_NOPE:].reshape(n, nkv, MLA_HEADS * V_HEAD).astype(BF16)
    gq = P["mla_g_q"].reshape(n, 1, nq)
    gkv = P["mla_g_kv"].reshape(n, 1, nkv)
    gqn = (jnp.pad(P["mla_g_qn"], ((0, 0), (0, pad))) * (QK_DIM ** -0.5 * _LOG2E)).reshape(n, 1, HEAD_PAD)
    gkn = jnp.pad(P["mla_g_kn"], ((0, 0), (0, pad))).reshape(n, 1, HEAD_PAD)
    wuq2 = _swap_cols(wuq.reshape(n, nq, MLA_HEADS, HEAD_PAD)).reshape(n, nq, MLA_HEADS * HEAD_PAD)
    wuq_r = jnp.concatenate([wuq, wuq2], axis=2)
    wcat_r = jnp.concatenate([wcat, _swap_cols(w_kpe).astype(BF16)], axis=2)
    return dict(plain=(wcat, gq, wuq, gkv, wuk, wv, gqn, gkn), rope=(wcat_r, wuq_r, jnp.swapaxes(wv, 1, 2)),
                wo=P["mla_w_o"].astype(BF16), tabs=_rope_tables(seq_s, gqn, gkn))


def _layer_spec(a, j):
    return pl.BlockSpec((1,) + a.shape[1:], lambda *_: (j,) + (0,) * (a.ndim - 1))


def _mla_proj(x, xblk0, mod4, g1, W, layer, j, row0, nrows, cidx, tm, rope_args):
    d = x.shape[1]
    wcat, gq, wuq, gkv, wuk, wv, gqn, gkn = W
    blk0 = row0 // tm
    hq = MLA_HEADS * HEAD_PAD
    hv = MLA_HEADS * V_HEAD
    nkv = gkv.shape[2]
    rope = rope_args is not None
    args = [x, mod4, g1, wcat, gq, wuq, gkv, wuk, wv, gqn, gkn]
    if rope:
        wcat_r, wuq_r, wvt, tabs = rope_args
        args[3], args[5], args[8] = wcat_r, wuq_r, wvt
    in_specs = [pl.BlockSpec((tm, d), lambda i: (xblk0 + i, 0)),
                pl.BlockSpec((1, 1, 1, 6 * d), lambda i: (layer, cidx(blk0 + i), 0, 0)),
                pl.BlockSpec((1, 1, d), lambda i: (layer, 0, 0))] + [_layer_spec(a, j) for a in args[3:]]
    out_specs = [pl.BlockSpec((tm, hq), lambda i: (i, 0)),
                 pl.BlockSpec((tm, hq), lambda i: (i, 0))]
    out_shape = [jax.ShapeDtypeStruct((nrows, hq), BF16),
                 jax.ShapeDtypeStruct((nrows, hq), BF16)]
    if rope:
        per = tabs[0].shape[1] // tm
        in_specs += [pl.BlockSpec((1, tm, HEAD_PAD), lambda i: (j, i % per, 0))] * 4
        args += list(tabs)
        out_specs += [pl.BlockSpec((hv, tm), lambda i: (0, i))]
        out_shape += [jax.ShapeDtypeStruct((hv, nrows), BF16)]
    else:
        out_specs += [pl.BlockSpec((tm, hv), lambda i: (i, 0)),
                      pl.BlockSpec((tm, nkv), lambda i: (i, 0)), pl.BlockSpec((tm, HEAD_PAD), lambda i: (i, 0))]
        out_shape += [jax.ShapeDtypeStruct((nrows, hv), BF16),
                      jax.ShapeDtypeStruct((nrows, nkv), F32), jax.ShapeDtypeStruct((nrows, HEAD_PAD), F32)]
    return pl.pallas_call(
        functools.partial(_mla_proj_kernel, d=d, rope=rope),
        grid=(nrows // tm,),
        in_specs=in_specs, out_specs=out_specs, out_shape=out_shape,
        compiler_params=_cparams(("arbitrary",)),
        name="mla_proj_rope" if rope else "mla_proj",
    )(*args)


def _mla_layer(x, mod4, P, MW, layer, j, cache_ckv, cache_kpe, npr, seq_p, seq_s):
    xa, xb = x if isinstance(x, tuple) else (x, x)
    d = xa.shape[1]
    nrs = xb.shape[0] if isinstance(x, tuple) else xb.shape[0] - npr
    m = npr + nrs
    nbp, nbs = npr // seq_p, nrs // seq_s
    past = cache_ckv.shape[1]
    W, (wcat_r, wuq_r, wvt), wo, tabs = MW["plain"], MW["rope"], MW["wo"], MW["tabs"]
    wuk, gkn = W[4], W[7]
    g1 = P["g_norm1"].reshape(-1, 1, d)
    hq = MLA_HEADS * HEAD_PAD
    hv = MLA_HEADS * V_HEAD
    tm = min(512, seq_s)
    cidx = _cond_index(tm, npr, seq_s)
    xb0 = 0 if isinstance(x, tuple) else npr // tm
    qp, kp, vp, ckv, kpe128 = _mla_proj(xa, 0, mod4, g1, W, layer, j, 0, npr, cidx, tm, None)
    qs, ks, vts = _mla_proj(xb, xb0, mod4, g1, W, layer, j, npr, nrs, cidx, tm, (wcat_r, wuq_r, wvt, tabs))
    cc = cache_ckv.reshape(nbs * past, -1)
    ck = jnp.pad(cache_kpe.reshape(nbs * past, -1), ((0, 0), (QK_NOPE, HEAD_PAD - QK_DIM)))
    kc, vtc = pl.pallas_call(
        _kv_ctx_kernel,
        grid=(nbs,),
        in_specs=[pl.BlockSpec((past, cc.shape[1]), lambda i: (i, 0)),
                  pl.BlockSpec((past, HEAD_PAD), lambda i: (i, 0)),
                  _layer_spec(wuk, j), _layer_spec(wvt, j), _layer_spec(gkn, j)],
        out_specs=[pl.BlockSpec((past, hq), lambda i: (i, 0)), pl.BlockSpec((hv, past), lambda i: (0, i))],
        out_shape=[jax.ShapeDtypeStruct((nbs * past, hq), BF16), jax.ShapeDtypeStruct((hv, nbs * past), BF16)],
        compiler_params=_cparams(("arbitrary",)),
        name="mla_kv_ctx",
    )(cc, ck, wuk, wvt, gkn)
    xp0 = 0
    xs0 = 0 if isinstance(x, tuple) else npr
    xn_p = pl.pallas_call(
        functools.partial(_attn_prompt_kernel, d=d),
        grid=(nbp,),
        in_specs=[pl.BlockSpec((seq_p, hq), lambda b: (b, 0)),
                  pl.BlockSpec((seq_p, hq), lambda b: (b, 0)),
                  pl.BlockSpec((seq_p, hv), lambda b: (b, 0)),
                  pl.BlockSpec((seq_p, d), lambda b: (xp0 // seq_p + b, 0)),
                  pl.BlockSpec((1, 1, 1, 6 * d), lambda b: (layer, 0, 0, 0)),
                  _layer_spec(wo, j)],
        out_specs=pl.BlockSpec((seq_p, d), lambda b: (b, 0)),
        out_shape=jax.ShapeDtypeStruct((m, d), F32),
        scratch_shapes=[pltpu.VMEM((seq_p, hv), BF16)],
        compiler_params=_cparams(("arbitrary",)),
        name="attn_prompt",
    )(qp, kp, vp, xa, mod4, wo)
    tq = min(seq_s, 512)
    nqt = seq_s // tq
    o0 = npr // tq
    xn = pl.pallas_call(
        functools.partial(_attn_sample_kernel, d=d),
        grid=(nbs, nqt),
        in_specs=[pl.BlockSpec((tq, hq), lambda b, qi: (b * nqt + qi, 0)),
                  pl.BlockSpec((past, hq), lambda b, qi: (b, 0)),
                  pl.BlockSpec((hv, past), lambda b, qi: (0, b)),
                  pl.BlockSpec((seq_s, hq), lambda b, qi: (b, 0)),
                  pl.BlockSpec((hv, seq_s), lambda b, qi: (0, b)),
                  pl.BlockSpec((tq, d), lambda b, qi: (xs0 // tq + b * nqt + qi, 0)),
                  pl.BlockSpec((1, 1, 1, 6 * d), lambda b, qi: (layer, 1 + b, 0, 0)),
                  _layer_spec(wo, j),
                  pl.BlockSpec(memory_space=pl.ANY)],
        out_specs=pl.BlockSpec((tq, d), lambda b, qi: (o0 + b * nqt + qi, 0)),
        out_shape=jax.ShapeDtypeStruct((m, d), F32),
        input_output_aliases={8: 0},
        compiler_params=_cparams(("arbitrary", "arbitrary")),
        name="attn_sample",
    )(qs, kc, vtc, ks, vts, xb, mod4, wo, xn_p)
    new_ckv = ckv.reshape(nbp, seq_p, -1)
    new_kpe = kpe128[:, QK_NOPE:QK_DIM].reshape(nbp, seq_p, QK_ROPE)
    return xn, new_ckv, new_kpe


SSD_R = 8
SSD_GW = SSD_R * SSD_HEADDIM
SSD_NX = SSD_GW // LANES
SSD_DTW = 2 * SSD_R
SSD_TOP = 8


def _ssd_in_kernel(x_ref, mod_ref, g_ref, w_ref, wdt_ref, zx_ref, dt_ref, h_sc, *, d, heads):
    @pl.when(pl.program_id(1) == 0)
    def _():
        h = _normmod(x_ref[...], g_ref[0], mod_ref[0, 0, :, 0:d], mod_ref[0, 0, :, d:2 * d]).astype(BF16)
        h_sc[...] = h
        row = lax.broadcasted_iota(jnp.int32, wdt_ref.shape[1:], 0)
        wdt = jnp.where(row < 2 * heads, wdt_ref[0], 0.0).astype(BF16)
        dtt = lax.dot_general(wdt, h, _NT, preferred_element_type=F32)
        for g in range(SSD_GROUPS):
            dt_ref[g, 0:SSD_R, :] = dtt[g * SSD_R:(g + 1) * SSD_R, :]
            dt_ref[g, SSD_R:SSD_DTW, :] = dtt[heads + g * SSD_R:heads + (g + 1) * SSD_R, :]

    r = lax.dot_general(h_sc[...], w_ref[0].astype(BF16), _NT, preferred_element_type=F32)
    for c in range(r.shape[1] // LANES):
        zx_ref[c] = r[:, c * LANES:(c + 1) * LANES]


def _split3(x):
    hi = x.astype(BF16)
    r1 = x - hi.astype(F32)
    mid = r1.astype(BF16)
    lo = (r1 - mid.astype(F32)).astype(BF16)
    return hi, mid, lo


def _dot3_right(x, tri):
    return sum(jnp.dot(p, tri, preferred_element_type=F32) for p in _split3(x))


SSD_NPIECE = 7


def _ssd_spread_matrix():
    sel = np.zeros((2, CHUNK, SSD_R * LANES + 2 * SSD_GW), np.float32)
    for dr in range(2):
        for r in range(SSD_R):
            h = dr * SSD_R + r
            for p in range(3):
                sel[dr, p * SSD_DTW + h, r * LANES:(r + 1) * LANES] = 1.0
            for p in range(3, 5):
                sel[dr, p * SSD_DTW + h, SSD_R * LANES + r * SSD_HEADDIM:SSD_R * LANES + (r + 1) * SSD_HEADDIM] = 1.0
            for p in range(5, 7):
                c0 = SSD_R * LANES + SSD_GW
                sel[dr, p * SSD_DTW + h, c0 + r * SSD_HEADDIM:c0 + (r + 1) * SSD_HEADDIM] = 1.0
    return jnp.asarray(sel, BF16)


def _ssd_core_kernel(*refs, t, has_h0):
    (z_ref, x_ref, b_ref, c_ref, dt_ref, wx_ref, wb_ref, wc_ref, bx_ref, bb_ref, bc_ref,
     dtb_ref, alog_ref, dsum_ref, gn_ref, bsel_ref) = refs[:16]
    if has_h0:
        h0_ref, y_ref = refs[16], refs[18]
        st_ref = None
        scr = refs[19:]
    else:
        y_ref, st_ref = refs[16:18]
        h0_ref = None
        scr = refs[18:]
    (pad_sc, xc_sc, bcv_sc, ccv_sc, bt_sc, cb_sc, y_sc, dtt_sc, dat_sc, s_sc) = scr
    nc = t // CHUNK
    half = (SSD_CONV - 1) // 2
    nblk = SSD_NX + 2

    @pl.when((pl.program_id(0) == 0) & (pl.program_id(1) == 0))
    def _():
        pad_sc[:, 0:SSD_TOP, :] = jnp.zeros((nblk, SSD_TOP, LANES), F32)
        pad_sc[:, SSD_TOP + t:2 * SSD_TOP + t, :] = jnp.zeros((nblk, SSD_TOP, LANES), F32)

    pad_sc[0:SSD_NX, SSD_TOP:SSD_TOP + t, :] = x_ref[...]
    pad_sc[SSD_NX, SSD_TOP:SSD_TOP + t, :] = b_ref[0]
    pad_sc[SSD_NX + 1, SSD_TOP:SSD_TOP + t, :] = c_ref[0]

    def conv_chunk(c):
        r0 = c * CHUNK
        for blk in range(nblk):
            if blk < SSD_NX:
                w_r, bias, widx = wx_ref, bx_ref[blk], blk
            elif blk == SSD_NX:
                w_r, bias, widx = wb_ref, bb_ref[0], 0
            else:
                w_r, bias, widx = wc_ref, bc_ref[0], 0
            acc = jnp.broadcast_to(bias, (CHUNK, LANES))
            for k in range(SSD_CONV):
                acc = acc + w_r[widx, k:k + 1, :] * pad_sc[blk, r0 + SSD_TOP - half + k:r0 + SSD_TOP - half + k + CHUNK, :]
            acc = _silu(acc)
            if blk < SSD_NX:
                xc_sc[blk, r0:r0 + CHUNK, :] = acc
            elif blk == SSD_NX:
                bcv_sc[r0:r0 + CHUNK, :] = acc.astype(BF16)
                bt_sc[c] = acc.T.astype(BF16)
            else:
                ccv_sc[r0:r0 + CHUNK, :] = acc.astype(BF16)

    def gate_chunk(c):
        rows = slice(c * CHUNK, (c + 1) * CHUNK)
        yz = [y_sc[q, rows, :] * _silu(z_ref[q, rows, :]) for q in range(SSD_NX)]
        ms = sum(jnp.sum(a * a, axis=-1, keepdims=True) for a in yz) * (1.0 / SSD_GW)
        rstd = lax.rsqrt(ms + EPS)
        for q in range(SSD_NX):
            sl = slice(q * LANES, (q + 1) * LANES)
            y_ref[rows, sl] = (yz[q] * rstd * gn_ref[0, :, sl]).astype(BF16)

    a_row = -jnp.exp(alog_ref[0]) * _LOG2E
    for c in range(nc):
        v = dt_ref[0, :, c * CHUNK:(c + 1) * CHUNK] + dtb_ref[0]
        dt = jnp.maximum(v, 0.0) + jnp.log1p(jnp.exp(-jnp.abs(v)))
        dtt_sc[c] = dt
        dat_sc[c] = dt * a_row

    row = lax.broadcasted_iota(jnp.int32, (CHUNK, CHUNK), 0)
    colm = lax.broadcasted_iota(jnp.int32, (CHUNK, CHUNK), 1)
    lower = colm <= row
    upper = colm >= row
    tri_l = jnp.where(lower, 1.0, 0.0).astype(BF16)
    tri_u = jnp.where(upper, 1.0, 0.0).astype(BF16)
    lane = lax.broadcasted_iota(jnp.int32, (CHUNK, LANES), 1)
    head_row = lax.broadcasted_iota(jnp.int32, (SSD_DTW, 1), 0)
    zero16 = jnp.zeros((SSD_DTW, CHUNK), F32)

    col0, w0, e0 = 0, SSD_R * LANES, SSD_R * LANES + SSD_GW
    zpad = jnp.zeros((CHUNK - SSD_NPIECE * SSD_DTW, CHUNK), F32)

    def prepare(dr, c):
        lo = dr * SSD_R
        tri_t = tri_l if dr else tri_u
        own = (head_row >= lo) & (head_row < lo + SSD_R)
        rows = slice(c * CHUNK, (c + 1) * CHUNK)
        dtt = dtt_sc[c]
        cum_t = _dot3_right(dat_sc[c], tri_t)
        edge = cum_t[:, 0:1] if dr else cum_t[:, CHUNK - 1:CHUNK]
        w2_t = dtt * jnp.exp2(edge - cum_t)
        ecum_t = jnp.exp2(cum_t)
        pieces = (list(_split3(jnp.where(own, cum_t, zero16)))
                  + list(_split3(jnp.where(own, w2_t, zero16))[:2])
                  + list(_split3(jnp.where(own, ecum_t, zero16))[:2]))
        staged = jnp.concatenate([p.astype(F32) for p in pieces] + [zpad], axis=0)
        spread = jnp.dot(staged.T.astype(BF16), bsel_ref[dr], preferred_element_type=F32)
        cmat = ccv_sc[rows, :]
        if dr == 0:
            cb = lax.dot_general(cmat, bcv_sc[rows, :], _NT, preferred_element_type=F32)
            cb_sc[c] = cb
        else:
            cb = cb_sc[c]
        rowterm = cum_t - jnp.log2(dtt)
        return dict(dr=dr, c=c, rows=rows, lo=lo, rowterm=rowterm, spread=spread, cmat=cmat, cb=cb)

    def scan_chunk(pre):
        dr, c, rows, lo = pre["dr"], pre["c"], pre["rows"], pre["lo"]
        rowterm, spread, cmat, cb = pre["rowterm"], pre["spread"], pre["cmat"], pre["cb"]
        mask = upper if dr else lower
        s_prev = s_sc[...]
        yoff = jnp.dot(cmat, s_prev.astype(BF16), preferred_element_type=F32)
        xd = []
        for q in range(SSD_NX):
            xblk = xc_sc[q, rows, :]
            xb16 = xblk.astype(BF16)
            mh = []
            for r in (2 * q, 2 * q + 1):
                seg = spread[:, col0 + r * LANES:col0 + (r + 1) * LANES] - rowterm[lo + r:lo + r + 1, :]
                lm = jnp.exp2(jnp.where(mask, seg, -jnp.inf))
                mh.append((cb * lm).astype(BF16))
            rhs = jnp.concatenate([jnp.where(lane < SSD_HEADDIM, xb16, jnp.zeros_like(xb16)),
                                   jnp.where(lane >= SSD_HEADDIM, xb16, jnp.zeros_like(xb16))], axis=0)
            y = jnp.dot(jnp.concatenate(mh, axis=1), rhs, preferred_element_type=F32)
            sl = slice(q * LANES, (q + 1) * LANES)
            y = y + yoff[:, sl] * spread[:, e0 + q * LANES:e0 + (q + 1) * LANES]
            if dr == 0:
                y_sc[q, rows, :] = y + xblk * dsum_ref[0, :, sl]
            else:
                y_sc[q, rows, :] = y_sc[q, rows, :] + y
            xd.append((xblk * spread[:, w0 + q * LANES:w0 + (q + 1) * LANES]).astype(BF16))
        st = jnp.dot(bt_sc[c], jnp.concatenate(xd, axis=1), preferred_element_type=F32)
        er = 0 if dr else CHUNK - 1
        cd = spread[er:er + 1, e0:e0 + SSD_GW]
        s_sc[...] = s_prev * cd + st

    order = [(0, c) for c in range(nc)] + [(1, c) for c in range(nc - 1, -1, -1)]
    conv_chunk(0)
    pre = prepare(*order[0])
    for k, (dr, c) in enumerate(order):
        if c == (nc - 1 if dr else 0):
            if has_h0:
                s_sc[...] = h0_ref[0, dr, 0].T
            else:
                s_sc[...] = jnp.zeros_like(s_sc)
        if dr == 0 and c + 1 < nc:
            conv_chunk(c + 1)
        nxt = prepare(*order[k + 1]) if k + 1 < len(order) else None
        scan_chunk(pre)
        pre = nxt
        if dr == 1:
            gate_chunk(c)
        if st_ref is not None and c == (0 if dr else nc - 1):
            st_ref[0, dr, 0] = s_sc[...].T


def _ssd_core(zx, dtc, cw, cbias, dtb, alog, dsum, gn, h0, ybuf, t, row0, nseq):
    has_h0 = h0 is not None
    m = zx.shape[1]
    bsel = _ssd_spread_matrix()
    b0 = row0 // t
    nc = t // CHUNK
    nz = SSD_GROUPS * SSD_NX
    in_specs = [pl.BlockSpec((SSD_NX, t, LANES), lambda b, g: (g, b0 + b, 0)),
                pl.BlockSpec((SSD_NX, t, LANES), lambda b, g: (SSD_GROUPS + g, b0 + b, 0)),
                pl.BlockSpec((1, t, LANES), lambda b, g: (2 * nz + g, b0 + b, 0)),
                pl.BlockSpec((1, t, LANES), lambda b, g: (2 * nz + SSD_GROUPS + g, b0 + b, 0)),
                pl.BlockSpec((1, SSD_DTW, t), lambda b, g: (g, 0, b0 + b)),
                pl.BlockSpec((SSD_NX, 8, LANES), lambda b, g: (g, 0, 0)),
                pl.BlockSpec((1, 8, LANES), lambda b, g: (nz + g, 0, 0)),
                pl.BlockSpec((1, 8, LANES), lambda b, g: (nz + SSD_GROUPS + g, 0, 0)),
                pl.BlockSpec((SSD_NX, 1, LANES), lambda b, g: (g, 0, 0)),
                pl.BlockSpec((1, 1, LANES), lambda b, g: (nz + g, 0, 0)),
                pl.BlockSpec((1, 1, LANES), lambda b, g: (nz + SSD_GROUPS + g, 0, 0)),
                pl.BlockSpec((1, SSD_DTW, CHUNK), lambda b, g: (g, 0, 0)),
                pl.BlockSpec((1, SSD_DTW, CHUNK), lambda b, g: (g, 0, 0)),
                pl.BlockSpec((1, 1, SSD_GW), lambda b, g: (g, 0, 0)),
                pl.BlockSpec((1, 1, SSD_GW), lambda b, g: (g, 0, 0)),
                pl.BlockSpec(bsel.shape, lambda b, g: (0, 0, 0))]
    args = [zx, zx, zx, zx, dtc, cw, cw, cw, cbias, cbias, cbias, dtb, alog, dsum, gn, bsel]
    st_spec = pl.BlockSpec((1, 2, 1, SSD_GW, SSD_STATE), lambda b, g: (b, 0, g, 0, 0))
    y_spec = pl.BlockSpec((t, SSD_GW), lambda b, g: (b0 + b, g))
    y_shape = jax.ShapeDtypeStruct((m, SSD_GROUPS * SSD_GW), BF16)
    aliases = {}
    if has_h0:
        in_specs += [st_spec, pl.BlockSpec(memory_space=pl.ANY)]
        args += [h0, ybuf]
        aliases = {len(args) - 1: 0}
        out_specs, out_shape = y_spec, y_shape
    else:
        out_specs = [y_spec, st_spec]
        out_shape = [y_shape, jax.ShapeDtypeStruct((nseq, 2, SSD_GROUPS, SSD_GW, SSD_STATE), F32)]
    scratch = [pltpu.VMEM((SSD_NX + 2, t + 2 * SSD_TOP, LANES), F32),
               pltpu.VMEM((SSD_NX, t, LANES), F32),
               pltpu.VMEM((t, LANES), BF16),
               pltpu.VMEM((t, LANES), BF16),
               pltpu.VMEM((nc, CHUNK, CHUNK), BF16),
               pltpu.VMEM((nc, CHUNK, CHUNK), F32),
               pltpu.VMEM((SSD_NX, t, LANES), F32),
               pltpu.VMEM((nc, SSD_DTW, CHUNK), F32),
               pltpu.VMEM((nc, SSD_DTW, CHUNK), F32),
               pltpu.VMEM((SSD_STATE, SSD_GW), F32)]
    return pl.pallas_call(
        functools.partial(_ssd_core_kernel, t=t, has_h0=has_h0),
        grid=(nseq, SSD_GROUPS),
        in_specs=in_specs, out_specs=out_specs, out_shape=out_shape,
        input_output_aliases=aliases,
        scratch_shapes=scratch,
        compiler_params=_cparams(("arbitrary", "arbitrary")),
        name="ssd_core_h0" if has_h0 else "ssd_core",
    )(*args)


def _ssd_layer(x, mod4, P, layer, j, st0, npr, seq_p, seq_s):
    m, d = x.shape
    nbp, nbs = npr // seq_p, (m - npr) // seq_s
    heads = SSD_GROUPS * SSD_R
    inner = heads * SSD_HEADDIM
    gn_w = SSD_GROUPS * SSD_STATE
    nzx = 2 * inner + 2 * gn_w
    tm, tn = min(1024, seq_s), 1024
    cidx = _cond_index(tm, npr, seq_s)
    w_in_t = jnp.swapaxes(P["ssd_w_in"], 1, 2)
    zx, dtc = pl.pallas_call(
        functools.partial(_ssd_in_kernel, d=d, heads=heads),
        grid=(m // tm, nzx // tn),
        in_specs=[pl.BlockSpec((tm, d), lambda i, n: (i, 0)),
                  _mod_spec(layer, cidx, 6 * d, 2),
                  pl.BlockSpec((1, 1, d), lambda i, n: (layer, 0, 0)),
                  pl.BlockSpec((1, tn, d), lambda i, n: (j, n, 0)),
                  pl.BlockSpec((1, LANES, d), lambda i, n: (j, nzx // LANES, 0))],
        out_specs=[pl.BlockSpec((tn // LANES, tm, LANES), lambda i, n: (n, i, 0)),
                   pl.BlockSpec((SSD_GROUPS, SSD_DTW, tm), lambda i, n: (0, 0, i))],
        out_shape=[jax.ShapeDtypeStruct((nzx // LANES, m, LANES), F32),
                   jax.ShapeDtypeStruct((SSD_GROUPS, SSD_DTW, m), F32)],
        scratch_shapes=[pltpu.VMEM((tm, d), BF16)],
        compiler_params=_cparams(("arbitrary", "arbitrary")),
        name="ssd_in",
    )(x, mod4, P["g_norm1"].reshape(-1, 1, d), w_in_t, w_in_t)

    def compact(a):
        a = a.reshape(2, SSD_GROUPS, SSD_R).transpose(1, 0, 2).reshape(SSD_GROUPS, SSD_DTW, 1)
        return jnp.broadcast_to(a, (SSD_GROUPS, SSD_DTW, CHUNK))

    ncb = (inner + 2 * gn_w) // LANES
    cw = jnp.pad(P["ssd_w_conv"][j], ((0, 8 - SSD_CONV), (0, 0))).reshape(8, ncb, LANES).transpose(1, 0, 2)
    cbias = P["ssd_b_conv"][j].reshape(ncb, 1, LANES)
    dsum = jnp.repeat(P["ssd_d"][j][0] + P["ssd_d"][j][1], SSD_HEADDIM).reshape(SSD_GROUPS, 1, SSD_GW)
    gn = P["ssd_g_norm"][j].reshape(SSD_GROUPS, 1, SSD_GW)
    common = (zx, dtc, cw, cbias, compact(P["ssd_dt_bias"][j]), compact(P["ssd_a_log"][j]), dsum, gn)
    yp, st = _ssd_core(*common, None, None, seq_p, 0, nbp)
    h0 = st0.reshape(nbs, 2, SSD_GROUPS, SSD_GW, SSD_STATE)
    y = _ssd_core(*common, h0, yp, seq_s, npr, nbs)
    return y, st.reshape(nbp, 2, heads, SSD_HEADDIM, SSD_STATE)


def kernel(x_prompt, x_sample, cache_ckv, cache_kpe, state_ssm, c, c_ctx, w_ada, b_ada, g_norm1, g_norm2,
           mla_w_dq, mla_g_q, mla_w_uq, mla_w_dkv, mla_g_kv, mla_w_ukv, mla_g_qn, mla_g_kn, mla_w_o,
           cv_w_pw1, cv_b_pw1, cv_w_dw, cv_b_dw, cv_g_ln, cv_b_ln, cv_w_pw2, cv_b_pw2,
           ssd_w_in, ssd_w_conv, ssd_b_conv, ssd_dt_bias, ssd_a_log, ssd_d, ssd_g_norm, ssd_w_out,
           ffn_w_in, ffn_w_out):
    P = dict(g_norm1=g_norm1, g_norm2=g_norm2,
             mla_w_dq=mla_w_dq, mla_g_q=mla_g_q, mla_w_uq=mla_w_uq, mla_w_dkv=mla_w_dkv, mla_g_kv=mla_g_kv,
             mla_w_ukv=mla_w_ukv, mla_g_qn=mla_g_qn, mla_g_kn=mla_g_kn, mla_w_o=mla_w_o,
             cv_w_pw1=cv_w_pw1, cv_b_pw1=cv_b_pw1, cv_w_dw=cv_w_dw, cv_b_dw=cv_b_dw, cv_g_ln=cv_g_ln,
             cv_b_ln=cv_b_ln, cv_w_pw2=cv_w_pw2, cv_b_pw2=cv_b_pw2,
             ssd_w_in=ssd_w_in, ssd_w_conv=ssd_w_conv, ssd_b_conv=ssd_b_conv, ssd_dt_bias=ssd_dt_bias,
             ssd_a_log=ssd_a_log, ssd_d=ssd_d, ssd_g_norm=ssd_g_norm, ssd_w_out=ssd_w_out)
    nbp, seq_p, d = x_prompt.shape
    nbs, seq_s, _ = x_sample.shape
    depth = w_ada.shape[0]
    npr = nbp * seq_p
    x = (x_prompt.reshape(npr, d), x_sample.reshape(nbs * seq_s, d))
    cond8 = jnp.concatenate([c_ctx[None, :], c, jnp.zeros((8 - 1 - nbs, d), F32)], axis=0)
    mod4 = _ada(cond8, w_ada, b_ada).reshape(depth, 8, 1, 6 * d)
    mla_w = _mla_weights(P, seq_s)
    ffn_tm = min(1024, seq_s)
    ffn_cidx = _cond_index(ffn_tm, npr, seq_s)
    ckvs, kpes, ssms = [], [], []
    for i in range(depth):
        kind, j = i % N_MIXERS, i // N_MIXERS
        if kind != 0 and isinstance(x, tuple):
            x = jnp.concatenate(x, axis=0)
        y = w_o = None
        if kind == 0:
            x, ckv, kpe = _mla_layer(x, mod4, P, mla_w, i, j, cache_ckv[:, j], cache_kpe[:, j], npr, seq_p, seq_s)
            ckvs.append(ckv)
            kpes.append(kpe)
        elif kind == 1:
            x = _conv_layer(x, mod4, P, i, j, npr, seq_p, seq_s)
        else:
            y, st = _ssd_layer(x, mod4, P, i, j, state_ssm[:, j], npr, seq_p, seq_s)
            w_o = ssd_w_out[j].astype(BF16)
            ssms.append(st)
        x = _ffn(x, mod4, g_norm2, ffn_w_in, ffn_w_out, i, ffn_cidx, ffn_tm, 256, npr,
                 y=y, w_o=w_o, split=(i == depth - 1))
    return (x[0].reshape(nbp, seq_p, d), x[1].reshape(nbs, seq_s, d),
            jnp.stack(ckvs, axis=1), jnp.stack(kpes, axis=1), jnp.stack(ssms, axis=1))
```

```python
import functools

import jax
import jax.numpy as jnp
import numpy as np
from jax import lax
from jax.experimental import pallas as pl
from jax.experimental.pallas import tpu as pltpu

F32 = jnp.float32
BF16 = jnp.bfloat16

EPS = 1e-6
ROPE_BASE = 10000.0
GRID_W = 64
N_MIXERS = 3

LANES = 128
VMEM_LIMIT_BYTES = 56 * 1024 * 1024

MLA_HEADS = 16
QK_NOPE = 64
QK_ROPE = 32
QK_DIM = QK_NOPE + QK_ROPE
V_HEAD = 64
HEAD_PAD = LANES
CONV_WIDTH = 31
SSD_HEADDIM = 64
SSD_GROUPS = 4
SSD_STATE = 128
SSD_CONV = 5
CHUNK = 128


def _cparams(sem):
    return pltpu.CompilerParams(dimension_semantics=sem, vmem_limit_bytes=VMEM_LIMIT_BYTES)


def _silu(x):
    return x * jax.nn.sigmoid(x)


def _normmod(x, g, sh, sc):
    ms = jnp.mean(x * x, axis=-1, keepdims=True)
    return (x * lax.rsqrt(ms + EPS)) * (g * (1.0 + sc)) + sh


def _cond_index(tm, n_prompt_rows, sample_seq):
    def f(i):
        r = i * tm
        return jnp.where(r < n_prompt_rows, 0, 1 + (r - n_prompt_rows) // sample_seq)
    return f


def _mod_spec(layer, cidx, d6, ngrid):
    if ngrid == 1:
        return pl.BlockSpec((1, 1, 1, d6), lambda i: (layer, cidx(i), 0, 0))
    return pl.BlockSpec((1, 1, 1, d6), lambda i, j: (layer, cidx(i), 0, 0))


def _ada_kernel(c_ref, w_ref, b_ref, o_ref):
    s = _silu(c_ref[...]).astype(BF16)
    o_ref[0] = jnp.dot(s, w_ref[0].astype(BF16), preferred_element_type=F32) + b_ref[0]


def _ada(cond8, w_ada, b_ada):
    depth, d, n = w_ada.shape
    tn = n // 4
    return pl.pallas_call(
        _ada_kernel,
        grid=(depth, n // tn),
        in_specs=[pl.BlockSpec((8, d), lambda l, j: (0, 0)),
                  pl.BlockSpec((1, d, tn), lambda l, j: (l, 0, j)),
                  pl.BlockSpec((1, 1, tn), lambda l, j: (l, 0, j))],
        out_specs=pl.BlockSpec((1, 8, tn), lambda l, j: (l, 0, j)),
        out_shape=jax.ShapeDtypeStruct((depth, 8, n), F32),
        compiler_params=_cparams(("arbitrary", "arbitrary")),
        name="ada",
    )(cond8, w_ada, b_ada.reshape(depth, 1, n))


FFN_SLOTS = 3


def _ffn_kernel(*refs, d, n_in_a, n_out_a, fused, layer, hid, th, ntiles):
    refs = list(refs)
    xa_ref = refs.pop(0)
    xb_ref = refs.pop(0) if n_in_a is not None else None
    y_ref, wo1_ref = (refs.pop(0), refs.pop(0)) if fused else (None, None)
    mod_ref, g_ref, win_hbm, wout_hbm = refs[:4]
    if n_out_a is None:
        o_ref, h_sc, wa_buf, wu_buf, wo_buf, sem = refs[4:]
        acc = o_ref
    else:
        oa_ref, ob_ref, h_sc, acc, wa_buf, wu_buf, wo_buf, sem = refs[4:]
    i = pl.program_id(0)
    nj = hid // th
    base = lax.rem(i * (nj % FFN_SLOTS), FFN_SLOTS)

    def slot_of(j):
        return lax.rem(base + j % FFN_SLOTS, FFN_SLOTS)

    def chunk_copies(j):
        c, s = j % nj, slot_of(j)
        return (pltpu.make_async_copy(win_hbm.at[layer, :, pl.ds(c * th, th)], wa_buf.at[s], sem.at[0, s]),
                pltpu.make_async_copy(win_hbm.at[layer, :, pl.ds(hid + c * th, th)], wu_buf.at[s], sem.at[1, s]),
                pltpu.make_async_copy(wout_hbm.at[layer, pl.ds(c * th, th), :], wo_buf.at[s], sem.at[2, s]))

    def start(j):
        for cp in chunk_copies(j):
            cp.start()

    def wait(j):
        for cp in chunk_copies(j):
            cp.wait()

    @pl.when(i == 0)
    def _():
        start(0)
        start(1)

    wait(0)
    wa0, wu0 = wa_buf[base].astype(BF16), wu_buf[base].astype(BF16)
    tm = acc.shape[0]
    rc = min(tm, 256)
    a_rows, u_rows = [], []
    for r0 in range(0, tm, rc):
        rows = slice(r0, r0 + rc)
        x = xa_ref[rows, :] if xb_ref is None else jnp.where(i < n_in_a, xa_ref[rows, :], xb_ref[rows, :])
        if fused:
            x = x + mod_ref[0, 0, :, 2 * d:3 * d] * jnp.dot(y_ref[rows, :], wo1_ref[...],
                                                             preferred_element_type=F32)
        acc[rows, :] = x
        hr = _normmod(x, g_ref[0], mod_ref[0, 0, :, 3 * d:4 * d], mod_ref[0, 0, :, 4 * d:5 * d]).astype(BF16)
        h_sc[rows, :] = hr
        a_rows.append(jnp.dot(hr, wa0, preferred_element_type=F32))
        u_rows.append(jnp.dot(hr, wu0, preferred_element_type=F32))
    au = (jnp.concatenate(a_rows, axis=0), jnp.concatenate(u_rows, axis=0))

    gate2 = mod_ref[0, 0, :, 5 * d:6 * d]
    h = h_sc[...]

    def up(j):
        s = slot_of(j)
        return (jnp.dot(h, wa_buf[s].astype(BF16), preferred_element_type=F32),
                jnp.dot(h, wu_buf[s].astype(BF16), preferred_element_type=F32))

    for j in range(nj):
        if j + 1 < nj:
            wait(j + 1)
        if j + 2 < nj:
            start(j + 2)
        else:
            @pl.when(i + 1 < ntiles)
            def _(j=j):
                start(j + 2)
        a, u = au
        t = (_silu(a) * u).astype(BF16)
        if j + 1 < nj:
            au = up(j + 1)
        acc[...] += gate2 * jnp.dot(t, wo_buf[slot_of(j)].astype(BF16), preferred_element_type=F32)

    if n_out_a is not None:
        @pl.when(i < n_out_a)
        def _():
            oa_ref[...] = acc[...]

        @pl.when(i >= n_out_a)
        def _():
            ob_ref[...] = acc[...]


def _ffn(x, mod4, g2, w_in, w_out, layer, cidx, tm, th, npr, y=None, w_o=None, split=False):
    pair = isinstance(x, tuple)
    d = x[0].shape[1] if pair else x.shape[1]
    m = (x[0].shape[0] + x[1].shape[0]) if pair else x.shape[0]
    hid = w_out.shape[1]
    nj = hid // th
    n_a = npr // tm
    fused = y is not None
    if pair:
        in_specs = [pl.BlockSpec((tm, d), lambda i: (jnp.minimum(i, n_a - 1), 0)),
                    pl.BlockSpec((tm, d), lambda i: (jnp.maximum(i - n_a, 0), 0))]
        args = list(x)
    else:
        in_specs = [pl.BlockSpec((tm, d), lambda i: (i, 0))]
        args = [x]
    if fused:
        k = y.shape[1]
        in_specs += [pl.BlockSpec((tm, k), lambda i: (i, 0)), pl.BlockSpec((k, d), lambda i: (0, 0))]
        args += [y, w_o]
    in_specs += [_mod_spec(layer, cidx, 6 * d, 1),
                 pl.BlockSpec((1, 1, d), lambda i: (layer, 0, 0)),
                 pl.BlockSpec(memory_space=pl.ANY),
                 pl.BlockSpec(memory_space=pl.ANY)]
    args += [mod4, g2.reshape(g2.shape[0], 1, d), w_in, w_out]
    scratch = [pltpu.VMEM((tm, d), BF16)]
    if split:
        out_specs = [pl.BlockSpec((tm, d), lambda i: (jnp.minimum(i, n_a - 1), 0)),
                     pl.BlockSpec((tm, d), lambda i: (jnp.maximum(i - n_a, 0), 0))]
        out_shape = [jax.ShapeDtypeStruct((npr, d), F32), jax.ShapeDtypeStruct((m - npr, d), F32)]
        scratch.append(pltpu.VMEM((tm, d), F32))
    else:
        out_specs = pl.BlockSpec((tm, d), lambda i: (i, 0))
        out_shape = jax.ShapeDtypeStruct((m, d), F32)
    scratch += [pltpu.VMEM((FFN_SLOTS, d, th), F32), pltpu.VMEM((FFN_SLOTS, d, th), F32),
                pltpu.VMEM((FFN_SLOTS, th, d), F32), pltpu.SemaphoreType.DMA((3, FFN_SLOTS))]
    assert nj >= 2 and hid % th == 0
    return pl.pallas_call(
        functools.partial(_ffn_kernel, d=d, n_in_a=n_a if pair else None, n_out_a=n_a if split else None,
                          fused=fused, layer=layer, hid=hid, th=th, ntiles=m // tm),
        grid=(m // tm,),
        in_specs=in_specs,
        out_specs=out_specs,
        out_shape=out_shape,
        scratch_shapes=scratch,
        compiler_params=_cparams(("arbitrary",)),
        name="ffn_proj" if fused else "ffn",
    )(*args)


def _conv_a_kernel(x_ref, mod_ref, g_ref, w_ref, b_ref, o_ref, w_sc, *, d):
    @pl.when(pl.program_id(0) == 0)
    def _():
        w_sc[...] = w_ref[0].astype(BF16)

    h = _normmod(x_ref[...], g_ref[0], mod_ref[0, 0, :, 0:d], mod_ref[0, 0, :, d:2 * d]).astype(BF16)
    y = jnp.dot(h, w_sc[...], preferred_element_type=F32) + b_ref[0]
    u = y[:, :d] * jax.nn.sigmoid(y[:, d:])
    for cb in range(d // LANES):
        o_ref[cb] = u[:, cb * LANES:(cb + 1) * LANES]


def _conv_b_kernel(*refs, d, t, rc, top):
    (x_ref, u_ref, mod_ref, wdw_ref, bdw_ref, gln_ref, bln_ref, w2_ref, b2_ref) = refs[:9]
    o_ref, upad, conv_sc, y_sc, w_sc = refs[-5:]
    nb = d // LANES
    half = (CONV_WIDTH - 1) // 2

    @pl.when(pl.program_id(0) == 0)
    def _():
        w_sc[...] = w2_ref[0].astype(BF16)
        upad[:, 0:top, :] = jnp.zeros((nb, top, LANES), F32)
        upad[:, top + t:2 * top + t, :] = jnp.zeros((nb, top, LANES), F32)

    upad[:, top:top + t, :] = u_ref[...]

    def conv_block(cb, carry):
        for r0 in range(0, t, rc):
            acc = jnp.broadcast_to(bdw_ref[cb], (rc, LANES))
            for k in range(CONV_WIDTH):
                acc = acc + wdw_ref[cb, pl.ds(k, 1), :] * upad[cb, pl.ds(r0 + top - half + k, rc), :]
            conv_sc[cb, pl.ds(r0, rc), :] = acc
        return carry

    lax.fori_loop(0, nb, conv_block, 0)

    rl = min(t, 256)
    for r0 in range(0, t, rl):
        s = jnp.zeros((rl, 1), F32)
        for cb in range(nb):
            s = s + jnp.sum(conv_sc[cb, r0:r0 + rl, :], axis=-1, keepdims=True)
        mean = s * (1.0 / d)
        v = jnp.zeros((rl, 1), F32)
        for cb in range(nb):
            c = conv_sc[cb, r0:r0 + rl, :] - mean
            v = v + jnp.sum(c * c, axis=-1, keepdims=True)
        rstd = lax.rsqrt(v * (1.0 / d) + EPS)
        for cb in range(nb):
            sl = slice(cb * LANES, (cb + 1) * LANES)
            y = (conv_sc[cb, r0:r0 + rl, :] - mean) * rstd * gln_ref[:, sl] + bln_ref[:, sl]
            y_sc[r0:r0 + rl, sl] = _silu(y).astype(BF16)

    gate = mod_ref[0, 0, :, 2 * d:3 * d]
    r = jnp.dot(y_sc[...], w_sc[...], preferred_element_type=F32) + b2_ref[0]
    o_ref[...] = x_ref[...] + gate * r


def _conv_b(x, u, mod4, wdw, bdw, gln, bln, w2, b2, layer, j, t, row0, nseq, cond0, cond_step, obuf):
    m, d = x.shape
    nb = d // LANES
    top = 16
    blk0 = row0 // t
    in_specs = [pl.BlockSpec((t, d), lambda i: (blk0 + i, 0)),
                pl.BlockSpec((nb, t, LANES), lambda i: (0, blk0 + i, 0)),
                pl.BlockSpec((1, 1, 1, 6 * d), lambda i: (layer, cond0 + cond_step * i, 0, 0)),
                pl.BlockSpec((nb, 32, LANES), lambda i: (0, 0, 0)),
                pl.BlockSpec((nb, 1, LANES), lambda i: (0, 0, 0)),
                pl.BlockSpec((1, d), lambda i: (0, 0)),
                pl.BlockSpec((1, d), lambda i: (0, 0)),
                pl.BlockSpec((1, d, d), lambda i: (j, 0, 0)),
                pl.BlockSpec((1, 1, d), lambda i: (j, 0, 0))]
    args = [x, u, mod4, wdw, bdw, gln, bln, w2, b2]
    aliases = {}
    if obuf is not None:
        in_specs.append(pl.BlockSpec(memory_space=pl.ANY))
        args.append(obuf)
        aliases = {len(args) - 1: 0}
    return pl.pallas_call(
        functools.partial(_conv_b_kernel, d=d, t=t, rc=64, top=top),
        grid=(nseq,),
        in_specs=in_specs,
        out_specs=pl.BlockSpec((t, d), lambda i: (blk0 + i, 0)),
        out_shape=jax.ShapeDtypeStruct((m, d), F32),
        input_output_aliases=aliases,
        scratch_shapes=[pltpu.VMEM((nb, t + 2 * top, LANES), F32),
                        pltpu.VMEM((nb, t, LANES), F32),
                        pltpu.VMEM((t, d), BF16),
                        pltpu.VMEM((d, d), BF16)],
        compiler_params=_cparams(("arbitrary",)),
        name="conv_b",
    )(*args)


def _conv_layer(x, mod4, P, layer, j, npr, seq_p, seq_s):
    m, d = x.shape
    nb = d // LANES
    tm = min(512, seq_s)
    cidx = _cond_index(tm, npr, seq_s)
    u = pl.pallas_call(
        functools.partial(_conv_a_kernel, d=d),
        grid=(m // tm,),
        in_specs=[pl.BlockSpec((tm, d), lambda i: (i, 0)),
                  _mod_spec(layer, cidx, 6 * d, 1),
                  pl.BlockSpec((1, 1, d), lambda i: (layer, 0, 0)),
                  pl.BlockSpec((1, d, 2 * d), lambda i: (j, 0, 0)),
                  pl.BlockSpec((1, 1, 2 * d), lambda i: (j, 0, 0))],
        out_specs=pl.BlockSpec((nb, tm, LANES), lambda i: (0, i, 0)),
        out_shape=jax.ShapeDtypeStruct((nb, m, LANES), F32),
        scratch_shapes=[pltpu.VMEM((d, 2 * d), BF16)],
        compiler_params=_cparams(("arbitrary",)),
        name="conv_a",
    )(x, mod4, P["g_norm1"].reshape(-1, 1, d), P["cv_w_pw1"], P["cv_b_pw1"].reshape(-1, 1, 2 * d))
    wdw = jnp.pad(P["cv_w_dw"][j], ((0, 32 - CONV_WIDTH), (0, 0))).reshape(32, nb, LANES).transpose(1, 0, 2)
    bdw = P["cv_b_dw"][j].reshape(nb, 1, LANES)
    gln = P["cv_g_ln"][j].reshape(1, d)
    bln = P["cv_b_ln"][j].reshape(1, d)
    b2 = P["cv_b_pw2"].reshape(-1, 1, d)
    args = (x, u, mod4, wdw, bdw, gln, bln, P["cv_w_pw2"], b2, layer, j)
    op = _conv_b(*args, seq_p, 0, npr // seq_p, 0, 0, None)
    return _conv_b(*args, seq_s, npr, (m - npr) // seq_s, 1, 1, op)


_LOG2E = 1.4426950408889634
_ROT = QK_ROPE // 4


def _swap_index():
    idx = list(range(HEAD_PAD))
    for base in (QK_NOPE, QK_NOPE + 2 * _ROT):
        for l in range(_ROT):
            idx[base + l], idx[base + _ROT + l] = base + _ROT + l, base + l
    return jnp.asarray(idx, jnp.int32)


def _swap_cols(w):
    lane = jnp.arange(HEAD_PAD)
    rot = (lane >= QK_NOPE) & (lane < QK_DIM)
    return jnp.where(rot, jnp.take(w, _swap_index(), axis=-1), 0.0)


def _rope_tables(seq, gqn, gkn):
    rows = seq // GRID_W
    row = jnp.repeat(jnp.arange(rows, dtype=F32), GRID_W)
    col = jnp.tile(jnp.arange(GRID_W, dtype=F32), rows)
    axis = QK_ROPE // 2
    inv = ROPE_BASE ** (-jnp.arange(0, axis, 2, dtype=F32) / axis)
    ar, ac = row[:, None] * inv, col[:, None] * inv
    cr, sr, cc, sc = jnp.cos(ar), jnp.sin(ar), jnp.cos(ac), jnp.sin(ac)
    one = jnp.ones((seq, QK_NOPE), F32)
    zn = jnp.zeros((seq, QK_NOPE), F32)
    zp = jnp.zeros((seq, HEAD_PAD - QK_DIM), F32)
    c = jnp.concatenate([one, cr, cr, cc, cc, zp], axis=1)
    s = jnp.concatenate([zn, -sr, sr, -sc, sc, zp], axis=1)
    idx = _swap_index()
    return tuple(t for g in (gqn, gkn) for t in (c[None] * g, s[None] * jnp.take(g, idx, axis=-1)))


def _head_rstd(xh):
    ms = jnp.sum(xh * xh, axis=-1, keepdims=True) * (1.0 / QK_DIM)
    return lax.rsqrt(ms + EPS)


def _expand_kv(ckv_b, kpe, wuk_ref, wv_ref, gk, rot, k_ref, v_ref, v_transposed):
    kk = jnp.dot(ckv_b, wuk_ref[0], preferred_element_type=F32)
    if v_transposed:
        v = lax.dot_general(wv_ref[0], ckv_b, _NT, preferred_element_type=F32)
    else:
        v = jnp.dot(ckv_b, wv_ref[0], preferred_element_type=F32)
    v_ref[...] = v.astype(BF16)
    for h in range(MLA_HEADS):
        sl = slice(h * HEAD_PAD, (h + 1) * HEAD_PAD)
        kh = kk[:, sl] + kpe
        r = _head_rstd(kh)
        kh = (kh * gk) * r if rot is None else (kh * rot[0] + rot[1]) * r
        k_ref[:, sl] = kh.astype(BF16)


def _mla_proj_kernel(*refs, d, rope):
    (x_ref, mod_ref, g_ref, wcat_ref, gq_ref, wuq_ref, gkv_ref, wuk_ref, wv_ref, gqn_ref, gkn_ref) = refs[:11]
    if rope:
        cq_ref, sq_ref, ck_ref, sk_ref = refs[11:15]
        q_ref, k_ref, v_ref = refs[15:18]
    else:
        q_ref, k_ref, v_ref, ckv_ref, kpe_ref = refs[11:16]
    h = _normmod(x_ref[...], g_ref[0], mod_ref[0, 0, :, 0:d], mod_ref[0, 0, :, d:2 * d]).astype(BF16)
    t = jnp.dot(h, wcat_ref[0], preferred_element_type=F32)
    nq = gq_ref.shape[2]
    nkv = gkv_ref.shape[2]
    hq = MLA_HEADS * HEAD_PAD
    qd, craw, kpe = t[:, :nq], t[:, nq:nq + nkv], t[:, nq + nkv:nq + nkv + HEAD_PAD]
    cq = (qd * lax.rsqrt(jnp.mean(qd * qd, axis=-1, keepdims=True) + EPS) * gq_ref[0]).astype(BF16)
    ckv = craw * lax.rsqrt(jnp.mean(craw * craw, axis=-1, keepdims=True) + EPS) * gkv_ref[0]
    if not rope:
        ckv_ref[...] = ckv
        kpe_ref[...] = kpe
    q = jnp.dot(cq, wuq_ref[0], preferred_element_type=F32)
    if rope:
        ct, st = cq_ref[0], sq_ref[0]
    else:
        gqn = gqn_ref[0]
    for hd in range(MLA_HEADS):
        sl = slice(hd * HEAD_PAD, (hd + 1) * HEAD_PAD)
        qh = q[:, sl]
        r = _head_rstd(qh)
        qh = (qh * ct + q[:, hq + hd * HEAD_PAD:hq + (hd + 1) * HEAD_PAD] * st) * r if rope else (qh * gqn) * r
        q_ref[:, sl] = qh.astype(BF16)
    rot = None
    if rope:
        kpe2 = t[:, nq + nkv + HEAD_PAD:]
        rot = (ck_ref[0], kpe2 * sk_ref[0])
    _expand_kv(ckv.astype(BF16), kpe, wuk_ref, wv_ref, gkn_ref[0], rot, k_ref, v_ref, rope)


def _kv_ctx_kernel(ckv_ref, kpe_ref, wuk_ref, wv_ref, gkn_ref, k_ref, v_ref):
    _expand_kv(ckv_ref[...].astype(BF16), kpe_ref[...], wuk_ref, wv_ref, gkn_ref[0], None, k_ref, v_ref, True)


_NT = (((1,), (1,)), ((), ()))


def _attn_prompt_kernel(q_ref, k_ref, v_ref, x_ref, mod_ref, wo_ref, o_ref, a_sc, *, d):
    t = q_ref.shape[0]
    lane = lax.broadcasted_iota(jnp.int32, (t, LANES), 1)
    for hp in range(MLA_HEADS // 2):
        vp = v_ref[:, hp * LANES:(hp + 1) * LANES]
        outs = []
        for hh in range(2):
            sl = slice((2 * hp + hh) * HEAD_PAD, (2 * hp + hh + 1) * HEAD_PAD)
            s = lax.dot_general(q_ref[:, sl], k_ref[:, sl], _NT, preferred_element_type=F32)
            e = jnp.exp2(s - s.max(axis=-1, keepdims=True))
            l = jnp.sum(e, axis=-1, keepdims=True)
            outs.append(jnp.dot(e.astype(BF16), vp, preferred_element_type=F32) * (1.0 / l))
        a_sc[:, hp * LANES:(hp + 1) * LANES] = jnp.where(lane < V_HEAD, outs[0], outs[1]).astype(BF16)
    gate = mod_ref[0, 0, :, 2 * d:3 * d]
    o_ref[...] = x_ref[...] + gate * jnp.dot(a_sc[...], wo_ref[0], preferred_element_type=F32)


def _attn_sample_kernel(q_ref, kc_ref, vtc_ref, kl_ref, vtl_ref, x_ref, mod_ref, wo_ref, obuf_ref, o_ref, *, d):
    del obuf_ref
    nh = q_ref.shape[1] // HEAD_PAD
    row = lax.broadcasted_iota(jnp.int32, (LANES, 1), 0)

    def scores(hh):
        sl = slice(hh * HEAD_PAD, (hh + 1) * HEAD_PAD)
        qh = q_ref[:, sl]
        return [lax.dot_general(k_ref[:, sl], qh, _NT, preferred_element_type=F32)
                for k_ref in (kc_ref, kl_ref)]

    halves = []
    nxt = scores(0)
    for hh in range(nh):
        sts = nxt
        if hh + 1 < nh:
            nxt = scores(hh + 1)
        m = jnp.max(sts[0], axis=0, keepdims=True)
        for st in sts[1:]:
            m = jnp.maximum(m, jnp.max(st, axis=0, keepdims=True))
        pr = slice((hh // 2) * LANES, (hh // 2 + 1) * LANES)
        own = (row < V_HEAD) if hh % 2 == 0 else (row >= V_HEAD)
        acc = None
        for st, vt_ref in zip(sts, (vtc_ref, vtl_ref)):
            e = jnp.exp2(st - m).astype(BF16)
            vt = vt_ref[pr, :]
            lhs = jnp.where(own, vt, jnp.ones_like(vt))
            pv = jnp.dot(lhs, e, preferred_element_type=F32)
            acc = pv if acc is None else acc + pv
        if hh % 2 == 0:
            halves.append(acc[0:V_HEAD] * (1.0 / acc[V_HEAD:V_HEAD + 1]))
        else:
            halves.append(acc[V_HEAD:2 * V_HEAD] * (1.0 / acc[0:1]))
    attn = jnp.concatenate(halves, axis=0).T.astype(BF16)
    gate = mod_ref[0, 0, :, 2 * d:3 * d]
    o_ref[...] = x_ref[...] + gate * jnp.dot(attn, wo_ref[0], preferred_element_type=F32)


def _mla_weights(P, seq_s):
    n = P["mla_w_dq"].shape[0]
    nq = P["mla_w_dq"].shape[2]
    nkv = P["mla_g_kv"].shape[1]
    pad = HEAD_PAD - QK_DIM
    w_dkv = P["mla_w_dkv"]
    w_kpe = jnp.pad(w_dkv[:, :, nkv:], ((0, 0), (0, 0), (QK_NOPE, pad)))
    wcat = jnp.concatenate([P["mla_w_dq"], w_dkv[:, :, :nkv], w_kpe], axis=2).astype(BF16)
    wuq = jnp.pad(P["mla_w_uq"].reshape(n, nq, MLA_HEADS, QK_DIM), ((0, 0), (0, 0), (0, 0), (0, pad)))
    wuq = wuq.reshape(n, nq, MLA_HEADS * HEAD_PAD).astype(BF16)
    wukv = P["mla_w_ukv"].reshape(n, nkv, MLA_HEADS, QK_NOPE + V_HEAD)
    wuk = jnp.pad(wukv[..., :QK_NOPE], ((0, 0), (0, 0), (0, 0), (0, HEAD_PAD - QK_NOPE)))
    wuk = wuk.reshape(n, nkv, MLA_HEADS * HEAD_PAD).astype(BF16)
    wv = wukv[..., QK_NOPE:].reshape(n, nkv, MLA_HEADS * V_HEAD).astype(BF16)
    gq = P["mla_g_q"].reshape(n, 1, nq)
    gkv = P["mla_g_kv"].reshape(n, 1, nkv)
    gqn = (jnp.pad(P["mla_g_qn"], ((0, 0), (0, pad))) * (QK_DIM ** -0.5 * _LOG2E)).reshape(n, 1, HEAD_PAD)
    gkn = jnp.pad(P["mla_g_kn"], ((0, 0), (0, pad))).reshape(n, 1, HEAD_PAD)
    wuq2 = _swap_cols(wuq.reshape(n, nq, MLA_HEADS, HEAD_PAD)).reshape(n, nq, MLA_HEADS * HEAD_PAD)
    wuq_r = jnp.concatenate([wuq, wuq2], axis=2)
    wcat_r = jnp.concatenate([wcat, _swap_cols(w_kpe).astype(BF16)], axis=2)
    return dict(plain=(wcat, gq, wuq, gkv, wuk, wv, gqn, gkn), rope=(wcat_r, wuq_r, jnp.swapaxes(wv, 1, 2)),
                wo=P["mla_w_o"].astype(BF16), tabs=_rope_tables(seq_s, gqn, gkn))


def _layer_spec(a, j):
    return pl.BlockSpec((1,) + a.shape[1:], lambda *_: (j,) + (0,) * (a.ndim - 1))


def _mla_proj(x, xblk0, mod4, g1, W, layer, j, row0, nrows, cidx, tm, rope_args):
    d = x.shape[1]
    wcat, gq, wuq, gkv, wuk, wv, gqn, gkn = W
    blk0 = row0 // tm
    hq = MLA_HEADS * HEAD_PAD
    hv = MLA_HEADS * V_HEAD
    nkv = gkv.shape[2]
    rope = rope_args is not None
    args = [x, mod4, g1, wcat, gq, wuq, gkv, wuk, wv, gqn, gkn]
    if rope:
        wcat_r, wuq_r, wvt, tabs = rope_args
        args[3], args[5], args[8] = wcat_r, wuq_r, wvt
    in_specs = [pl.BlockSpec((tm, d), lambda i: (xblk0 + i, 0)),
                pl.BlockSpec((1, 1, 1, 6 * d), lambda i: (layer, cidx(blk0 + i), 0, 0)),
                pl.BlockSpec((1, 1, d), lambda i: (layer, 0, 0))] + [_layer_spec(a, j) for a in args[3:]]
    out_specs = [pl.BlockSpec((tm, hq), lambda i: (i, 0)),
                 pl.BlockSpec((tm, hq), lambda i: (i, 0))]
    out_shape = [jax.ShapeDtypeStruct((nrows, hq), BF16),
                 jax.ShapeDtypeStruct((nrows, hq), BF16)]
    if rope:
        per = tabs[0].shape[1] // tm
        in_specs += [pl.BlockSpec((1, tm, HEAD_PAD), lambda i: (j, i % per, 0))] * 4
        args += list(tabs)
        out_specs += [pl.BlockSpec((hv, tm), lambda i: (0, i))]
        out_shape += [jax.ShapeDtypeStruct((hv, nrows), BF16)]
    else:
        out_specs += [pl.BlockSpec((tm, hv), lambda i: (i, 0)),
                      pl.BlockSpec((tm, nkv), lambda i: (i, 0)), pl.BlockSpec((tm, HEAD_PAD), lambda i: (i, 0))]
        out_shape += [jax.ShapeDtypeStruct((nrows, hv), BF16),
                      jax.ShapeDtypeStruct((nrows, nkv), F32), jax.ShapeDtypeStruct((nrows, HEAD_PAD), F32)]
    return pl.pallas_call(
        functools.partial(_mla_proj_kernel, d=d, rope=rope),
        grid=(nrows // tm,),
        in_specs=in_specs, out_specs=out_specs, out_shape=out_shape,
        compiler_params=_cparams(("arbitrary",)),
        name="mla_proj_rope" if rope else "mla_proj",
    )(*args)


def _mla_layer(x, mod4, P, MW, layer, j, cache_ckv, cache_kpe, npr, seq_p, seq_s):
    xa, xb = x if isinstance(x, tuple) else (x, x)
    d = xa.shape[1]
    nrs = xb.shape[0] if isinstance(x, tuple) else xb.shape[0] - npr
    m = npr + nrs
    nbp, nbs = npr // seq_p, nrs // seq_s
    past = cache_ckv.shape[1]
    W, (wcat_r, wuq_r, wvt), wo, tabs = MW["plain"], MW["rope"], MW["wo"], MW["tabs"]
    wuk, gkn = W[4], W[7]
    g1 = P["g_norm1"].reshape(-1, 1, d)
    hq = MLA_HEADS * HEAD_PAD
    hv = MLA_HEADS * V_HEAD
    tm = min(512, seq_s)
    cidx = _cond_index(tm, npr, seq_s)
    xb0 = 0 if isinstance(x, tuple) else npr // tm
    qp, kp, vp, ckv, kpe128 = _mla_proj(xa, 0, mod4, g1, W, layer, j, 0, npr, cidx, tm, None)
    qs, ks, vts = _mla_proj(xb, xb0, mod4, g1, W, layer, j, npr, nrs, cidx, tm, (wcat_r, wuq_r, wvt, tabs))
    cc = cache_ckv.reshape(nbs * past, -1)
    ck = jnp.pad(cache_kpe.reshape(nbs * past, -1), ((0, 0), (QK_NOPE, HEAD_PAD - QK_DIM)))
    kc, vtc = pl.pallas_call(
        _kv_ctx_kernel,
        grid=(nbs,),
        in_specs=[pl.BlockSpec((past, cc.shape[1]), lambda i: (i, 0)),
                  pl.BlockSpec((past, HEAD_PAD), lambda i: (i, 0)),
                  _layer_spec(wuk, j), _layer_spec(wvt, j), _layer_spec(gkn, j)],
        out_specs=[pl.BlockSpec((past, hq), lambda i: (i, 0)), pl.BlockSpec((hv, past), lambda i: (0, i))],
        out_shape=[jax.ShapeDtypeStruct((nbs * past, hq), BF16), jax.ShapeDtypeStruct((hv, nbs * past), BF16)],
        compiler_params=_cparams(("arbitrary",)),
        name="mla_kv_ctx",
    )(cc, ck, wuk, wvt, gkn)
    xp0 = 0
    xs0 = 0 if isinstance(x, tuple) else npr
    xn_p = pl.pallas_call(
        functools.partial(_attn_prompt_kernel, d=d),
        grid=(nbp,),
        in_specs=[pl.BlockSpec((seq_p, hq), lambda b: (b, 0)),
                  pl.BlockSpec((seq_p, hq), lambda b: (b, 0)),
                  pl.BlockSpec((seq_p, hv), lambda b: (b, 0)),
                  pl.BlockSpec((seq_p, d), lambda b: (xp0 // seq_p + b, 0)),
                  pl.BlockSpec((1, 1, 1, 6 * d), lambda b: (layer, 0, 0, 0)),
                  _layer_spec(wo, j)],
        out_specs=pl.BlockSpec((seq_p, d), lambda b: (b, 0)),
        out_shape=jax.ShapeDtypeStruct((m, d), F32),
        scratch_shapes=[pltpu.VMEM((seq_p, hv), BF16)],
        compiler_params=_cparams(("arbitrary",)),
        name="attn_prompt",
    )(qp, kp, vp, xa, mod4, wo)
    tq = min(seq_s, 512)
    nqt = seq_s // tq
    o0 = npr // tq
    xn = pl.pallas_call(
        functools.partial(_attn_sample_kernel, d=d),
        grid=(nbs, nqt),
        in_specs=[pl.BlockSpec((tq, hq), lambda b, qi: (b * nqt + qi, 0)),
                  pl.BlockSpec((past, hq), lambda b, qi: (b, 0)),
                  pl.BlockSpec((hv, past), lambda b, qi: (0, b)),
                  pl.BlockSpec((seq_s, hq), lambda b, qi: (b, 0)),
                  pl.BlockSpec((hv, seq_s), lambda b, qi: (0, b)),
                  pl.BlockSpec((tq, d), lambda b, qi: (xs0 // tq + b * nqt + qi, 0)),
                  pl.BlockSpec((1, 1, 1, 6 * d), lambda b, qi: (layer, 1 + b, 0, 0)),
                  _layer_spec(wo, j),
                  pl.BlockSpec(memory_space=pl.ANY)],
        out_specs=pl.BlockSpec((tq, d), lambda b, qi: (o0 + b * nqt + qi, 0)),
        out_shape=jax.ShapeDtypeStruct((m, d), F32),
        input_output_aliases={8: 0},
        compiler_params=_cparams(("arbitrary", "arbitrary")),
        name="attn_sample",
    )(qs, kc, vtc, ks, vts, xb, mod4, wo, xn_p)
    new_ckv = ckv.reshape(nbp, seq_p, -1)
    new_kpe = kpe128[:, QK_NOPE:QK_DIM].reshape(nbp, seq_p, QK_ROPE)
    return xn, new_ckv, new_kpe


SSD_R = 8
SSD_GW = SSD_R * SSD_HEADDIM
SSD_NX = SSD_GW // LANES
SSD_DTW = 2 * SSD_R
SSD_TOP = 8


def _ssd_in_kernel(x_ref, mod_ref, g_ref, w_ref, wdt_ref, zx_ref, dt_ref, h_sc, *, d, heads):
    i = pl.program_id(1)
    tm = x_ref.shape[0]
    rows = pl.ds(pl.multiple_of(i * tm, tm), tm)

    @pl.when(pl.program_id(0) == 0)
    def _():
        h = _normmod(x_ref[...], g_ref[0], mod_ref[0, 0, :, 0:d], mod_ref[0, 0, :, d:2 * d]).astype(BF16)
        h_sc[rows, :] = h
        row = lax.broadcasted_iota(jnp.int32, wdt_ref.shape[1:], 0)
        wdt = jnp.where(row < 2 * heads, wdt_ref[0], 0.0).astype(BF16)
        dtt = lax.dot_general(wdt, h, _NT, preferred_element_type=F32)
        for g in range(SSD_GROUPS):
            dt_ref[g, 0:SSD_R, :] = dtt[g * SSD_R:(g + 1) * SSD_R, :]
            dt_ref[g, SSD_R:SSD_DTW, :] = dtt[heads + g * SSD_R:heads + (g + 1) * SSD_R, :]

    r = lax.dot_general(h_sc[rows, :], w_ref[0].astype(BF16), _NT, preferred_element_type=F32)
    for c in range(r.shape[1] // LANES):
        zx_ref[c] = r[:, c * LANES:(c + 1) * LANES]


def _split3(x):
    hi = x.astype(BF16)
    r1 = x - hi.astype(F32)
    mid = r1.astype(BF16)
    lo = (r1 - mid.astype(F32)).astype(BF16)
    return hi, mid, lo


def _dot3_right(x, tri):
    return sum(jnp.dot(p, tri, preferred_element_type=F32) for p in _split3(x))


SSD_NPIECE = 7


def _ssd_spread_matrix():
    sel = np.zeros((2, CHUNK, SSD_R * LANES + 2 * SSD_GW), np.float32)
    for dr in range(2):
        for r in range(SSD_R):
            h = dr * SSD_R + r
            for p in range(3):
                sel[dr, p * SSD_DTW + h, r * LANES:(r + 1) * LANES] = 1.0
            for p in range(3, 5):
                sel[dr, p * SSD_DTW + h, SSD_R * LANES + r * SSD_HEADDIM:SSD_R * LANES + (r + 1) * SSD_HEADDIM] = 1.0
            for p in range(5, 7):
                c0 = SSD_R * LANES + SSD_GW
                sel[dr, p * SSD_DTW + h, c0 + r * SSD_HEADDIM:c0 + (r + 1) * SSD_HEADDIM] = 1.0
    return jnp.asarray(sel, BF16)


def _ssd_core_kernel(*refs, t, has_h0):
    (z_ref, x_ref, b_ref, c_ref, dt_ref, wx_ref, wb_ref, wc_ref, bx_ref, bb_ref, bc_ref,
     dtb_ref, alog_ref, dsum_ref, gn_ref, bsel_ref) = refs[:16]
    if has_h0:
        h0_ref, y_ref = refs[16], refs[18]
        st_ref = None
        scr = refs[19:]
    else:
        y_ref, st_ref = refs[16:18]
        h0_ref = None
        scr = refs[18:]
    (pad_sc, xc_sc, bcv_sc, ccv_sc, bt_sc, cb_sc, y_sc, dtt_sc, dat_sc, s_sc) = scr
    nc = t // CHUNK
    half = (SSD_CONV - 1) // 2
    nblk = SSD_NX + 2

    @pl.when((pl.program_id(0) == 0) & (pl.program_id(1) == 0))
    def _():
        pad_sc[:, 0:SSD_TOP, :] = jnp.zeros((nblk, SSD_TOP, LANES), F32)
        pad_sc[:, SSD_TOP + t:2 * SSD_TOP + t, :] = jnp.zeros((nblk, SSD_TOP, LANES), F32)

    pad_sc[0:SSD_NX, SSD_TOP:SSD_TOP + t, :] = x_ref[...]
    pad_sc[SSD_NX, SSD_TOP:SSD_TOP + t, :] = b_ref[0]
    pad_sc[SSD_NX + 1, SSD_TOP:SSD_TOP + t, :] = c_ref[0]

    def conv_chunk(c):
        r0 = c * CHUNK
        for blk in range(nblk):
            if blk < SSD_NX:
                w_r, bias, widx = wx_ref, bx_ref[blk], blk
            elif blk == SSD_NX:
                w_r, bias, widx = wb_ref, bb_ref[0], 0
            else:
                w_r, bias, widx = wc_ref, bc_ref[0], 0
            acc = jnp.broadcast_to(bias, (CHUNK, LANES))
            for k in range(SSD_CONV):
                acc = acc + w_r[widx, k:k + 1, :] * pad_sc[blk, r0 + SSD_TOP - half + k:r0 + SSD_TOP - half + k + CHUNK, :]
            acc = _silu(acc)
            if blk < SSD_NX:
                xc_sc[blk, r0:r0 + CHUNK, :] = acc
            elif blk == SSD_NX:
                bcv_sc[r0:r0 + CHUNK, :] = acc.astype(BF16)
                bt_sc[c] = acc.T.astype(BF16)
            else:
                ccv_sc[r0:r0 + CHUNK, :] = acc.astype(BF16)

    def gate_chunk(c):
        rows = slice(c * CHUNK, (c + 1) * CHUNK)
        yz = [y_sc[q, rows, :] * _silu(z_ref[q, rows, :]) for q in range(SSD_NX)]
        ms = sum(jnp.sum(a * a, axis=-1, keepdims=True) for a in yz) * (1.0 / SSD_GW)
        rstd = lax.rsqrt(ms + EPS)
        for q in range(SSD_NX):
            sl = slice(q * LANES, (q + 1) * LANES)
            y_ref[rows, sl] = (yz[q] * rstd * gn_ref[0, :, sl]).astype(BF16)

    a_row = -jnp.exp(alog_ref[0]) * _LOG2E
    for c in range(nc):
        v = dt_ref[0, :, c * CHUNK:(c + 1) * CHUNK] + dtb_ref[0]
        dt = jnp.maximum(v, 0.0) + jnp.log1p(jnp.exp(-jnp.abs(v)))
        dtt_sc[c] = dt
        dat_sc[c] = dt * a_row

    row = lax.broadcasted_iota(jnp.int32, (CHUNK, CHUNK), 0)
    colm = lax.broadcasted_iota(jnp.int32, (CHUNK, CHUNK), 1)
    lower = colm <= row
    upper = colm >= row
    tri_l = jnp.where(lower, 1.0, 0.0).astype(BF16)
    tri_u = jnp.where(upper, 1.0, 0.0).astype(BF16)
    lane = lax.broadcasted_iota(jnp.int32, (CHUNK, LANES), 1)
    head_row = lax.broadcasted_iota(jnp.int32, (SSD_DTW, 1), 0)
    zero16 = jnp.zeros((SSD_DTW, CHUNK), F32)

    col0, w0, e0 = 0, SSD_R * LANES, SSD_R * LANES + SSD_GW
    zpad = jnp.zeros((CHUNK - SSD_NPIECE * SSD_DTW, CHUNK), F32)

    def prepare(dr, c):
        lo = dr * SSD_R
        tri_t = tri_l if dr else tri_u
        own = (head_row >= lo) & (head_row < lo + SSD_R)
        rows = slice(c * CHUNK, (c + 1) * CHUNK)
        dtt = dtt_sc[c]
        cum_t = _dot3_right(dat_sc[c], tri_t)
        edge = cum_t[:, 0:1] if dr else cum_t[:, CHUNK - 1:CHUNK]
        w2_t = dtt * jnp.exp2(edge - cum_t)
        ecum_t = jnp.exp2(cum_t)
        pieces = (list(_split3(jnp.where(own, cum_t, zero16)))
                  + list(_split3(jnp.where(own, w2_t, zero16))[:2])
                  + list(_split3(jnp.where(own, ecum_t, zero16))[:2]))
        staged = jnp.concatenate([p.astype(F32) for p in pieces] + [zpad], axis=0)
        spread = jnp.dot(staged.T.astype(BF16), bsel_ref[dr], preferred_element_type=F32)
        cmat = ccv_sc[rows, :]
        if dr == 0:
            cb = lax.dot_general(cmat, bcv_sc[rows, :], _NT, preferred_element_type=F32)
            cb_sc[c] = cb
        else:
            cb = cb_sc[c]
        rowterm = cum_t - jnp.log2(dtt)
        return dict(dr=dr, c=c, rows=rows, lo=lo, rowterm=rowterm, spread=spread, cmat=cmat, cb=cb)

    def scan_chunk(pre):
        dr, c, rows, lo = pre["dr"], pre["c"], pre["rows"], pre["lo"]
        rowterm, spread, cmat, cb = pre["rowterm"], pre["spread"], pre["cmat"], pre["cb"]
        mask = upper if dr else lower
        s_prev = s_sc[...]
        yoff = jnp.dot(cmat, s_prev.astype(BF16), preferred_element_type=F32)
        xd = []
        for q in range(SSD_NX):
            xblk = xc_sc[q, rows, :]
            xb16 = xblk.astype(BF16)
            mh = []
            for r in (2 * q, 2 * q + 1):
                seg = spread[:, col0 + r * LANES:col0 + (r + 1) * LANES] - rowterm[lo + r:lo + r + 1, :]
                lm = jnp.exp2(jnp.where(mask, seg, -jnp.inf))
                mh.append((cb * lm).astype(BF16))
            rhs = jnp.concatenate([jnp.where(lane < SSD_HEADDIM, xb16, jnp.zeros_like(xb16)),
                                   jnp.where(lane >= SSD_HEADDIM, xb16, jnp.zeros_like(xb16))], axis=0)
            y = jnp.dot(jnp.concatenate(mh, axis=1), rhs, preferred_element_type=F32)
            sl = slice(q * LANES, (q + 1) * LANES)
            y = y + yoff[:, sl] * spread[:, e0 + q * LANES:e0 + (q + 1) * LANES]
            if dr == 0:
                y_sc[q, rows, :] = y + xblk * dsum_ref[0, :, sl]
            else:
                y_sc[q, rows, :] = y_sc[q, rows, :] + y
            xd.append((xblk * spread[:, w0 + q * LANES:w0 + (q + 1) * LANES]).astype(BF16))
        st = jnp.dot(bt_sc[c], jnp.concatenate(xd, axis=1), preferred_element_type=F32)
        er = 0 if dr else CHUNK - 1
        cd = spread[er:er + 1, e0:e0 + SSD_GW]
        s_sc[...] = s_prev * cd + st

    order = [(0, c) for c in range(nc)] + [(1, c) for c in range(nc - 1, -1, -1)]
    conv_chunk(0)
    pre = prepare(*order[0])
    for k, (dr, c) in enumerate(order):
        if c == (nc - 1 if dr else 0):
            if has_h0:
                s_sc[...] = h0_ref[0, dr, 0].T
            else:
                s_sc[...] = jnp.zeros_like(s_sc)
        if dr == 0 and c + 1 < nc:
            conv_chunk(c + 1)
        nxt = prepare(*order[k + 1]) if k + 1 < len(order) else None
        scan_chunk(pre)
        pre = nxt
        if dr == 1:
            gate_chunk(c)
        if st_ref is not None and c == (0 if dr else nc - 1):
            st_ref[0, dr, 0] = s_sc[...].T


def _ssd_core(zx, dtc, cw, cbias, dtb, alog, dsum, gn, h0, ybuf, t, row0, nseq):
    has_h0 = h0 is not None
    m = zx.shape[1]
    bsel = _ssd_spread_matrix()
    b0 = row0 // t
    nc = t // CHUNK
    nz = SSD_GROUPS * SSD_NX
    in_specs = [pl.BlockSpec((SSD_NX, t, LANES), lambda b, g: (g, b0 + b, 0)),
                pl.BlockSpec((SSD_NX, t, LANES), lambda b, g: (SSD_GROUPS + g, b0 + b, 0)),
                pl.BlockSpec((1, t, LANES), lambda b, g: (2 * nz + g, b0 + b, 0)),
                pl.BlockSpec((1, t, LANES), lambda b, g: (2 * nz + SSD_GROUPS + g, b0 + b, 0)),
                pl.BlockSpec((1, SSD_DTW, t), lambda b, g: (g, 0, b0 + b)),
                pl.BlockSpec((SSD_NX, 8, LANES), lambda b, g: (g, 0, 0)),
                pl.BlockSpec((1, 8, LANES), lambda b, g: (nz + g, 0, 0)),
                pl.BlockSpec((1, 8, LANES), lambda b, g: (nz + SSD_GROUPS + g, 0, 0)),
                pl.BlockSpec((SSD_NX, 1, LANES), lambda b, g: (g, 0, 0)),
                pl.BlockSpec((1, 1, LANES), lambda b, g: (nz + g, 0, 0)),
                pl.BlockSpec((1, 1, LANES), lambda b, g: (nz + SSD_GROUPS + g, 0, 0)),
                pl.BlockSpec((1, SSD_DTW, CHUNK), lambda b, g: (g, 0, 0)),
                pl.BlockSpec((1, SSD_DTW, CHUNK), lambda b, g: (g, 0, 0)),
                pl.BlockSpec((1, 1, SSD_GW), lambda b, g: (g, 0, 0)),
                pl.BlockSpec((1, 1, SSD_GW), lambda b, g: (g, 0, 0)),
                pl.BlockSpec(bsel.shape, lambda b, g: (0, 0, 0))]
    args = [zx, zx, zx, zx, dtc, cw, cw, cw, cbias, cbias, cbias, dtb, alog, dsum, gn, bsel]
    st_spec = pl.BlockSpec((1, 2, 1, SSD_GW, SSD_STATE), lambda b, g: (b, 0, g, 0, 0))
    y_spec = pl.BlockSpec((t, SSD_GW), lambda b, g: (b0 + b, g))
    y_shape = jax.ShapeDtypeStruct((m, SSD_GROUPS * SSD_GW), BF16)
    aliases = {}
    if has_h0:
        in_specs += [st_spec, pl.BlockSpec(memory_space=pl.ANY)]
        args += [h0, ybuf]
        aliases = {len(args) - 1: 0}
        out_specs, out_shape = y_spec, y_shape
    else:
        out_specs = [y_spec, st_spec]
        out_shape = [y_shape, jax.ShapeDtypeStruct((nseq, 2, SSD_GROUPS, SSD_GW, SSD_STATE), F32)]
    scratch = [pltpu.VMEM((SSD_NX + 2, t + 2 * SSD_TOP, LANES), F32),
               pltpu.VMEM((SSD_NX, t, LANES), F32),
               pltpu.VMEM((t, LANES), BF16),
               pltpu.VMEM((t, LANES), BF16),
               pltpu.VMEM((nc, CHUNK, CHUNK), BF16),
               pltpu.VMEM((nc, CHUNK, CHUNK), F32),
               pltpu.VMEM((SSD_NX, t, LANES), F32),
               pltpu.VMEM((nc, SSD_DTW, CHUNK), F32),
               pltpu.VMEM((nc, SSD_DTW, CHUNK), F32),
               pltpu.VMEM((SSD_STATE, SSD_GW), F32)]
    return pl.pallas_call(
        functools.partial(_ssd_core_kernel, t=t, has_h0=has_h0),
        grid=(nseq, SSD_GROUPS),
        in_specs=in_specs, out_specs=out_specs, out_shape=out_shape,
        input_output_aliases=aliases,
        scratch_shapes=scratch,
        compiler_params=_cparams(("arbitrary", "arbitrary")),
        name="ssd_core_h0" if has_h0 else "ssd_core",
    )(*args)


def _ssd_layer(x, mod4, P, layer, j, st0, npr, seq_p, seq_s):
    m, d = x.shape
    nbp, nbs = npr // seq_p, (m - npr) // seq_s
    heads = SSD_GROUPS * SSD_R
    inner = heads * SSD_HEADDIM
    gn_w = SSD_GROUPS * SSD_STATE
    nzx = 2 * inner + 2 * gn_w
    tm, tn = min(1024, seq_s), 1024
    cidx = _cond_index(tm, npr, seq_s)
    w_in_t = jnp.swapaxes(P["ssd_w_in"], 1, 2)
    nt = m // tm

    def row_tile(n, i):
        return jnp.where(n == 0, i, nt - 1)

    zx, dtc = pl.pallas_call(
        functools.partial(_ssd_in_kernel, d=d, heads=heads),
        grid=(nzx // tn, nt),
        in_specs=[pl.BlockSpec((tm, d), lambda n, i: (row_tile(n, i), 0)),
                  pl.BlockSpec((1, 1, 1, 6 * d), lambda n, i: (layer, cidx(row_tile(n, i)), 0, 0)),
                  pl.BlockSpec((1, 1, d), lambda n, i: (layer, 0, 0)),
                  pl.BlockSpec((1, tn, d), lambda n, i: (j, n, 0)),
                  pl.BlockSpec((1, LANES, d), lambda n, i: (j, nzx // LANES, 0))],
        out_specs=[pl.BlockSpec((tn // LANES, tm, LANES), lambda n, i: (n, i, 0)),
                   pl.BlockSpec((SSD_GROUPS, SSD_DTW, tm), lambda n, i: (0, 0, row_tile(n, i)))],
        out_shape=[jax.ShapeDtypeStruct((nzx // LANES, m, LANES), F32),
                   jax.ShapeDtypeStruct((SSD_GROUPS, SSD_DTW, m), F32)],
        scratch_shapes=[pltpu.VMEM((m, d), BF16)],
        compiler_params=_cparams(("arbitrary", "arbitrary")),
        name="ssd_in",
    )(x, mod4, P["g_norm1"].reshape(-1, 1, d), w_in_t, w_in_t)

    def compact(a):
        a = a.reshape(2, SSD_GROUPS, SSD_R).transpose(1, 0, 2).reshape(SSD_GROUPS, SSD_DTW, 1)
        return jnp.broadcast_to(a, (SSD_GROUPS, SSD_DTW, CHUNK))

    ncb = (inner + 2 * gn_w) // LANES
    cw = jnp.pad(P["ssd_w_conv"][j], ((0, 8 - SSD_CONV), (0, 0))).reshape(8, ncb, LANES).transpose(1, 0, 2)
    cbias = P["ssd_b_conv"][j].reshape(ncb, 1, LANES)
    dsum = jnp.repeat(P["ssd_d"][j][0] + P["ssd_d"][j][1], SSD_HEADDIM).reshape(SSD_GROUPS, 1, SSD_GW)
    gn = P["ssd_g_norm"][j].reshape(SSD_GROUPS, 1, SSD_GW)
    common = (zx, dtc, cw, cbias, compact(P["ssd_dt_bias"][j]), compact(P["ssd_a_log"][j]), dsum, gn)
    yp, st = _ssd_core(*common, None, None, seq_p, 0, nbp)
    h0 = st0.reshape(nbs, 2, SSD_GROUPS, SSD_GW, SSD_STATE)
    y = _ssd_core(*common, h0, yp, seq_s, npr, nbs)
    return y, st.reshape(nbp, 2, heads, SSD_HEADDIM, SSD_STATE)


def kernel(x_prompt, x_sample, cache_ckv, cache_kpe, state_ssm, c, c_ctx, w_ada, b_ada, g_norm1, g_norm2,
           mla_w_dq, mla_g_q, mla_w_uq, mla_w_dkv, mla_g_kv, mla_w_ukv, mla_g_qn, mla_g_kn, mla_w_o,
           cv_w_pw1, cv_b_pw1, cv_w_dw, cv_b_dw, cv_g_ln, cv_b_ln, cv_w_pw2, cv_b_pw2,
           ssd_w_in, ssd_w_conv, ssd_b_conv, ssd_dt_bias, ssd_a_log, ssd_d, ssd_g_norm, ssd_w_out,
           ffn_w_in, ffn_w_out):
    P = dict(g_norm1=g_norm1, g_norm2=g_norm2,
             mla_w_dq=mla_w_dq, mla_g_q=mla_g_q, mla_w_uq=mla_w_uq, mla_w_dkv=mla_w_dkv, mla_g_kv=mla_g_kv,
             mla_w_ukv=mla_w_ukv, mla_g_qn=mla_g_qn, mla_g_kn=mla_g_kn, mla_w_o=mla_w_o,
             cv_w_pw1=cv_w_pw1, cv_b_pw1=cv_b_pw1, cv_w_dw=cv_w_dw, cv_b_dw=cv_b_dw, cv_g_ln=cv_g_ln,
             cv_b_ln=cv_b_ln, cv_w_pw2=cv_w_pw2, cv_b_pw2=cv_b_pw2,
             ssd_w_in=ssd_w_in, ssd_w_conv=ssd_w_conv, ssd_b_conv=ssd_b_conv, ssd_dt_bias=ssd_dt_bias,
             ssd_a_log=ssd_a_log, ssd_d=ssd_d, ssd_g_norm=ssd_g_norm, ssd_w_out=ssd_w_out)
    nbp, seq_p, d = x_prompt.shape
    nbs, seq_s, _ = x_sample.shape
    depth = w_ada.shape[0]
    npr = nbp * seq_p
    x = (x_prompt.reshape(npr, d), x_sample.reshape(nbs * seq_s, d))
    cond8 = jnp.concatenate([c_ctx[None, :], c, jnp.zeros((8 - 1 - nbs, d), F32)], axis=0)
    mod4 = _ada(cond8, w_ada, b_ada).reshape(depth, 8, 1, 6 * d)
    mla_w = _mla_weights(P, seq_s)
    ffn_tm = min(1024, seq_s)
    ffn_cidx = _cond_index(ffn_tm, npr, seq_s)
    ckvs, kpes, ssms = [], [], []
    for i in range(depth):
        kind, j = i % N_MIXERS, i // N_MIXERS
        if kind != 0 and isinstance(x, tuple):
            x = jnp.concatenate(x, axis=0)
        y = w_o = None
        if kind == 0:
            x, ckv, kpe = _mla_layer(x, mod4, P, mla_w, i, j, cache_ckv[:, j], cache_kpe[:, j], npr, seq_p, seq_s)
            ckvs.append(ckv)
            kpes.append(kpe)
        elif kind == 1:
            x = _conv_layer(x, mod4, P, i, j, npr, seq_p, seq_s)
        else:
            y, st = _ssd_layer(x, mod4, P, i, j, state_ssm[:, j], npr, seq_p, seq_s)
            w_o = ssd_w_out[j].astype(BF16)
            ssms.append(st)
        x = _ffn(x, mod4, g_norm2, ffn_w_in, ffn_w_out, i, ffn_cidx, ffn_tm, 256, npr,
                 y=y, w_o=w_o, split=(i == depth - 1))
    return (x[0].reshape(nbp, seq_p, d), x[1].reshape(nbs, seq_s, d),
            jnp.stack(ckvs, axis=1), jnp.stack(kpes, axis=1), jnp.stack(ssms, axis=1))
```

```python
import functools

import jax
import jax.numpy as jnp
import numpy as np
from jax import lax
from jax.experimental import pallas as pl
from jax.experimental.pallas import tpu as pltpu

F32 = jnp.float32
BF16 = jnp.bfloat16

EPS = 1e-6
ROPE_BASE = 10000.0
GRID_W = 64
N_MIXERS = 3

LANES = 128
VMEM_LIMIT_BYTES = 56 * 1024 * 1024

MLA_HEADS = 16
QK_NOPE = 64
QK_ROPE = 32
QK_DIM = QK_NOPE + QK_ROPE
V_HEAD = 64
HEAD_PAD = LANES
CONV_WIDTH = 31
SSD_HEADDIM = 64
SSD_GROUPS = 4
SSD_STATE = 128
SSD_CONV = 5
CHUNK = 128


def _cparams(sem):
    return pltpu.CompilerParams(dimension_semantics=sem, vmem_limit_bytes=VMEM_LIMIT_BYTES)


def _silu(x):
    return x * jax.nn.sigmoid(x)


def _normmod(x, g, sh, sc):
    ms = jnp.mean(x * x, axis=-1, keepdims=True)
    return (x * lax.rsqrt(ms + EPS)) * (g * (1.0 + sc)) + sh


def _cond_index(tm, n_prompt_rows, sample_seq):
    def f(i):
        r = i * tm
        return jnp.where(r < n_prompt_rows, 0, 1 + (r - n_prompt_rows) // sample_seq)
    return f


def _mod_spec(layer, cidx, d6, ngrid):
    if ngrid == 1:
        return pl.BlockSpec((1, 1, 1, d6), lambda i: (layer, cidx(i), 0, 0))
    return pl.BlockSpec((1, 1, 1, d6), lambda i, j: (layer, cidx(i), 0, 0))


def _ada_kernel(c_ref, w_ref, b_ref, o_ref):
    s = _silu(c_ref[...]).astype(BF16)
    o_ref[0] = jnp.dot(s, w_ref[0].astype(BF16), preferred_element_type=F32) + b_ref[0]


def _ada(cond8, w_ada, b_ada):
    depth, d, n = w_ada.shape
    tn = n // 4
    return pl.pallas_call(
        _ada_kernel,
        grid=(depth, n // tn),
        in_specs=[pl.BlockSpec((8, d), lambda l, j: (0, 0)),
                  pl.BlockSpec((1, d, tn), lambda l, j: (l, 0, j)),
                  pl.BlockSpec((1, 1, tn), lambda l, j: (l, 0, j))],
        out_specs=pl.BlockSpec((1, 8, tn), lambda l, j: (l, 0, j)),
        out_shape=jax.ShapeDtypeStruct((depth, 8, n), F32),
        compiler_params=_cparams(("arbitrary", "arbitrary")),
        name="ada",
    )(cond8, w_ada, b_ada.reshape(depth, 1, n))


FFN_SLOTS = 3


def _ffn_kernel(*refs, d, n_in_a, n_out_a, fused, layer, hid, th, ntiles):
    refs = list(refs)
    xa_ref = refs.pop(0)
    xb_ref = refs.pop(0) if n_in_a is not None else None
    y_ref, wo1_ref = (refs.pop(0), refs.pop(0)) if fused else (None, None)
    mod_ref, g_ref, win_hbm, wout_hbm = refs[:4]
    if n_out_a is None:
        o_ref, h_sc, wa_buf, wu_buf, wo_buf, sem = refs[4:]
        acc = o_ref
    else:
        oa_ref, ob_ref, h_sc, acc, wa_buf, wu_buf, wo_buf, sem = refs[4:]
    i = pl.program_id(0)
    nj = hid // th
    base = lax.rem(i * (nj % FFN_SLOTS), FFN_SLOTS)

    def slot_of(j):
        return lax.rem(base + j % FFN_SLOTS, FFN_SLOTS)

    def chunk_copies(j):
        c, s = j % nj, slot_of(j)
        return (pltpu.make_async_copy(win_hbm.at[layer, :, pl.ds(c * th, th)], wa_buf.at[s], sem.at[0, s]),
                pltpu.make_async_copy(win_hbm.at[layer, :, pl.ds(hid + c * th, th)], wu_buf.at[s], sem.at[1, s]),
                pltpu.make_async_copy(wout_hbm.at[layer, pl.ds(c * th, th), :], wo_buf.at[s], sem.at[2, s]))

    def start(j):
        for cp in chunk_copies(j):
            cp.start()

    def wait(j):
        for cp in chunk_copies(j):
            cp.wait()

    @pl.when(i == 0)
    def _():
        start(0)
        start(1)

    wait(0)
    wa0, wu0 = wa_buf[base].astype(BF16), wu_buf[base].astype(BF16)
    tm = acc.shape[0]
    rc = min(tm, 256)
    a_rows, u_rows = [], []
    for r0 in range(0, tm, rc):
        rows = slice(r0, r0 + rc)
        x = xa_ref[rows, :] if xb_ref is None else jnp.where(i < n_in_a, xa_ref[rows, :], xb_ref[rows, :])
        if fused:
            x = x + mod_ref[0, 0, :, 2 * d:3 * d] * jnp.dot(y_ref[rows, :], wo1_ref[...],
                                                             preferred_element_type=F32)
        acc[rows, :] = x
        hr = _normmod(x, g_ref[0], mod_ref[0, 0, :, 3 * d:4 * d], mod_ref[0, 0, :, 4 * d:5 * d]).astype(BF16)
        h_sc[rows, :] = hr
        a_rows.append(jnp.dot(hr, wa0, preferred_element_type=F32))
        u_rows.append(jnp.dot(hr, wu0, preferred_element_type=F32))
    au = (jnp.concatenate(a_rows, axis=0), jnp.concatenate(u_rows, axis=0))

    gate2 = mod_ref[0, 0, :, 5 * d:6 * d]
    h = h_sc[...]

    def up(j):
        s = slot_of(j)
        return (jnp.dot(h, wa_buf[s].astype(BF16), preferred_element_type=F32),
                jnp.dot(h, wu_buf[s].astype(BF16), preferred_element_type=F32))

    for j in range(nj):
        if j + 1 < nj:
            wait(j + 1)
        if j + 2 < nj:
            start(j + 2)
        else:
            @pl.when(i + 1 < ntiles)
            def _(j=j):
                start(j + 2)
        a, u = au
        t = (_silu(a) * u).astype(BF16)
        if j + 1 < nj:
            au = up(j + 1)
        acc[...] += gate2 * jnp.dot(t, wo_buf[slot_of(j)].astype(BF16), preferred_element_type=F32)

    if n_out_a is not None:
        @pl.when(i < n_out_a)
        def _():
            oa_ref[...] = acc[...]

        @pl.when(i >= n_out_a)
        def _():
            ob_ref[...] = acc[...]


def _ffn(x, mod4, g2, w_in, w_out, layer, cidx, tm, th, npr, y=None, w_o=None, split=False):
    pair = isinstance(x, tuple)
    d = x[0].shape[1] if pair else x.shape[1]
    m = (x[0].shape[0] + x[1].shape[0]) if pair else x.shape[0]
    hid = w_out.shape[1]
    nj = hid // th
    n_a = npr // tm
    fused = y is not None
    if pair:
        in_specs = [pl.BlockSpec((tm, d), lambda i: (jnp.minimum(i, n_a - 1), 0)),
                    pl.BlockSpec((tm, d), lambda i: (jnp.maximum(i - n_a, 0), 0))]
        args = list(x)
    else:
        in_specs = [pl.BlockSpec((tm, d), lambda i: (i, 0))]
        args = [x]
    if fused:
        k = y.shape[1]
        in_specs += [pl.BlockSpec((tm, k), lambda i: (i, 0)), pl.BlockSpec((k, d), lambda i: (0, 0))]
        args += [y, w_o]
    in_specs += [_mod_spec(layer, cidx, 6 * d, 1),
                 pl.BlockSpec((1, 1, d), lambda i: (layer, 0, 0)),
                 pl.BlockSpec(memory_space=pl.ANY),
                 pl.BlockSpec(memory_space=pl.ANY)]
    args += [mod4, g2.reshape(g2.shape[0], 1, d), w_in, w_out]
    scratch = [pltpu.VMEM((tm, d), BF16)]
    if split:
        out_specs = [pl.BlockSpec((tm, d), lambda i: (jnp.minimum(i, n_a - 1), 0)),
                     pl.BlockSpec((tm, d), lambda i: (jnp.maximum(i - n_a, 0), 0))]
        out_shape = [jax.ShapeDtypeStruct((npr, d), F32), jax.ShapeDtypeStruct((m - npr, d), F32)]
        scratch.append(pltpu.VMEM((tm, d), F32))
    else:
        out_specs = pl.BlockSpec((tm, d), lambda i: (i, 0))
        out_shape = jax.ShapeDtypeStruct((m, d), F32)
    scratch += [pltpu.VMEM((FFN_SLOTS, d, th), F32), pltpu.VMEM((FFN_SLOTS, d, th), F32),
                pltpu.VMEM((FFN_SLOTS, th, d), F32), pltpu.SemaphoreType.DMA((3, FFN_SLOTS))]
    assert nj >= 2 and hid % th == 0
    return pl.pallas_call(
        functools.partial(_ffn_kernel, d=d, n_in_a=n_a if pair else None, n_out_a=n_a if split else None,
                          fused=fused, layer=layer, hid=hid, th=th, ntiles=m // tm),
        grid=(m // tm,),
        in_specs=in_specs,
        out_specs=out_specs,
        out_shape=out_shape,
        scratch_shapes=scratch,
        compiler_params=_cparams(("arbitrary",)),
        name="ffn_proj" if fused else "ffn",
    )(*args)


def _conv_a_kernel(x_ref, mod_ref, g_ref, w_ref, b_ref, o_ref, w_sc, *, d):
    @pl.when(pl.program_id(0) == 0)
    def _():
        w_sc[...] = w_ref[0].astype(BF16)

    h = _normmod(x_ref[...], g_ref[0], mod_ref[0, 0, :, 0:d], mod_ref[0, 0, :, d:2 * d]).astype(BF16)
    y = jnp.dot(h, w_sc[...], preferred_element_type=F32) + b_ref[0]
    u = y[:, :d] * jax.nn.sigmoid(y[:, d:])
    for cb in range(d // LANES):
        o_ref[cb] = u[:, cb * LANES:(cb + 1) * LANES]


def _conv_b_kernel(*refs, d, t, rc, top):
    (x_ref, u_ref, mod_ref, wdw_ref, bdw_ref, gln_ref, bln_ref, w2_ref, b2_ref) = refs[:9]
    o_ref, upad, conv_sc, y_sc, w_sc = refs[-5:]
    nb = d // LANES
    half = (CONV_WIDTH - 1) // 2

    @pl.when(pl.program_id(0) == 0)
    def _():
        w_sc[...] = w2_ref[0].astype(BF16)
        upad[:, 0:top, :] = jnp.zeros((nb, top, LANES), F32)
        upad[:, top + t:2 * top + t, :] = jnp.zeros((nb, top, LANES), F32)

    upad[:, top:top + t, :] = u_ref[...]

    def conv_block(cb, carry):
        for r0 in range(0, t, rc):
            acc = jnp.broadcast_to(bdw_ref[cb], (rc, LANES))
            for k in range(CONV_WIDTH):
                acc = acc + wdw_ref[cb, pl.ds(k, 1), :] * upad[cb, pl.ds(r0 + top - half + k, rc), :]
            conv_sc[cb, pl.ds(r0, rc), :] = acc
        return carry

    lax.fori_loop(0, nb, conv_block, 0)

    rl = min(t, 256)
    for r0 in range(0, t, rl):
        s = jnp.zeros((rl, 1), F32)
        for cb in range(nb):
            s = s + jnp.sum(conv_sc[cb, r0:r0 + rl, :], axis=-1, keepdims=True)
        mean = s * (1.0 / d)
        v = jnp.zeros((rl, 1), F32)
        for cb in range(nb):
            c = conv_sc[cb, r0:r0 + rl, :] - mean
            v = v + jnp.sum(c * c, axis=-1, keepdims=True)
        rstd = lax.rsqrt(v * (1.0 / d) + EPS)
        for cb in range(nb):
            sl = slice(cb * LANES, (cb + 1) * LANES)
            y = (conv_sc[cb, r0:r0 + rl, :] - mean) * rstd * gln_ref[:, sl] + bln_ref[:, sl]
            y_sc[r0:r0 + rl, sl] = _silu(y).astype(BF16)

    gate = mod_ref[0, 0, :, 2 * d:3 * d]
    r = jnp.dot(y_sc[...], w_sc[...], preferred_element_type=F32) + b2_ref[0]
    o_ref[...] = x_ref[...] + gate * r


def _conv_b(x, u, mod4, wdw, bdw, gln, bln, w2, b2, layer, j, t, row0, nseq, cond0, cond_step, obuf):
    m, d = x.shape
    nb = d // LANES
    top = 16
    blk0 = row0 // t
    in_specs = [pl.BlockSpec((t, d), lambda i: (blk0 + i, 0)),
                pl.BlockSpec((nb, t, LANES), lambda i: (0, blk0 + i, 0)),
                pl.BlockSpec((1, 1, 1, 6 * d), lambda i: (layer, cond0 + cond_step * i, 0, 0)),
                pl.BlockSpec((nb, 32, LANES), lambda i: (0, 0, 0)),
                pl.BlockSpec((nb, 1, LANES), lambda i: (0, 0, 0)),
                pl.BlockSpec((1, d), lambda i: (0, 0)),
                pl.BlockSpec((1, d), lambda i: (0, 0)),
                pl.BlockSpec((1, d, d), lambda i: (j, 0, 0)),
                pl.BlockSpec((1, 1, d), lambda i: (j, 0, 0))]
    args = [x, u, mod4, wdw, bdw, gln, bln, w2, b2]
    aliases = {}
    if obuf is not None:
        in_specs.append(pl.BlockSpec(memory_space=pl.ANY))
        args.append(obuf)
        aliases = {len(args) - 1: 0}
    return pl.pallas_call(
        functools.partial(_conv_b_kernel, d=d, t=t, rc=64, top=top),
        grid=(nseq,),
        in_specs=in_specs,
        out_specs=pl.BlockSpec((t, d), lambda i: (blk0 + i, 0)),
        out_shape=jax.ShapeDtypeStruct((m, d), F32),
        input_output_aliases=aliases,
        scratch_shapes=[pltpu.VMEM((nb, t + 2 * top, LANES), F32),
                        pltpu.VMEM((nb, t, LANES), F32),
                        pltpu.VMEM((t, d), BF16),
                        pltpu.VMEM((d, d), BF16)],
        compiler_params=_cparams(("arbitrary",)),
        name="conv_b",
    )(*args)


def _conv_layer(x, mod4, P, layer, j, npr, seq_p, seq_s):
    m, d = x.shape
    nb = d // LANES
    tm = min(1024, seq_s)
    cidx = _cond_index(tm, npr, seq_s)
    u = pl.pallas_call(
        functools.partial(_conv_a_kernel, d=d),
        grid=(m // tm,),
        in_specs=[pl.BlockSpec((tm, d), lambda i: (i, 0)),
                  _mod_spec(layer, cidx, 6 * d, 1),
                  pl.BlockSpec((1, 1, d), lambda i: (layer, 0, 0)),
                  pl.BlockSpec((1, d, 2 * d), lambda i: (j, 0, 0)),
                  pl.BlockSpec((1, 1, 2 * d), lambda i: (j, 0, 0))],
        out_specs=pl.BlockSpec((nb, tm, LANES), lambda i: (0, i, 0)),
        out_shape=jax.ShapeDtypeStruct((nb, m, LANES), F32),
        scratch_shapes=[pltpu.VMEM((d, 2 * d), BF16)],
        compiler_params=_cparams(("arbitrary",)),
        name="conv_a",
    )(x, mod4, P["g_norm1"].reshape(-1, 1, d), P["cv_w_pw1"], P["cv_b_pw1"].reshape(-1, 1, 2 * d))
    wdw = jnp.pad(P["cv_w_dw"][j], ((0, 32 - CONV_WIDTH), (0, 0))).reshape(32, nb, LANES).transpose(1, 0, 2)
    bdw = P["cv_b_dw"][j].reshape(nb, 1, LANES)
    gln = P["cv_g_ln"][j].reshape(1, d)
    bln = P["cv_b_ln"][j].reshape(1, d)
    b2 = P["cv_b_pw2"].reshape(-1, 1, d)
    args = (x, u, mod4, wdw, bdw, gln, bln, P["cv_w_pw2"], b2, layer, j)
    op = _conv_b(*args, seq_p, 0, npr // seq_p, 0, 0, None)
    return _conv_b(*args, seq_s, npr, (m - npr) // seq_s, 1, 1, op)


_LOG2E = 1.4426950408889634
_ROT = QK_ROPE // 4


def _swap_index():
    idx = list(range(HEAD_PAD))
    for base in (QK_NOPE, QK_NOPE + 2 * _ROT):
        for l in range(_ROT):
            idx[base + l], idx[base + _ROT + l] = base + _ROT + l, base + l
    return jnp.asarray(idx, jnp.int32)


def _swap_cols(w):
    lane = jnp.arange(HEAD_PAD)
    rot = (lane >= QK_NOPE) & (lane < QK_DIM)
    return jnp.where(rot, jnp.take(w, _swap_index(), axis=-1), 0.0)


def _rope_tables(seq, gqn, gkn):
    rows = seq // GRID_W
    row = jnp.repeat(jnp.arange(rows, dtype=F32), GRID_W)
    col = jnp.tile(jnp.arange(GRID_W, dtype=F32), rows)
    axis = QK_ROPE // 2
    inv = ROPE_BASE ** (-jnp.arange(0, axis, 2, dtype=F32) / axis)
    ar, ac = row[:, None] * inv, col[:, None] * inv
    cr, sr, cc, sc = jnp.cos(ar), jnp.sin(ar), jnp.cos(ac), jnp.sin(ac)
    one = jnp.ones((seq, QK_NOPE), F32)
    zn = jnp.zeros((seq, QK_NOPE), F32)
    zp = jnp.zeros((seq, HEAD_PAD - QK_DIM), F32)
    c = jnp.concatenate([one, cr, cr, cc, cc, zp], axis=1)
    s = jnp.concatenate([zn, -sr, sr, -sc, sc, zp], axis=1)
    idx = _swap_index()
    return tuple(t for g in (gqn, gkn) for t in (c[None] * g, s[None] * jnp.take(g, idx, axis=-1)))


def _head_rstd(xh):
    ms = jnp.sum(xh * xh, axis=-1, keepdims=True) * (1.0 / QK_DIM)
    return lax.rsqrt(ms + EPS)


def _expand_kv(ckv_b, kpe, wuk_ref, wv_ref, gk, rot, k_ref, v_ref, v_transposed):
    kk = jnp.dot(ckv_b, wuk_ref[0], preferred_element_type=F32)
    if v_transposed:
        v = lax.dot_general(wv_ref[0], ckv_b, _NT, preferred_element_type=F32)
    else:
        v = jnp.dot(ckv_b, wv_ref[0], preferred_element_type=F32)
    v_ref[...] = v.astype(BF16)
    for h in range(MLA_HEADS):
        sl = slice(h * HEAD_PAD, (h + 1) * HEAD_PAD)
        kh = kk[:, sl] + kpe
        r = _head_rstd(kh)
        kh = (kh * gk) * r if rot is None else (kh * rot[0] + rot[1]) * r
        k_ref[:, sl] = kh.astype(BF16)


def _mla_proj_kernel(*refs, d, rope):
    (x_ref, mod_ref, g_ref, wcat_ref, gq_ref, wuq_ref, gkv_ref, wuk_ref, wv_ref, gqn_ref, gkn_ref) = refs[:11]
    if rope:
        cq_ref, sq_ref, ck_ref, sk_ref = refs[11:15]
        q_ref, k_ref, v_ref = refs[15:18]
    else:
        q_ref, k_ref, v_ref, ckv_ref, kpe_ref = refs[11:16]
    h = _normmod(x_ref[...], g_ref[0], mod_ref[0, 0, :, 0:d], mod_ref[0, 0, :, d:2 * d]).astype(BF16)
    t = jnp.dot(h, wcat_ref[0], preferred_element_type=F32)
    nq = gq_ref.shape[2]
    nkv = gkv_ref.shape[2]
    hq = MLA_HEADS * HEAD_PAD
    qd, craw, kpe = t[:, :nq], t[:, nq:nq + nkv], t[:, nq + nkv:nq + nkv + HEAD_PAD]
    cq = (qd * lax.rsqrt(jnp.mean(qd * qd, axis=-1, keepdims=True) + EPS) * gq_ref[0]).astype(BF16)
    ckv = craw * lax.rsqrt(jnp.mean(craw * craw, axis=-1, keepdims=True) + EPS) * gkv_ref[0]
    if not rope:
        ckv_ref[...] = ckv
        kpe_ref[...] = kpe
    q = jnp.dot(cq, wuq_ref[0], preferred_element_type=F32)
    if rope:
        ct, st = cq_ref[0], sq_ref[0]
    else:
        gqn = gqn_ref[0]
    for hd in range(MLA_HEADS):
        sl = slice(hd * HEAD_PAD, (hd + 1) * HEAD_PAD)
        qh = q[:, sl]
        r = _head_rstd(qh)
        qh = (qh * ct + q[:, hq + hd * HEAD_PAD:hq + (hd + 1) * HEAD_PAD] * st) * r if rope else (qh * gqn) * r
        q_ref[:, sl] = qh.astype(BF16)
    rot = None
    if rope:
        kpe2 = t[:, nq + nkv + HEAD_PAD:]
        rot = (ck_ref[0], kpe2 * sk_ref[0])
    _expand_kv(ckv.astype(BF16), kpe, wuk_ref, wv_ref, gkn_ref[0], rot, k_ref, v_ref, rope)


def _kv_ctx_kernel(ckv_ref, kpe_ref, wuk_ref, wv_ref, gkn_ref, k_ref, v_ref):
    _expand_kv(ckv_ref[...].astype(BF16), kpe_ref[...], wuk_ref, wv_ref, gkn_ref[0], None, k_ref, v_ref, True)


_NT = (((1,), (1,)), ((), ()))


def _attn_prompt_kernel(q_ref, k_ref, v_ref, x_ref, mod_ref, wo_ref, o_ref, a_sc, *, d):
    t = q_ref.shape[0]
    lane = lax.broadcasted_iota(jnp.int32, (t, LANES), 1)
    for hp in range(MLA_HEADS // 2):
        vp = v_ref[:, hp * LANES:(hp + 1) * LANES]
        outs = []
        for hh in range(2):
            sl = slice((2 * hp + hh) * HEAD_PAD, (2 * hp + hh + 1) * HEAD_PAD)
            s = lax.dot_general(q_ref[:, sl], k_ref[:, sl], _NT, preferred_element_type=F32)
            e = jnp.exp2(s - s.max(axis=-1, keepdims=True))
            l = jnp.sum(e, axis=-1, keepdims=True)
            outs.append(jnp.dot(e.astype(BF16), vp, preferred_element_type=F32) * (1.0 / l))
        a_sc[:, hp * LANES:(hp + 1) * LANES] = jnp.where(lane < V_HEAD, outs[0], outs[1]).astype(BF16)
    gate = mod_ref[0, 0, :, 2 * d:3 * d]
    o_ref[...] = x_ref[...] + gate * jnp.dot(a_sc[...], wo_ref[0], preferred_element_type=F32)


def _attn_sample_kernel(q_ref, kc_ref, vtc_ref, kl_ref, vtl_ref, x_ref, mod_ref, wo_ref, obuf_ref, o_ref, *, d):
    del obuf_ref
    nh = q_ref.shape[1] // HEAD_PAD
    row = lax.broadcasted_iota(jnp.int32, (LANES, 1), 0)

    def scores(hh):
        sl = slice(hh * HEAD_PAD, (hh + 1) * HEAD_PAD)
        qh = q_ref[:, sl]
        return [lax.dot_general(k_ref[:, sl], qh, _NT, preferred_element_type=F32)
                for k_ref in (kc_ref, kl_ref)]

    halves = []
    nxt = scores(0)
    for hh in range(nh):
        sts = nxt
        if hh + 1 < nh:
            nxt = scores(hh + 1)
        m = jnp.max(sts[0], axis=0, keepdims=True)
        for st in sts[1:]:
            m = jnp.maximum(m, jnp.max(st, axis=0, keepdims=True))
        pr = slice((hh // 2) * LANES, (hh // 2 + 1) * LANES)
        own = (row < V_HEAD) if hh % 2 == 0 else (row >= V_HEAD)
        acc = None
        for st, vt_ref in zip(sts, (vtc_ref, vtl_ref)):
            e = jnp.exp2(st - m).astype(BF16)
            vt = vt_ref[pr, :]
            lhs = jnp.where(own, vt, jnp.ones_like(vt))
            pv = jnp.dot(lhs, e, preferred_element_type=F32)
            acc = pv if acc is None else acc + pv
        if hh % 2 == 0:
            halves.append(acc[0:V_HEAD] * (1.0 / acc[V_HEAD:V_HEAD + 1]))
        else:
            halves.append(acc[V_HEAD:2 * V_HEAD] * (1.0 / acc[0:1]))
    attn = jnp.concatenate(halves, axis=0).T.astype(BF16)
    gate = mod_ref[0, 0, :, 2 * d:3 * d]
    o_ref[...] = x_ref[...] + gate * jnp.dot(attn, wo_ref[0], preferred_element_type=F32)


def _mla_weights(P, seq_s):
    n = P["mla_w_dq"].shape[0]
    nq = P["mla_w_dq"].shape[2]
    nkv = P["mla_g_kv"].shape[1]
    pad = HEAD_PAD - QK_DIM
    w_dkv = P["mla_w_dkv"]
    w_kpe = jnp.pad(w_dkv[:, :, nkv:], ((0, 0), (0, 0), (QK_NOPE, pad)))
    wcat = jnp.concatenate([P["mla_w_dq"], w_dkv[:, :, :nkv], w_kpe], axis=2).astype(BF16)
    wuq = jnp.pad(P["mla_w_uq"].reshape(n, nq, MLA_HEADS, QK_DIM), ((0, 0), (0, 0), (0, 0), (0, pad)))
    wuq = wuq.reshape(n, nq, MLA_HEADS * HEAD_PAD).astype(BF16)
    wukv = P["mla_w_ukv"].reshape(n, nkv, MLA_HEADS, QK_NOPE + V_HEAD)
    wuk = jnp.pad(wukv[..., :QK_NOPE], ((0, 0), (0, 0), (0, 0), (0, HEAD_PAD - QK_NOPE)))
    wuk = wuk.reshape(n, nkv, MLA_HEADS * HEAD_PAD).astype(BF16)
    wv = wukv[..., QK_NOPE:].reshape(n, nkv, MLA_HEADS * V_HEAD).astype(BF16)
    gq = P["mla_g_q"].reshape(n, 1, nq)
    gkv = P["mla_g_kv"].reshape(n, 1, nkv)
    gqn = (jnp.pad(P["mla_g_qn"], ((0, 0), (0, pad))) * (QK_DIM ** -0.5 * _LOG2E)).reshape(n, 1, HEAD_PAD)
    gkn = jnp.pad(P["mla_g_kn"], ((0, 0), (0, pad))).reshape(n, 1, HEAD_PAD)
    wuq2 = _swap_cols(wuq.reshape(n, nq, MLA_HEADS, HEAD_PAD)).reshape(n, nq, MLA_HEADS * HEAD_PAD)
    wuq_r = jnp.concatenate([wuq, wuq2], axis=2)
    wcat_r = jnp.concatenate([wcat, _swap_cols(w_kpe).astype(BF16)], axis=2)
    return dict(plain=(wcat, gq, wuq, gkv, wuk, wv, gqn, gkn), rope=(wcat_r, wuq_r, jnp.swapaxes(wv, 1, 2)),
                wo=P["mla_w_o"].astype(BF16), tabs=_rope_tables(seq_s, gqn, gkn))


def _layer_spec(a, j):
    return pl.BlockSpec((1,) + a.shape[1:], lambda *_: (j,) + (0,) * (a.ndim - 1))


def _mla_proj(x, xblk0, mod4, g1, W, layer, j, row0, nrows, cidx, tm, rope_args):
    d = x.shape[1]
    wcat, gq, wuq, gkv, wuk, wv, gqn, gkn = W
    blk0 = row0 // tm
    hq = MLA_HEADS * HEAD_PAD
    hv = MLA_HEADS * V_HEAD
    nkv = gkv.shape[2]
    rope = rope_args is not None
    args = [x, mod4, g1, wcat, gq, wuq, gkv, wuk, wv, gqn, gkn]
    if rope:
        wcat_r, wuq_r, wvt, tabs = rope_args
        args[3], args[5], args[8] = wcat_r, wuq_r, wvt
    in_specs = [pl.BlockSpec((tm, d), lambda i: (xblk0 + i, 0)),
                pl.BlockSpec((1, 1, 1, 6 * d), lambda i: (layer, cidx(blk0 + i), 0, 0)),
                pl.BlockSpec((1, 1, d), lambda i: (layer, 0, 0))] + [_layer_spec(a, j) for a in args[3:]]
    out_specs = [pl.BlockSpec((tm, hq), lambda i: (i, 0)),
                 pl.BlockSpec((tm, hq), lambda i: (i, 0))]
    out_shape = [jax.ShapeDtypeStruct((nrows, hq), BF16),
                 jax.ShapeDtypeStruct((nrows, hq), BF16)]
    if rope:
        per = tabs[0].shape[1] // tm
        in_specs += [pl.BlockSpec((1, tm, HEAD_PAD), lambda i: (j, i % per, 0))] * 4
        args += list(tabs)
        out_specs += [pl.BlockSpec((hv, tm), lambda i: (0, i))]
        out_shape += [jax.ShapeDtypeStruct((hv, nrows), BF16)]
    else:
        out_specs += [pl.BlockSpec((tm, hv), lambda i: (i, 0)),
                      pl.BlockSpec((tm, nkv), lambda i: (i, 0)), pl.BlockSpec((tm, HEAD_PAD), lambda i: (i, 0))]
        out_shape += [jax.ShapeDtypeStruct((nrows, hv), BF16),
                      jax.ShapeDtypeStruct((nrows, nkv), F32), jax.ShapeDtypeStruct((nrows, HEAD_PAD), F32)]
    return pl.pallas_call(
        functools.partial(_mla_proj_kernel, d=d, rope=rope),
        grid=(nrows // tm,),
        in_specs=in_specs, out_specs=out_specs, out_shape=out_shape,
        compiler_params=_cparams(("arbitrary",)),
        name="mla_proj_rope" if rope else "mla_proj",
    )(*args)


def _mla_layer(x, mod4, P, MW, layer, j, cache_ckv, cache_kpe, npr, seq_p, seq_s):
    xa, xb = x if isinstance(x, tuple) else (x, x)
    d = xa.shape[1]
    nrs = xb.shape[0] if isinstance(x, tuple) else xb.shape[0] - npr
    m = npr + nrs
    nbp, nbs = npr // seq_p, nrs // seq_s
    past = cache_ckv.shape[1]
    W, (wcat_r, wuq_r, wvt), wo, tabs = MW["plain"], MW["rope"], MW["wo"], MW["tabs"]
    wuk, gkn = W[4], W[7]
    g1 = P["g_norm1"].reshape(-1, 1, d)
    hq = MLA_HEADS * HEAD_PAD
    hv = MLA_HEADS * V_HEAD
    tm = min(512, seq_s)
    cidx = _cond_index(tm, npr, seq_s)
    xb0 = 0 if isinstance(x, tuple) else npr // tm
    qp, kp, vp, ckv, kpe128 = _mla_proj(xa, 0, mod4, g1, W, layer, j, 0, npr, cidx, tm, None)
    qs, ks, vts = _mla_proj(xb, xb0, mod4, g1, W, layer, j, npr, nrs, cidx, tm, (wcat_r, wuq_r, wvt, tabs))
    cc = cache_ckv.reshape(nbs * past, -1)
    ck = jnp.pad(cache_kpe.reshape(nbs * past, -1), ((0, 0), (QK_NOPE, HEAD_PAD - QK_DIM)))
    kc, vtc = pl.pallas_call(
        _kv_ctx_kernel,
        grid=(nbs,),
        in_specs=[pl.BlockSpec((past, cc.shape[1]), lambda i: (i, 0)),
                  pl.BlockSpec((past, HEAD_PAD), lambda i: (i, 0)),
                  _layer_spec(wuk, j), _layer_spec(wvt, j), _layer_spec(gkn, j)],
        out_specs=[pl.BlockSpec((past, hq), lambda i: (i, 0)), pl.BlockSpec((hv, past), lambda i: (0, i))],
        out_shape=[jax.ShapeDtypeStruct((nbs * past, hq), BF16), jax.ShapeDtypeStruct((hv, nbs * past), BF16)],
        compiler_params=_cparams(("arbitrary",)),
        name="mla_kv_ctx",
    )(cc, ck, wuk, wvt, gkn)
    xp0 = 0
    xs0 = 0 if isinstance(x, tuple) else npr
    xn_p = pl.pallas_call(
        functools.partial(_attn_prompt_kernel, d=d),
        grid=(nbp,),
        in_specs=[pl.BlockSpec((seq_p, hq), lambda b: (b, 0)),
                  pl.BlockSpec((seq_p, hq), lambda b: (b, 0)),
                  pl.BlockSpec((seq_p, hv), lambda b: (b, 0)),
                  pl.BlockSpec((seq_p, d), lambda b: (xp0 // seq_p + b, 0)),
                  pl.BlockSpec((1, 1, 1, 6 * d), lambda b: (layer, 0, 0, 0)),
                  _layer_spec(wo, j)],
        out_specs=pl.BlockSpec((seq_p, d), lambda b: (b, 0)),
        out_shape=jax.ShapeDtypeStruct((m, d), F32),
        scratch_shapes=[pltpu.VMEM((seq_p, hv), BF16)],
        compiler_params=_cparams(("arbitrary",)),
        name="attn_prompt",
    )(qp, kp, vp, xa, mod4, wo)
    tq = min(seq_s, 512)
    nqt = seq_s // tq
    o0 = npr // tq
    xn = pl.pallas_call(
        functools.partial(_attn_sample_kernel, d=d),
        grid=(nbs, nqt),
        in_specs=[pl.BlockSpec((tq, hq), lambda b, qi: (b * nqt + qi, 0)),
                  pl.BlockSpec((past, hq), lambda b, qi: (b, 0)),
                  pl.BlockSpec((hv, past), lambda b, qi: (0, b)),
                  pl.BlockSpec((seq_s, hq), lambda b, qi: (b, 0)),
                  pl.BlockSpec((hv, seq_s), lambda b, qi: (0, b)),
                  pl.BlockSpec((tq, d), lambda b, qi: (xs0 // tq + b * nqt + qi, 0)),
                  pl.BlockSpec((1, 1, 1, 6 * d), lambda b, qi: (layer, 1 + b, 0, 0)),
                  _layer_spec(wo, j),
                  pl.BlockSpec(memory_space=pl.ANY)],
        out_specs=pl.BlockSpec((tq, d), lambda b, qi: (o0 + b * nqt + qi, 0)),
        out_shape=jax.ShapeDtypeStruct((m, d), F32),
        input_output_aliases={8: 0},
        compiler_params=_cparams(("arbitrary", "arbitrary")),
        name="attn_sample",
    )(qs, kc, vtc, ks, vts, xb, mod4, wo, xn_p)
    new_ckv = ckv.reshape(nbp, seq_p, -1)
    new_kpe = kpe128[:, QK_NOPE:QK_DIM].reshape(nbp, seq_p, QK_ROPE)
    return xn, new_ckv, new_kpe


SSD_R = 8
SSD_GW = SSD_R * SSD_HEADDIM
SSD_NX = SSD_GW // LANES
SSD_DTW = 2 * SSD_R
SSD_TOP = 8


def _ssd_in_kernel(x_ref, mod_ref, g_ref, w_ref, wdt_ref, zx_ref, dt_ref, h_sc, *, d, heads):
    i = pl.program_id(1)
    tm = x_ref.shape[0]
    rows = pl.ds(pl.multiple_of(i * tm, tm), tm)

    @pl.when(pl.program_id(0) == 0)
    def _():
        h = _normmod(x_ref[...], g_ref[0], mod_ref[0, 0, :, 0:d], mod_ref[0, 0, :, d:2 * d]).astype(BF16)
        h_sc[rows, :] = h
        row = lax.broadcasted_iota(jnp.int32, wdt_ref.shape[1:], 0)
        wdt = jnp.where(row < 2 * heads, wdt_ref[0], 0.0).astype(BF16)
        dtt = lax.dot_general(wdt, h, _NT, preferred_element_type=F32)
        for g in range(SSD_GROUPS):
            dt_ref[g, 0:SSD_R, :] = dtt[g * SSD_R:(g + 1) * SSD_R, :]
            dt_ref[g, SSD_R:SSD_DTW, :] = dtt[heads + g * SSD_R:heads + (g + 1) * SSD_R, :]

    r = lax.dot_general(h_sc[rows, :], w_ref[0].astype(BF16), _NT, preferred_element_type=F32)
    for c in range(r.shape[1] // LANES):
        zx_ref[c] = r[:, c * LANES:(c + 1) * LANES]


def _split3(x):
    hi = x.astype(BF16)
    r1 = x - hi.astype(F32)
    mid = r1.astype(BF16)
    lo = (r1 - mid.astype(F32)).astype(BF16)
    return hi, mid, lo


def _dot3_right(x, tri):
    return sum(jnp.dot(p, tri, preferred_element_type=F32) for p in _split3(x))


SSD_NPIECE = 7


def _ssd_spread_matrix():
    sel = np.zeros((2, CHUNK, SSD_R * LANES + 2 * SSD_GW), np.float32)
    for dr in range(2):
        for r in range(SSD_R):
            h = dr * SSD_R + r
            for p in range(3):
                sel[dr, p * SSD_DTW + h, r * LANES:(r + 1) * LANES] = 1.0
            for p in range(3, 5):
                sel[dr, p * SSD_DTW + h, SSD_R * LANES + r * SSD_HEADDIM:SSD_R * LANES + (r + 1) * SSD_HEADDIM] = 1.0
            for p in range(5, 7):
                c0 = SSD_R * LANES + SSD_GW
                sel[dr, p * SSD_DTW + h, c0 + r * SSD_HEADDIM:c0 + (r + 1) * SSD_HEADDIM] = 1.0
    return jnp.asarray(sel, BF16)


def _ssd_core_kernel(*refs, t, has_h0):
    (z_ref, x_ref, b_ref, c_ref, dt_ref, wx_ref, wb_ref, wc_ref, bx_ref, bb_ref, bc_ref,
     dtb_ref, alog_ref, dsum_ref, gn_ref, bsel_ref) = refs[:16]
    if has_h0:
        h0_ref, y_ref = refs[16], refs[18]
        st_ref = None
        scr = refs[19:]
    else:
        y_ref, st_ref = refs[16:18]
        h0_ref = None
        scr = refs[18:]
    (pad_sc, xc_sc, bcv_sc, ccv_sc, bt_sc, cb_sc, y_sc, dtt_sc, dat_sc, s_sc) = scr
    nc = t // CHUNK
    half = (SSD_CONV - 1) // 2
    nblk = SSD_NX + 2

    @pl.when((pl.program_id(0) == 0) & (pl.program_id(1) == 0))
    def _():
        pad_sc[:, 0:SSD_TOP, :] = jnp.zeros((nblk, SSD_TOP, LANES), F32)
        pad_sc[:, SSD_TOP + t:2 * SSD_TOP + t, :] = jnp.zeros((nblk, SSD_TOP, LANES), F32)

    pad_sc[0:SSD_NX, SSD_TOP:SSD_TOP + t, :] = x_ref[...]
    pad_sc[SSD_NX, SSD_TOP:SSD_TOP + t, :] = b_ref[0]
    pad_sc[SSD_NX + 1, SSD_TOP:SSD_TOP + t, :] = c_ref[0]

    def conv_chunk(c):
        r0 = c * CHUNK
        for blk in range(nblk):
            if blk < SSD_NX:
                w_r, bias, widx = wx_ref, bx_ref[blk], blk
            elif blk == SSD_NX:
                w_r, bias, widx = wb_ref, bb_ref[0], 0
            else:
                w_r, bias, widx = wc_ref, bc_ref[0], 0
            acc = jnp.broadcast_to(bias, (CHUNK, LANES))
            for k in range(SSD_CONV):
                acc = acc + w_r[widx, k:k + 1, :] * pad_sc[blk, r0 + SSD_TOP - half + k:r0 + SSD_TOP - half + k + CHUNK, :]
            acc = _silu(acc)
            if blk < SSD_NX:
                xc_sc[blk, r0:r0 + CHUNK, :] = acc
            elif blk == SSD_NX:
                bcv_sc[r0:r0 + CHUNK, :] = acc.astype(BF16)
                bt_sc[c] = acc.T.astype(BF16)
            else:
                ccv_sc[r0:r0 + CHUNK, :] = acc.astype(BF16)

    def gate_chunk(c):
        rows = slice(c * CHUNK, (c + 1) * CHUNK)
        yz = [y_sc[q, rows, :] * _silu(z_ref[q, rows, :]) for q in range(SSD_NX)]
        ms = sum(jnp.sum(a * a, axis=-1, keepdims=True) for a in yz) * (1.0 / SSD_GW)
        rstd = lax.rsqrt(ms + EPS)
        for q in range(SSD_NX):
            sl = slice(q * LANES, (q + 1) * LANES)
            y_ref[rows, sl] = (yz[q] * rstd * gn_ref[0, :, sl]).astype(BF16)

    a_row = -jnp.exp(alog_ref[0]) * _LOG2E
    for c in range(nc):
        v = dt_ref[0, :, c * CHUNK:(c + 1) * CHUNK] + dtb_ref[0]
        dt = jnp.maximum(v, 0.0) + jnp.log1p(jnp.exp(-jnp.abs(v)))
        dtt_sc[c] = dt
        dat_sc[c] = dt * a_row

    row = lax.broadcasted_iota(jnp.int32, (CHUNK, CHUNK), 0)
    colm = lax.broadcasted_iota(jnp.int32, (CHUNK, CHUNK), 1)
    lower = colm <= row
    upper = colm >= row
    tri_l = jnp.where(lower, 1.0, 0.0).astype(BF16)
    tri_u = jnp.where(upper, 1.0, 0.0).astype(BF16)
    lane = lax.broadcasted_iota(jnp.int32, (CHUNK, LANES), 1)
    head_row = lax.broadcasted_iota(jnp.int32, (SSD_DTW, 1), 0)
    zero16 = jnp.zeros((SSD_DTW, CHUNK), F32)

    col0, w0, e0 = 0, SSD_R * LANES, SSD_R * LANES + SSD_GW
    zpad = jnp.zeros((CHUNK - SSD_NPIECE * SSD_DTW, CHUNK), F32)

    def prepare(dr, c):
        lo = dr * SSD_R
        tri_t = tri_l if dr else tri_u
        own = (head_row >= lo) & (head_row < lo + SSD_R)
        rows = slice(c * CHUNK, (c + 1) * CHUNK)
        dtt = dtt_sc[c]
        cum_t = _dot3_right(dat_sc[c], tri_t)
        edge = cum_t[:, 0:1] if dr else cum_t[:, CHUNK - 1:CHUNK]
        w2_t = dtt * jnp.exp2(edge - cum_t)
        ecum_t = jnp.exp2(cum_t)
        pieces = (list(_split3(jnp.where(own, cum_t, zero16)))
                  + list(_split3(jnp.where(own, w2_t, zero16))[:2])
                  + list(_split3(jnp.where(own, ecum_t, zero16))[:2]))
        staged = jnp.concatenate([p.astype(F32) for p in pieces] + [zpad], axis=0)
        spread = jnp.dot(staged.T.astype(BF16), bsel_ref[dr], preferred_element_type=F32)
        cmat = ccv_sc[rows, :]
        if dr == 0:
            cb = lax.dot_general(cmat, bcv_sc[rows, :], _NT, preferred_element_type=F32)
            cb_sc[c] = cb
        else:
            cb = cb_sc[c]
        rowterm = cum_t - jnp.log2(dtt)
        return dict(dr=dr, c=c, rows=rows, lo=lo, rowterm=rowterm, spread=spread, cmat=cmat, cb=cb)

    def scan_chunk(pre):
        dr, c, rows, lo = pre["dr"], pre["c"], pre["rows"], pre["lo"]
        rowterm, spread, cmat, cb = pre["rowterm"], pre["spread"], pre["cmat"], pre["cb"]
        mask = upper if dr else lower
        s_prev = s_sc[...]
        yoff = jnp.dot(cmat, s_prev.astype(BF16), preferred_element_type=F32)
        xd = []
        for q in range(SSD_NX):
            xblk = xc_sc[q, rows, :]
            xb16 = xblk.astype(BF16)
            mh = []
            for r in (2 * q, 2 * q + 1):
                seg = spread[:, col0 + r * LANES:col0 + (r + 1) * LANES] - rowterm[lo + r:lo + r + 1, :]
                lm = jnp.exp2(jnp.where(mask, seg, -jnp.inf))
                mh.append((cb * lm).astype(BF16))
            rhs = jnp.concatenate([jnp.where(lane < SSD_HEADDIM, xb16, jnp.zeros_like(xb16)),
                                   jnp.where(lane >= SSD_HEADDIM, xb16, jnp.zeros_like(xb16))], axis=0)
            y = jnp.dot(jnp.concatenate(mh, axis=1), rhs, preferred_element_type=F32)
            sl = slice(q * LANES, (q + 1) * LANES)
            y = y + yoff[:, sl] * spread[:, e0 + q * LANES:e0 + (q + 1) * LANES]
            if dr == 0:
                y_sc[q, rows, :] = y + xblk * dsum_ref[0, :, sl]
            else:
                y_sc[q, rows, :] = y_sc[q, rows, :] + y
            xd.append((xblk * spread[:, w0 + q * LANES:w0 + (q + 1) * LANES]).astype(BF16))
        st = jnp.dot(bt_sc[c], jnp.concatenate(xd, axis=1), preferred_element_type=F32)
        er = 0 if dr else CHUNK - 1
        cd = spread[er:er + 1, e0:e0 + SSD_GW]
        s_sc[...] = s_prev * cd + st

    order = [(0, c) for c in range(nc)] + [(1, c) for c in range(nc - 1, -1, -1)]
    conv_chunk(0)
    pre = prepare(*order[0])
    for k, (dr, c) in enumerate(order):
        if c == (nc - 1 if dr else 0):
            if has_h0:
                s_sc[...] = h0_ref[0, dr, 0].T
            else:
                s_sc[...] = jnp.zeros_like(s_sc)
        if dr == 0 and c + 1 < nc:
            conv_chunk(c + 1)
        nxt = prepare(*order[k + 1]) if k + 1 < len(order) else None
        scan_chunk(pre)
        pre = nxt
        if dr == 1:
            gate_chunk(c)
        if st_ref is not None and c == (0 if dr else nc - 1):
            st_ref[0, dr, 0] = s_sc[...].T


def _ssd_core(zx, dtc, cw, cbias, dtb, alog, dsum, gn, h0, ybuf, t, row0, nseq):
    has_h0 = h0 is not None
    m = zx.shape[1]
    bsel = _ssd_spread_matrix()
    b0 = row0 // t
    nc = t // CHUNK
    nz = SSD_GROUPS * SSD_NX
    in_specs = [pl.BlockSpec((SSD_NX, t, LANES), lambda b, g: (g, b0 + b, 0)),
                pl.BlockSpec((SSD_NX, t, LANES), lambda b, g: (SSD_GROUPS + g, b0 + b, 0)),
                pl.BlockSpec((1, t, LANES), lambda b, g: (2 * nz + g, b0 + b, 0)),
                pl.BlockSpec((1, t, LANES), lambda b, g: (2 * nz + SSD_GROUPS + g, b0 + b, 0)),
                pl.BlockSpec((1, SSD_DTW, t), lambda b, g: (g, 0, b0 + b)),
                pl.BlockSpec((SSD_NX, 8, LANES), lambda b, g: (g, 0, 0)),
                pl.BlockSpec((1, 8, LANES), lambda b, g: (nz + g, 0, 0)),
                pl.BlockSpec((1, 8, LANES), lambda b, g: (nz + SSD_GROUPS + g, 0, 0)),
                pl.BlockSpec((SSD_NX, 1, LANES), lambda b, g: (g, 0, 0)),
                pl.BlockSpec((1, 1, LANES), lambda b, g: (nz + g, 0, 0)),
                pl.BlockSpec((1, 1, LANES), lambda b, g: (nz + SSD_GROUPS + g, 0, 0)),
                pl.BlockSpec((1, SSD_DTW, CHUNK), lambda b, g: (g, 0, 0)),
                pl.BlockSpec((1, SSD_DTW, CHUNK), lambda b, g: (g, 0, 0)),
                pl.BlockSpec((1, 1, SSD_GW), lambda b, g: (g, 0, 0)),
                pl.BlockSpec((1, 1, SSD_GW), lambda b, g: (g, 0, 0)),
                pl.BlockSpec(bsel.shape, lambda b, g: (0, 0, 0))]
    args = [zx, zx, zx, zx, dtc, cw, cw, cw, cbias, cbias, cbias, dtb, alog, dsum, gn, bsel]
    st_spec = pl.BlockSpec((1, 2, 1, SSD_GW, SSD_STATE), lambda b, g: (b, 0, g, 0, 0))
    y_spec = pl.BlockSpec((t, SSD_GW), lambda b, g: (b0 + b, g))
    y_shape = jax.ShapeDtypeStruct((m, SSD_GROUPS * SSD_GW), BF16)
    aliases = {}
    if has_h0:
        in_specs += [st_spec, pl.BlockSpec(memory_space=pl.ANY)]
        args += [h0, ybuf]
        aliases = {len(args) - 1: 0}
        out_specs, out_shape = y_spec, y_shape
    else:
        out_specs = [y_spec, st_spec]
        out_shape = [y_shape, jax.ShapeDtypeStruct((nseq, 2, SSD_GROUPS, SSD_GW, SSD_STATE), F32)]
    scratch = [pltpu.VMEM((SSD_NX + 2, t + 2 * SSD_TOP, LANES), F32),
               pltpu.VMEM((SSD_NX, t, LANES), F32),
               pltpu.VMEM((t, LANES), BF16),
               pltpu.VMEM((t, LANES), BF16),
               pltpu.VMEM((nc, CHUNK, CHUNK), BF16),
               pltpu.VMEM((nc, CHUNK, CHUNK), F32),
               pltpu.VMEM((SSD_NX, t, LANES), F32),
               pltpu.VMEM((nc, SSD_DTW, CHUNK), F32),
               pltpu.VMEM((nc, SSD_DTW, CHUNK), F32),
               pltpu.VMEM((SSD_STATE, SSD_GW), F32)]
    return pl.pallas_call(
        functools.partial(_ssd_core_kernel, t=t, has_h0=has_h0),
        grid=(nseq, SSD_GROUPS),
        in_specs=in_specs, out_specs=out_specs, out_shape=out_shape,
        input_output_aliases=aliases,
        scratch_shapes=scratch,
        compiler_params=_cparams(("arbitrary", "arbitrary")),
        name="ssd_core_h0" if has_h0 else "ssd_core",
    )(*args)


def _ssd_layer(x, mod4, P, layer, j, st0, npr, seq_p, seq_s):
    m, d = x.shape
    nbp, nbs = npr // seq_p, (m - npr) // seq_s
    heads = SSD_GROUPS * SSD_R
    inner = heads * SSD_HEADDIM
    gn_w = SSD_GROUPS * SSD_STATE
    nzx = 2 * inner + 2 * gn_w
    tm, tn = min(1024, seq_s), 1280
    cidx = _cond_index(tm, npr, seq_s)
    w_in_t = jnp.swapaxes(P["ssd_w_in"], 1, 2)
    nt = m // tm

    def row_tile(n, i):
        return jnp.where(n == 0, i, nt - 1)

    zx, dtc = pl.pallas_call(
        functools.partial(_ssd_in_kernel, d=d, heads=heads),
        grid=(nzx // tn, nt),
        in_specs=[pl.BlockSpec((tm, d), lambda n, i: (row_tile(n, i), 0)),
                  pl.BlockSpec((1, 1, 1, 6 * d), lambda n, i: (layer, cidx(row_tile(n, i)), 0, 0)),
                  pl.BlockSpec((1, 1, d), lambda n, i: (layer, 0, 0)),
                  pl.BlockSpec((1, tn, d), lambda n, i: (j, n, 0)),
                  pl.BlockSpec((1, LANES, d), lambda n, i: (j, nzx // LANES, 0))],
        out_specs=[pl.BlockSpec((tn // LANES, tm, LANES), lambda n, i: (n, i, 0)),
                   pl.BlockSpec((SSD_GROUPS, SSD_DTW, tm), lambda n, i: (0, 0, row_tile(n, i)))],
        out_shape=[jax.ShapeDtypeStruct((nzx // LANES, m, LANES), F32),
                   jax.ShapeDtypeStruct((SSD_GROUPS, SSD_DTW, m), F32)],
        scratch_shapes=[pltpu.VMEM((m, d), BF16)],
        compiler_params=_cparams(("arbitrary", "arbitrary")),
        name="ssd_in",
    )(x, mod4, P["g_norm1"].reshape(-1, 1, d), w_in_t, w_in_t)

    def compact(a):
        a = a.reshape(2, SSD_GROUPS, SSD_R).transpose(1, 0, 2).reshape(SSD_GROUPS, SSD_DTW, 1)
        return jnp.broadcast_to(a, (SSD_GROUPS, SSD_DTW, CHUNK))

    ncb = (inner + 2 * gn_w) // LANES
    cw = jnp.pad(P["ssd_w_conv"][j], ((0, 8 - SSD_CONV), (0, 0))).reshape(8, ncb, LANES).transpose(1, 0, 2)
    cbias = P["ssd_b_conv"][j].reshape(ncb, 1, LANES)
    dsum = jnp.repeat(P["ssd_d"][j][0] + P["ssd_d"][j][1], SSD_HEADDIM).reshape(SSD_GROUPS, 1, SSD_GW)
    gn = P["ssd_g_norm"][j].reshape(SSD_GROUPS, 1, SSD_GW)
    common = (zx, dtc, cw, cbias, compact(P["ssd_dt_bias"][j]), compact(P["ssd_a_log"][j]), dsum, gn)
    yp, st = _ssd_core(*common, None, None, seq_p, 0, nbp)
    h0 = st0.reshape(nbs, 2, SSD_GROUPS, SSD_GW, SSD_STATE)
    y = _ssd_core(*common, h0, yp, seq_s, npr, nbs)
    return y, st.reshape(nbp, 2, heads, SSD_HEADDIM, SSD_STATE)


def kernel(x_prompt, x_sample, cache_ckv, cache_kpe, state_ssm, c, c_ctx, w_ada, b_ada, g_norm1, g_norm2,
           mla_w_dq, mla_g_q, mla_w_uq, mla_w_dkv, mla_g_kv, mla_w_ukv, mla_g_qn, mla_g_kn, mla_w_o,
           cv_w_pw1, cv_b_pw1, cv_w_dw, cv_b_dw, cv_g_ln, cv_b_ln, cv_w_pw2, cv_b_pw2,
           ssd_w_in, ssd_w_conv, ssd_b_conv, ssd_dt_bias, ssd_a_log, ssd_d, ssd_g_norm, ssd_w_out,
           ffn_w_in, ffn_w_out):
    P = dict(g_norm1=g_norm1, g_norm2=g_norm2,
             mla_w_dq=mla_w_dq, mla_g_q=mla_g_q, mla_w_uq=mla_w_uq, mla_w_dkv=mla_w_dkv, mla_g_kv=mla_g_kv,
             mla_w_ukv=mla_w_ukv, mla_g_qn=mla_g_qn, mla_g_kn=mla_g_kn, mla_w_o=mla_w_o,
             cv_w_pw1=cv_w_pw1, cv_b_pw1=cv_b_pw1, cv_w_dw=cv_w_dw, cv_b_dw=cv_b_dw, cv_g_ln=cv_g_ln,
             cv_b_ln=cv_b_ln, cv_w_pw2=cv_w_pw2, cv_b_pw2=cv_b_pw2,
             ssd_w_in=ssd_w_in, ssd_w_conv=ssd_w_conv, ssd_b_conv=ssd_b_conv, ssd_dt_bias=ssd_dt_bias,
             ssd_a_log=ssd_a_log, ssd_d=ssd_d, ssd_g_norm=ssd_g_norm, ssd_w_out=ssd_w_out)
    nbp, seq_p, d = x_prompt.shape
    nbs, seq_s, _ = x_sample.shape
    depth = w_ada.shape[0]
    npr = nbp * seq_p
    x = (x_prompt.reshape(npr, d), x_sample.reshape(nbs * seq_s, d))
    cond8 = jnp.concatenate([c_ctx[None, :], c, jnp.zeros((8 - 1 - nbs, d), F32)], axis=0)
    mod4 = _ada(cond8, w_ada, b_ada).reshape(depth, 8, 1, 6 * d)
    mla_w = _mla_weights(P, seq_s)
    ffn_tm = min(1024, seq_s)
    ffn_cidx = _cond_index(ffn_tm, npr, seq_s)
    ckvs, kpes, ssms = [], [], []
    for i in range(depth):
        kind, j = i % N_MIXERS, i // N_MIXERS
        if kind != 0 and isinstance(x, tuple):
            x = jnp.concatenate(x, axis=0)
        y = w_o = None
        if kind == 0:
            x, ckv, kpe = _mla_layer(x, mod4, P, mla_w, i, j, cache_ckv[:, j], cache_kpe[:, j], npr, seq_p, seq_s)
            ckvs.append(ckv)
            kpes.append(kpe)
        elif kind == 1:
            x = _conv_layer(x, mod4, P, i, j, npr, seq_p, seq_s)
        else:
            y, st = _ssd_layer(x, mod4, P, i, j, state_ssm[:, j], npr, seq_p, seq_s)
            w_o = ssd_w_out[j].astype(BF16)
            ssms.append(st)
        x = _ffn(x, mod4, g_norm2, ffn_w_in, ffn_w_out, i, ffn_cidx, ffn_tm, 256, npr,
                 y=y, w_o=w_o, split=(i == depth - 1))
    return (x[0].reshape(nbp, seq_p, d), x[1].reshape(nbs, seq_s, d),
            jnp.stack(ckvs, axis=1), jnp.stack(kpes, axis=1), jnp.stack(ssms, axis=1))
```

```python
import functools

import jax
import jax.numpy as jnp
import numpy as np
from jax import lax
from jax.experimental import pallas as pl
from jax.experimental.pallas import tpu as pltpu

F32 = jnp.float32
BF16 = jnp.bfloat16

EPS = 1e-6
ROPE_BASE = 10000.0
GRID_W = 64
N_MIXERS = 3

LANES = 128
VMEM_LIMIT_BYTES = 56 * 1024 * 1024

MLA_HEADS = 16
QK_NOPE = 64
QK_ROPE = 32
QK_DIM = QK_NOPE + QK_ROPE
V_HEAD = 64
HEAD_PAD = LANES
CONV_WIDTH = 31
SSD_HEADDIM = 64
SSD_GROUPS = 4
SSD_STATE = 128
SSD_CONV = 5
CHUNK = 128


def _cparams(sem):
    return pltpu.CompilerParams(dimension_semantics=sem, vmem_limit_bytes=VMEM_LIMIT_BYTES)


def _silu(x):
    return x * jax.nn.sigmoid(x)


def _normmod(x, g, sh, sc):
    ms = jnp.mean(x * x, axis=-1, keepdims=True)
    return (x * lax.rsqrt(ms + EPS)) * (g * (1.0 + sc)) + sh


def _cond_index(tm, n_prompt_rows, sample_seq):
    def f(i):
        r = i * tm
        return jnp.where(r < n_prompt_rows, 0, 1 + (r - n_prompt_rows) // sample_seq)
    return f


def _mod_spec(layer, cidx, d6, ngrid):
    if ngrid == 1:
        return pl.BlockSpec((1, 1, 1, d6), lambda i: (layer, cidx(i), 0, 0))
    return pl.BlockSpec((1, 1, 1, d6), lambda i, j: (layer, cidx(i), 0, 0))


def _ada_kernel(c_ref, w_ref, b_ref, o_ref):
    s = _silu(c_ref[...]).astype(BF16)
    o_ref[0] = jnp.dot(s, w_ref[0].astype(BF16), preferred_element_type=F32) + b_ref[0]


def _ada(cond8, w_ada, b_ada):
    depth, d, n = w_ada.shape
    tn = n // 4
    return pl.pallas_call(
        _ada_kernel,
        grid=(depth, n // tn),
        in_specs=[pl.BlockSpec((8, d), lambda l, j: (0, 0)),
                  pl.BlockSpec((1, d, tn), lambda l, j: (l, 0, j)),
                  pl.BlockSpec((1, 1, tn), lambda l, j: (l, 0, j))],
        out_specs=pl.BlockSpec((1, 8, tn), lambda l, j: (l, 0, j)),
        out_shape=jax.ShapeDtypeStruct((depth, 8, n), F32),
        compiler_params=_cparams(("arbitrary", "arbitrary")),
        name="ada",
    )(cond8, w_ada, b_ada.reshape(depth, 1, n))


FFN_SLOTS = 3


def _ffn_kernel(*refs, d, n_in_a, n_out_a, fused, layer, hid, th, ntiles):
    refs = list(refs)
    xa_ref = refs.pop(0)
    xb_ref = refs.pop(0) if n_in_a is not None else None
    y_ref, wo1_ref = (refs.pop(0), refs.pop(0)) if fused else (None, None)
    mod_ref, g_ref, win_hbm, wout_hbm = refs[:4]
    if n_out_a is None:
        o_ref, h_sc, wa_buf, wu_buf, wo_buf, sem = refs[4:]
        acc = o_ref
    else:
        oa_ref, ob_ref, h_sc, acc, wa_buf, wu_buf, wo_buf, sem = refs[4:]
    i = pl.program_id(0)
    nj = hid // th
    base = lax.rem(i * (nj % FFN_SLOTS), FFN_SLOTS)

    def slot_of(j):
        return lax.rem(base + j % FFN_SLOTS, FFN_SLOTS)

    def chunk_copies(j):
        c, s = j % nj, slot_of(j)
        return (pltpu.make_async_copy(win_hbm.at[layer, :, pl.ds(c * th, th)], wa_buf.at[s], sem.at[0, s]),
                pltpu.make_async_copy(win_hbm.at[layer, :, pl.ds(hid + c * th, th)], wu_buf.at[s], sem.at[1, s]),
                pltpu.make_async_copy(wout_hbm.at[layer, pl.ds(c * th, th), :], wo_buf.at[s], sem.at[2, s]))

    def start(j):
        for cp in chunk_copies(j):
            cp.start()

    def wait(j):
        for cp in chunk_copies(j):
            cp.wait()

    @pl.when(i == 0)
    def _():
        start(0)
        start(1)

    wait(0)
    wa0, wu0 = wa_buf[base].astype(BF16), wu_buf[base].astype(BF16)
    tm = acc.shape[0]
    rc = min(tm, 256)
    a_rows, u_rows = [], []
    for r0 in range(0, tm, rc):
        rows = slice(r0, r0 + rc)
        x = xa_ref[rows, :] if xb_ref is None else jnp.where(i < n_in_a, xa_ref[rows, :], xb_ref[rows, :])
        if fused:
            x = x + mod_ref[0, 0, :, 2 * d:3 * d] * jnp.dot(y_ref[rows, :], wo1_ref[...],
                                                             preferred_element_type=F32)
        acc[rows, :] = x
        hr = _normmod(x, g_ref[0], mod_ref[0, 0, :, 3 * d:4 * d], mod_ref[0, 0, :, 4 * d:5 * d]).astype(BF16)
        h_sc[rows, :] = hr
        a_rows.append(jnp.dot(hr, wa0, preferred_element_type=F32))
        u_rows.append(jnp.dot(hr, wu0, preferred_element_type=F32))
    au = (jnp.concatenate(a_rows, axis=0), jnp.concatenate(u_rows, axis=0))

    gate2 = mod_ref[0, 0, :, 5 * d:6 * d]
    h = h_sc[...]

    def up(j):
        s = slot_of(j)
        return (jnp.dot(h, wa_buf[s].astype(BF16), preferred_element_type=F32),
                jnp.dot(h, wu_buf[s].astype(BF16), preferred_element_type=F32))

    for j in range(nj):
        if j + 1 < nj:
            wait(j + 1)
        if j + 2 < nj:
            start(j + 2)
        else:
            @pl.when(i + 1 < ntiles)
            def _(j=j):
                start(j + 2)
        a, u = au
        t = (_silu(a) * u).astype(BF16)
        if j + 1 < nj:
            au = up(j + 1)
        acc[...] += gate2 * jnp.dot(t, wo_buf[slot_of(j)].astype(BF16), preferred_element_type=F32)

    if n_out_a is not None:
        @pl.when(i < n_out_a)
        def _():
            oa_ref[...] = acc[...]

        @pl.when(i >= n_out_a)
        def _():
            ob_ref[...] = acc[...]


def _ffn(x, mod4, g2, w_in, w_out, layer, cidx, tm, th, npr, y=None, w_o=None, split=False):
    pair = isinstance(x, tuple)
    d = x[0].shape[1] if pair else x.shape[1]
    m = (x[0].shape[0] + x[1].shape[0]) if pair else x.shape[0]
    hid = w_out.shape[1]
    nj = hid // th
    n_a = npr // tm
    fused = y is not None
    if pair:
        in_specs = [pl.BlockSpec((tm, d), lambda i: (jnp.minimum(i, n_a - 1), 0)),
                    pl.BlockSpec((tm, d), lambda i: (jnp.maximum(i - n_a, 0), 0))]
        args = list(x)
    else:
        in_specs = [pl.BlockSpec((tm, d), lambda i: (i, 0))]
        args = [x]
    if fused:
        k = y.shape[1]
        in_specs += [pl.BlockSpec((tm, k), lambda i: (i, 0)), pl.BlockSpec((k, d), lambda i: (0, 0))]
        args += [y, w_o]
    in_specs += [_mod_spec(layer, cidx, 6 * d, 1),
                 pl.BlockSpec((1, 1, d), lambda i: (layer, 0, 0)),
                 pl.BlockSpec(memory_space=pl.ANY),
                 pl.BlockSpec(memory_space=pl.ANY)]
    args += [mod4, g2.reshape(g2.shape[0], 1, d), w_in, w_out]
    scratch = [pltpu.VMEM((tm, d), BF16)]
    if split:
        out_specs = [pl.BlockSpec((tm, d), lambda i: (jnp.minimum(i, n_a - 1), 0)),
                     pl.BlockSpec((tm, d), lambda i: (jnp.maximum(i - n_a, 0), 0))]
        out_shape = [jax.ShapeDtypeStruct((npr, d), F32), jax.ShapeDtypeStruct((m - npr, d), F32)]
        scratch.append(pltpu.VMEM((tm, d), F32))
    else:
        out_specs = pl.BlockSpec((tm, d), lambda i: (i, 0))
        out_shape = jax.ShapeDtypeStruct((m, d), F32)
    scratch += [pltpu.VMEM((FFN_SLOTS, d, th), F32), pltpu.VMEM((FFN_SLOTS, d, th), F32),
                pltpu.VMEM((FFN_SLOTS, th, d), F32), pltpu.SemaphoreType.DMA((3, FFN_SLOTS))]
    assert nj >= 2 and hid % th == 0
    return pl.pallas_call(
        functools.partial(_ffn_kernel, d=d, n_in_a=n_a if pair else None, n_out_a=n_a if split else None,
                          fused=fused, layer=layer, hid=hid, th=th, ntiles=m // tm),
        grid=(m // tm,),
        in_specs=in_specs,
        out_specs=out_specs,
        out_shape=out_shape,
        scratch_shapes=scratch,
        compiler_params=_cparams(("arbitrary",)),
        name="ffn_proj" if fused else "ffn",
    )(*args)


def _conv_a_kernel(x_ref, mod_ref, g_ref, w_ref, b_ref, o_ref, w_sc, *, d):
    @pl.when(pl.program_id(0) == 0)
    def _():
        w_sc[...] = w_ref[0].astype(BF16)

    h = _normmod(x_ref[...], g_ref[0], mod_ref[0, 0, :, 0:d], mod_ref[0, 0, :, d:2 * d]).astype(BF16)
    y = jnp.dot(h, w_sc[...], preferred_element_type=F32) + b_ref[0]
    u = y[:, :d] * jax.nn.sigmoid(y[:, d:])
    for cb in range(d // LANES):
        o_ref[cb] = u[:, cb * LANES:(cb + 1) * LANES]


def _conv_b_kernel(*refs, d, t, rc, top):
    (x_ref, u_ref, mod_ref, wdw_ref, bdw_ref, gln_ref, bln_ref, w2_ref, b2_ref) = refs[:9]
    o_ref, upad, conv_sc, y_sc, w_sc = refs[-5:]
    nb = d // LANES
    half = (CONV_WIDTH - 1) // 2

    @pl.when(pl.program_id(0) == 0)
    def _():
        w_sc[...] = w2_ref[0].astype(BF16)
        upad[:, 0:top, :] = jnp.zeros((nb, top, LANES), F32)
        upad[:, top + t:2 * top + t, :] = jnp.zeros((nb, top, LANES), F32)

    upad[:, top:top + t, :] = u_ref[...]

    def conv_block(cb, carry):
        for r0 in range(0, t, rc):
            acc = jnp.broadcast_to(bdw_ref[cb], (rc, LANES))
            for k in range(CONV_WIDTH):
                acc = acc + wdw_ref[cb, pl.ds(k, 1), :] * upad[cb, pl.ds(r0 + top - half + k, rc), :]
            conv_sc[cb, pl.ds(r0, rc), :] = acc
        return carry

    lax.fori_loop(0, nb, conv_block, 0)

    rl = min(t, 256)
    for r0 in range(0, t, rl):
        s = jnp.zeros((rl, 1), F32)
        for cb in range(nb):
            s = s + jnp.sum(conv_sc[cb, r0:r0 + rl, :], axis=-1, keepdims=True)
        mean = s * (1.0 / d)
        v = jnp.zeros((rl, 1), F32)
        for cb in range(nb):
            c = conv_sc[cb, r0:r0 + rl, :] - mean
            v = v + jnp.sum(c * c, axis=-1, keepdims=True)
        rstd = lax.rsqrt(v * (1.0 / d) + EPS)
        for cb in range(nb):
            sl = slice(cb * LANES, (cb + 1) * LANES)
            y = (conv_sc[cb, r0:r0 + rl, :] - mean) * rstd * gln_ref[:, sl] + bln_ref[:, sl]
            y_sc[r0:r0 + rl, sl] = _silu(y).astype(BF16)

    gate = mod_ref[0, 0, :, 2 * d:3 * d]
    r = jnp.dot(y_sc[...], w_sc[...], preferred_element_type=F32) + b2_ref[0]
    o_ref[...] = x_ref[...] + gate * r


def _conv_b(x, u, mod4, wdw, bdw, gln, bln, w2, b2, layer, j, t, row0, nseq, cond0, cond_step, obuf):
    m, d = x.shape
    nb = d // LANES
    top = 16
    blk0 = row0 // t
    in_specs = [pl.BlockSpec((t, d), lambda i: (blk0 + i, 0)),
                pl.BlockSpec((nb, t, LANES), lambda i: (0, blk0 + i, 0)),
                pl.BlockSpec((1, 1, 1, 6 * d), lambda i: (layer, cond0 + cond_step * i, 0, 0)),
                pl.BlockSpec((nb, 32, LANES), lambda i: (0, 0, 0)),
                pl.BlockSpec((nb, 1, LANES), lambda i: (0, 0, 0)),
                pl.BlockSpec((1, d), lambda i: (0, 0)),
                pl.BlockSpec((1, d), lambda i: (0, 0)),
                pl.BlockSpec((1, d, d), lambda i: (j, 0, 0)),
                pl.BlockSpec((1, 1, d), lambda i: (j, 0, 0))]
    args = [x, u, mod4, wdw, bdw, gln, bln, w2, b2]
    aliases = {}
    if obuf is not None:
        in_specs.append(pl.BlockSpec(memory_space=pl.ANY))
        args.append(obuf)
        aliases = {len(args) - 1: 0}
    return pl.pallas_call(
        functools.partial(_conv_b_kernel, d=d, t=t, rc=64, top=top),
        grid=(nseq,),
        in_specs=in_specs,
        out_specs=pl.BlockSpec((t, d), lambda i: (blk0 + i, 0)),
        out_shape=jax.ShapeDtypeStruct((m, d), F32),
        input_output_aliases=aliases,
        scratch_shapes=[pltpu.VMEM((nb, t + 2 * top, LANES), F32),
                        pltpu.VMEM((nb, t, LANES), F32),
                        pltpu.VMEM((t, d), BF16),
                        pltpu.VMEM((d, d), BF16)],
        compiler_params=_cparams(("arbitrary",)),
        name="conv_b",
    )(*args)


def _conv_layer(x, mod4, P, layer, j, npr, seq_p, seq_s):
    m, d = x.shape
    nb = d // LANES
    tm = min(1024, seq_s)
    cidx = _cond_index(tm, npr, seq_s)
    u = pl.pallas_call(
        functools.partial(_conv_a_kernel, d=d),
        grid=(m // tm,),
        in_specs=[pl.BlockSpec((tm, d), lambda i: (i, 0)),
                  _mod_spec(layer, cidx, 6 * d, 1),
                  pl.BlockSpec((1, 1, d), lambda i: (layer, 0, 0)),
                  pl.BlockSpec((1, d, 2 * d), lambda i: (j, 0, 0)),
                  pl.BlockSpec((1, 1, 2 * d), lambda i: (j, 0, 0))],
        out_specs=pl.BlockSpec((nb, tm, LANES), lambda i: (0, i, 0)),
        out_shape=jax.ShapeDtypeStruct((nb, m, LANES), F32),
        scratch_shapes=[pltpu.VMEM((d, 2 * d), BF16)],
        compiler_params=_cparams(("arbitrary",)),
        name="conv_a",
    )(x, mod4, P["g_norm1"].reshape(-1, 1, d), P["cv_w_pw1"], P["cv_b_pw1"].reshape(-1, 1, 2 * d))
    wdw = jnp.pad(P["cv_w_dw"][j], ((0, 32 - CONV_WIDTH), (0, 0))).reshape(32, nb, LANES).transpose(1, 0, 2)
    bdw = P["cv_b_dw"][j].reshape(nb, 1, LANES)
    gln = P["cv_g_ln"][j].reshape(1, d)
    bln = P["cv_b_ln"][j].reshape(1, d)
    b2 = P["cv_b_pw2"].reshape(-1, 1, d)
    args = (x, u, mod4, wdw, bdw, gln, bln, P["cv_w_pw2"], b2, layer, j)
    op = _conv_b(*args, seq_p, 0, npr // seq_p, 0, 0, None)
    return _conv_b(*args, seq_s, npr, (m - npr) // seq_s, 1, 1, op)


_LOG2E = 1.4426950408889634
_ROT = QK_ROPE // 4


def _swap_index():
    idx = list(range(HEAD_PAD))
    for base in (QK_NOPE, QK_NOPE + 2 * _ROT):
        for l in range(_ROT):
            idx[base + l], idx[base + _ROT + l] = base + _ROT + l, base + l
    return jnp.asarray(idx, jnp.int32)


def _swap_cols(w):
    lane = jnp.arange(HEAD_PAD)
    rot = (lane >= QK_NOPE) & (lane < QK_DIM)
    return jnp.where(rot, jnp.take(w, _swap_index(), axis=-1), 0.0)


def _rope_tables(seq, gqn, gkn):
    rows = seq // GRID_W
    row = jnp.repeat(jnp.arange(rows, dtype=F32), GRID_W)
    col = jnp.tile(jnp.arange(GRID_W, dtype=F32), rows)
    axis = QK_ROPE // 2
    inv = ROPE_BASE ** (-jnp.arange(0, axis, 2, dtype=F32) / axis)
    ar, ac = row[:, None] * inv, col[:, None] * inv
    cr, sr, cc, sc = jnp.cos(ar), jnp.sin(ar), jnp.cos(ac), jnp.sin(ac)
    one = jnp.ones((seq, QK_NOPE), F32)
    zn = jnp.zeros((seq, QK_NOPE), F32)
    zp = jnp.zeros((seq, HEAD_PAD - QK_DIM), F32)
    c = jnp.concatenate([one, cr, cr, cc, cc, zp], axis=1)
    s = jnp.concatenate([zn, -sr, sr, -sc, sc, zp], axis=1)
    idx = _swap_index()
    return tuple(t for g in (gqn, gkn) for t in (c[None] * g, s[None] * jnp.take(g, idx, axis=-1)))


def _head_rstd(xh):
    ms = jnp.sum(xh * xh, axis=-1, keepdims=True) * (1.0 / QK_DIM)
    return lax.rsqrt(ms + EPS)


def _expand_kv(ckv_b, kpe, wuk_ref, wv_ref, gk, rot, k_ref, v_ref, v_transposed):
    kk = jnp.dot(ckv_b, wuk_ref[0], preferred_element_type=F32)
    if v_transposed:
        v = lax.dot_general(wv_ref[0], ckv_b, _NT, preferred_element_type=F32)
    else:
        v = jnp.dot(ckv_b, wv_ref[0], preferred_element_type=F32)
    v_ref[...] = v.astype(BF16)
    for h in range(MLA_HEADS):
        sl = slice(h * HEAD_PAD, (h + 1) * HEAD_PAD)
        kh = kk[:, sl] + kpe
        r = _head_rstd(kh)
        kh = (kh * gk) * r if rot is None else (kh * rot[0] + rot[1]) * r
        k_ref[:, sl] = kh.astype(BF16)


def _mla_proj_kernel(*refs, d, rope):
    (x_ref, mod_ref, g_ref, wcat_ref, gq_ref, wuq_ref, gkv_ref, wuk_ref, wv_ref, gqn_ref, gkn_ref) = refs[:11]
    if rope:
        cq_ref, sq_ref, ck_ref, sk_ref = refs[11:15]
        q_ref, k_ref, v_ref = refs[15:18]
    else:
        q_ref, k_ref, v_ref, ckv_ref, kpe_ref = refs[11:16]
    h = _normmod(x_ref[...], g_ref[0], mod_ref[0, 0, :, 0:d], mod_ref[0, 0, :, d:2 * d]).astype(BF16)
    t = jnp.dot(h, wcat_ref[0], preferred_element_type=F32)
    nq = gq_ref.shape[2]
    nkv = gkv_ref.shape[2]
    hq = MLA_HEADS * HEAD_PAD
    qd, craw, kpe = t[:, :nq], t[:, nq:nq + nkv], t[:, nq + nkv:nq + nkv + HEAD_PAD]
    cq = (qd * lax.rsqrt(jnp.mean(qd * qd, axis=-1, keepdims=True) + EPS) * gq_ref[0]).astype(BF16)
    ckv = craw * lax.rsqrt(jnp.mean(craw * craw, axis=-1, keepdims=True) + EPS) * gkv_ref[0]
    if not rope:
        ckv_ref[...] = ckv
        kpe_ref[...] = kpe
    q = jnp.dot(cq, wuq_ref[0], preferred_element_type=F32)
    if rope:
        ct, st = cq_ref[0], sq_ref[0]
    else:
        gqn = gqn_ref[0]
    for hd in range(MLA_HEADS):
        sl = slice(hd * HEAD_PAD, (hd + 1) * HEAD_PAD)
        qh = q[:, sl]
        r = _head_rstd(qh)
        qh = (qh * ct + q[:, hq + hd * HEAD_PAD:hq + (hd + 1) * HEAD_PAD] * st) * r if rope else (qh * gqn) * r
        q_ref[:, sl] = qh.astype(BF16)
    rot = None
    if rope:
        kpe2 = t[:, nq + nkv + HEAD_PAD:]
        rot = (ck_ref[0], kpe2 * sk_ref[0])
    _expand_kv(ckv.astype(BF16), kpe, wuk_ref, wv_ref, gkn_ref[0], rot, k_ref, v_ref, rope)


def _kv_ctx_kernel(ckv_ref, kpe_ref, wuk_ref, wv_ref, gkn_ref, k_ref, v_ref):
    _expand_kv(ckv_ref[...].astype(BF16), kpe_ref[...], wuk_ref, wv_ref, gkn_ref[0], None, k_ref, v_ref, True)


_NT = (((1,), (1,)), ((), ()))


def _attn_prompt_kernel(q_ref, k_ref, v_ref, x_ref, mod_ref, wo_ref, o_ref, a_sc, *, d):
    t = q_ref.shape[0]
    lane = lax.broadcasted_iota(jnp.int32, (t, LANES), 1)
    for hp in range(MLA_HEADS // 2):
        vp = v_ref[:, hp * LANES:(hp + 1) * LANES]
        outs = []
        for hh in range(2):
            sl = slice((2 * hp + hh) * HEAD_PAD, (2 * hp + hh + 1) * HEAD_PAD)
            s = lax.dot_general(q_ref[:, sl], k_ref[:, sl], _NT, preferred_element_type=F32)
            e = jnp.exp2(s - s.max(axis=-1, keepdims=True))
            l = jnp.sum(e, axis=-1, keepdims=True)
            outs.append(jnp.dot(e.astype(BF16), vp, preferred_element_type=F32) * (1.0 / l))
        a_sc[:, hp * LANES:(hp + 1) * LANES] = jnp.where(lane < V_HEAD, outs[0], outs[1]).astype(BF16)
    gate = mod_ref[0, 0, :, 2 * d:3 * d]
    o_ref[...] = x_ref[...] + gate * jnp.dot(a_sc[...], wo_ref[0], preferred_element_type=F32)


def _attn_sample_kernel(q_ref, kc_ref, vtc_ref, kl_ref, vtl_ref, x_ref, mod_ref, wo_ref, obuf_ref, o_ref, *, d):
    del obuf_ref
    nh = q_ref.shape[1] // HEAD_PAD
    row = lax.broadcasted_iota(jnp.int32, (LANES, 1), 0)

    def scores(hh):
        sl = slice(hh * HEAD_PAD, (hh + 1) * HEAD_PAD)
        qh = q_ref[:, sl]
        return [lax.dot_general(k_ref[:, sl], qh, _NT, preferred_element_type=F32)
                for k_ref in (kc_ref, kl_ref)]

    halves = []
    nxt = scores(0)
    for hh in range(nh):
        sts = nxt
        if hh + 1 < nh:
            nxt = scores(hh + 1)
        m = jnp.max(sts[0], axis=0, keepdims=True)
        for st in sts[1:]:
            m = jnp.maximum(m, jnp.max(st, axis=0, keepdims=True))
        pr = slice((hh // 2) * LANES, (hh // 2 + 1) * LANES)
        own = (row < V_HEAD) if hh % 2 == 0 else (row >= V_HEAD)
        acc = None
        for st, vt_ref in zip(sts, (vtc_ref, vtl_ref)):
            e = jnp.exp2(st - m).astype(BF16)
            vt = vt_ref[pr, :]
            lhs = jnp.where(own, vt, jnp.ones_like(vt))
            pv = jnp.dot(lhs, e, preferred_element_type=F32)
            acc = pv if acc is None else acc + pv
        if hh % 2 == 0:
            halves.append(acc[0:V_HEAD] * (1.0 / acc[V_HEAD:V_HEAD + 1]))
        else:
            halves.append(acc[V_HEAD:2 * V_HEAD] * (1.0 / acc[0:1]))
    attn = jnp.concatenate(halves, axis=0).T.astype(BF16)
    gate = mod_ref[0, 0, :, 2 * d:3 * d]
    o_ref[...] = x_ref[...] + gate * jnp.dot(attn, wo_ref[0], preferred_element_type=F32)


def _mla_weights(P, seq_s):
    n = P["mla_w_dq"].shape[0]
    nq = P["mla_w_dq"].shape[2]
    nkv = P["mla_g_kv"].shape[1]
    pad = HEAD_PAD - QK_DIM
    w_dkv = P["mla_w_dkv"]
    w_kpe = jnp.pad(w_dkv[:, :, nkv:], ((0, 0), (0, 0), (QK_NOPE, pad)))
    wcat = jnp.concatenate([P["mla_w_dq"], w_dkv[:, :, :nkv], w_kpe], axis=2).astype(BF16)
    wuq = jnp.pad(P["mla_w_uq"].reshape(n, nq, MLA_HEADS, QK_DIM), ((0, 0), (0, 0), (0, 0), (0, pad)))
    wuq = wuq.reshape(n, nq, MLA_HEADS * HEAD_PAD).astype(BF16)
    wukv = P["mla_w_ukv"].reshape(n, nkv, MLA_HEADS, QK_NOPE + V_HEAD)
    wuk = jnp.pad(wukv[..., :QK_NOPE], ((0, 0), (0, 0), (0, 0), (0, HEAD_PAD - QK_NOPE)))
    wuk = wuk.reshape(n, nkv, MLA_HEADS * HEAD_PAD).astype(BF16)
    wv = wukv[..., QK_NOPE:].reshape(n, nkv, MLA_HEADS * V_HEAD).astype(BF16)
    gq = P["mla_g_q"].reshape(n, 1, nq)
    gkv = P["mla_g_kv"].reshape(n, 1, nkv)
    gqn = (jnp.pad(P["mla_g_qn"], ((0, 0), (0, pad))) * (QK_DIM ** -0.5 * _LOG2E)).reshape(n, 1, HEAD_PAD)
    gkn = jnp.pad(P["mla_g_kn"], ((0, 0), (0, pad))).reshape(n, 1, HEAD_PAD)
    wuq2 = _swap_cols(wuq.reshape(n, nq, MLA_HEADS, HEAD_PAD)).reshape(n, nq, MLA_HEADS * HEAD_PAD)
    wuq_r = jnp.concatenate([wuq, wuq2], axis=2)
    wcat_r = jnp.concatenate([wcat, _swap_cols(w_kpe).astype(BF16)], axis=2)
    return dict(plain=(wcat, gq, wuq, gkv, wuk, wv, gqn, gkn), rope=(wcat_r, wuq_r, jnp.swapaxes(wv, 1, 2)),
                wo=P["mla_w_o"].astype(BF16), tabs=_rope_tables(seq_s, gqn, gkn))


def _layer_spec(a, j):
    return pl.BlockSpec((1,) + a.shape[1:], lambda *_: (j,) + (0,) * (a.ndim - 1))


def _mla_proj(x, xblk0, mod4, g1, W, layer, j, row0, nrows, cidx, tm, rope_args):
    d = x.shape[1]
    wcat, gq, wuq, gkv, wuk, wv, gqn, gkn = W
    blk0 = row0 // tm
    hq = MLA_HEADS * HEAD_PAD
    hv = MLA_HEADS * V_HEAD
    nkv = gkv.shape[2]
    rope = rope_args is not None
    args = [x, mod4, g1, wcat, gq, wuq, gkv, wuk, wv, gqn, gkn]
    if rope:
        wcat_r, wuq_r, wvt, tabs = rope_args
        args[3], args[5], args[8] = wcat_r, wuq_r, wvt
    in_specs = [pl.BlockSpec((tm, d), lambda i: (xblk0 + i, 0)),
                pl.BlockSpec((1, 1, 1, 6 * d), lambda i: (layer, cidx(blk0 + i), 0, 0)),
                pl.BlockSpec((1, 1, d), lambda i: (layer, 0, 0))] + [_layer_spec(a, j) for a in args[3:]]
    out_specs = [pl.BlockSpec((tm, hq), lambda i: (i, 0)),
                 pl.BlockSpec((tm, hq), lambda i: (i, 0))]
    out_shape = [jax.ShapeDtypeStruct((nrows, hq), BF16),
                 jax.ShapeDtypeStruct((nrows, hq), BF16)]
    if rope:
        per = tabs[0].shape[1] // tm
        in_specs += [pl.BlockSpec((1, tm, HEAD_PAD), lambda i: (j, i % per, 0))] * 4
        args += list(tabs)
        out_specs += [pl.BlockSpec((hv, tm), lambda i: (0, i))]
        out_shape += [jax.ShapeDtypeStruct((hv, nrows), BF16)]
    else:
        out_specs += [pl.BlockSpec((tm, hv), lambda i: (i, 0)),
                      pl.BlockSpec((tm, nkv), lambda i: (i, 0)), pl.BlockSpec((tm, HEAD_PAD), lambda i: (i, 0))]
        out_shape += [jax.ShapeDtypeStruct((nrows, hv), BF16),
                      jax.ShapeDtypeStruct((nrows, nkv), F32), jax.ShapeDtypeStruct((nrows, HEAD_PAD), F32)]
    return pl.pallas_call(
        functools.partial(_mla_proj_kernel, d=d, rope=rope),
        grid=(nrows // tm,),
        in_specs=in_specs, out_specs=out_specs, out_shape=out_shape,
        compiler_params=_cparams(("arbitrary",)),
        name="mla_proj_rope" if rope else "mla_proj",
    )(*args)


def _mla_layer(x, mod4, P, MW, layer, j, cache_ckv, cache_kpe, npr, seq_p, seq_s):
    xa, xb = x if isinstance(x, tuple) else (x, x)
    d = xa.shape[1]
    nrs = xb.shape[0] if isinstance(x, tuple) else xb.shape[0] - npr
    m = npr + nrs
    nbp, nbs = npr // seq_p, nrs // seq_s
    past = cache_ckv.shape[1]
    W, (wcat_r, wuq_r, wvt), wo, tabs = MW["plain"], MW["rope"], MW["wo"], MW["tabs"]
    wuk, gkn = W[4], W[7]
    g1 = P["g_norm1"].reshape(-1, 1, d)
    hq = MLA_HEADS * HEAD_PAD
    hv = MLA_HEADS * V_HEAD
    tm = min(512, seq_s)
    cidx = _cond_index(tm, npr, seq_s)
    xb0 = 0 if isinstance(x, tuple) else npr // tm
    tm_p = min(1024, seq_s)
    qp, kp, vp, ckv, kpe128 = _mla_proj(xa, 0, mod4, g1, W, layer, j, 0, npr, _cond_index(tm_p, npr, seq_s), tm_p,
                                        None)
    qs, ks, vts = _mla_proj(xb, xb0, mod4, g1, W, layer, j, npr, nrs, cidx, tm, (wcat_r, wuq_r, wvt, tabs))
    cc = cache_ckv.reshape(nbs * past, -1)
    ck = jnp.pad(cache_kpe.reshape(nbs * past, -1), ((0, 0), (QK_NOPE, HEAD_PAD - QK_DIM)))
    kc, vtc = pl.pallas_call(
        _kv_ctx_kernel,
        grid=(nbs,),
        in_specs=[pl.BlockSpec((past, cc.shape[1]), lambda i: (i, 0)),
                  pl.BlockSpec((past, HEAD_PAD), lambda i: (i, 0)),
                  _layer_spec(wuk, j), _layer_spec(wvt, j), _layer_spec(gkn, j)],
        out_specs=[pl.BlockSpec((past, hq), lambda i: (i, 0)), pl.BlockSpec((hv, past), lambda i: (0, i))],
        out_shape=[jax.ShapeDtypeStruct((nbs * past, hq), BF16), jax.ShapeDtypeStruct((hv, nbs * past), BF16)],
        compiler_params=_cparams(("arbitrary",)),
        name="mla_kv_ctx",
    )(cc, ck, wuk, wvt, gkn)
    xp0 = 0
    xs0 = 0 if isinstance(x, tuple) else npr
    xn_p = pl.pallas_call(
        functools.partial(_attn_prompt_kernel, d=d),
        grid=(nbp,),
        in_specs=[pl.BlockSpec((seq_p, hq), lambda b: (b, 0)),
                  pl.BlockSpec((seq_p, hq), lambda b: (b, 0)),
                  pl.BlockSpec((seq_p, hv), lambda b: (b, 0)),
                  pl.BlockSpec((seq_p, d), lambda b: (xp0 // seq_p + b, 0)),
                  pl.BlockSpec((1, 1, 1, 6 * d), lambda b: (layer, 0, 0, 0)),
                  _layer_spec(wo, j)],
        out_specs=pl.BlockSpec((seq_p, d), lambda b: (b, 0)),
        out_shape=jax.ShapeDtypeStruct((m, d), F32),
        scratch_shapes=[pltpu.VMEM((seq_p, hv), BF16)],
        compiler_params=_cparams(("arbitrary",)),
        name="attn_prompt",
    )(qp, kp, vp, xa, mod4, wo)
    tq = min(seq_s, 512)
    nqt = seq_s // tq
    o0 = npr // tq
    xn = pl.pallas_call(
        functools.partial(_attn_sample_kernel, d=d),
        grid=(nbs, nqt),
        in_specs=[pl.BlockSpec((tq, hq), lambda b, qi: (b * nqt + qi, 0)),
                  pl.BlockSpec((past, hq), lambda b, qi: (b, 0)),
                  pl.BlockSpec((hv, past), lambda b, qi: (0, b)),
                  pl.BlockSpec((seq_s, hq), lambda b, qi: (b, 0)),
                  pl.BlockSpec((hv, seq_s), lambda b, qi: (0, b)),
                  pl.BlockSpec((tq, d), lambda b, qi: (xs0 // tq + b * nqt + qi, 0)),
                  pl.BlockSpec((1, 1, 1, 6 * d), lambda b, qi: (layer, 1 + b, 0, 0)),
                  _layer_spec(wo, j),
                  pl.BlockSpec(memory_space=pl.ANY)],
        out_specs=pl.BlockSpec((tq, d), lambda b, qi: (o0 + b * nqt + qi, 0)),
        out_shape=jax.ShapeDtypeStruct((m, d), F32),
        input_output_aliases={8: 0},
        compiler_params=_cparams(("arbitrary", "arbitrary")),
        name="attn_sample",
    )(qs, kc, vtc, ks, vts, xb, mod4, wo, xn_p)
    new_ckv = ckv.reshape(nbp, seq_p, -1)
    new_kpe = kpe128[:, QK_NOPE:QK_DIM].reshape(nbp, seq_p, QK_ROPE)
    return xn, new_ckv, new_kpe


SSD_R = 8
SSD_GW = SSD_R * SSD_HEADDIM
SSD_NX = SSD_GW // LANES
SSD_DTW = 2 * SSD_R
SSD_TOP = 8


def _ssd_in_kernel(x_ref, mod_ref, g_ref, w_ref, wdt_ref, zx_ref, dt_ref, h_sc, *, d, heads):
    i = pl.program_id(1)
    tm = x_ref.shape[0]
    rows = pl.ds(pl.multiple_of(i * tm, tm), tm)

    @pl.when(pl.program_id(0) == 0)
    def _():
        h = _normmod(x_ref[...], g_ref[0], mod_ref[0, 0, :, 0:d], mod_ref[0, 0, :, d:2 * d]).astype(BF16)
        h_sc[rows, :] = h
        row = lax.broadcasted_iota(jnp.int32, wdt_ref.shape[1:], 0)
        wdt = jnp.where(row < 2 * heads, wdt_ref[0], 0.0).astype(BF16)
        dtt = lax.dot_general(wdt, h, _NT, preferred_element_type=F32)
        for g in range(SSD_GROUPS):
            dt_ref[g, 0:SSD_R, :] = dtt[g * SSD_R:(g + 1) * SSD_R, :]
            dt_ref[g, SSD_R:SSD_DTW, :] = dtt[heads + g * SSD_R:heads + (g + 1) * SSD_R, :]

    r = lax.dot_general(h_sc[rows, :], w_ref[0].astype(BF16), _NT, preferred_element_type=F32)
    for c in range(r.shape[1] // LANES):
        zx_ref[c] = r[:, c * LANES:(c + 1) * LANES]


def _split3(x):
    hi = x.astype(BF16)
    r1 = x - hi.astype(F32)
    mid = r1.astype(BF16)
    lo = (r1 - mid.astype(F32)).astype(BF16)
    return hi, mid, lo


def _dot3_right(x, tri):
    return sum(jnp.dot(p, tri, preferred_element_type=F32) for p in _split3(x))


SSD_NPIECE = 7


def _ssd_spread_matrix():
    sel = np.zeros((2, CHUNK, SSD_R * LANES + 2 * SSD_GW), np.float32)
    for dr in range(2):
        for r in range(SSD_R):
            h = dr * SSD_R + r
            for p in range(3):
                sel[dr, p * SSD_DTW + h, r * LANES:(r + 1) * LANES] = 1.0
            for p in range(3, 5):
                sel[dr, p * SSD_DTW + h, SSD_R * LANES + r * SSD_HEADDIM:SSD_R * LANES + (r + 1) * SSD_HEADDIM] = 1.0
            for p in range(5, 7):
                c0 = SSD_R * LANES + SSD_GW
                sel[dr, p * SSD_DTW + h, c0 + r * SSD_HEADDIM:c0 + (r + 1) * SSD_HEADDIM] = 1.0
    return jnp.asarray(sel, BF16)


def _ssd_core_kernel(*refs, t, has_h0):
    (z_ref, x_ref, b_ref, c_ref, dt_ref, wx_ref, wb_ref, wc_ref, bx_ref, bb_ref, bc_ref,
     dtb_ref, alog_ref, dsum_ref, gn_ref, bsel_ref) = refs[:16]
    if has_h0:
        h0_ref, y_ref = refs[16], refs[18]
        st_ref = None
        scr = refs[19:]
    else:
        y_ref, st_ref = refs[16:18]
        h0_ref = None
        scr = refs[18:]
    (pad_sc, xc_sc, bcv_sc, ccv_sc, bt_sc, cb_sc, y_sc, dtt_sc, dat_sc, s_sc) = scr
    nc = t // CHUNK
    half = (SSD_CONV - 1) // 2
    nblk = SSD_NX + 2

    @pl.when((pl.program_id(0) == 0) & (pl.program_id(1) == 0))
    def _():
        pad_sc[:, 0:SSD_TOP, :] = jnp.zeros((nblk, SSD_TOP, LANES), F32)
        pad_sc[:, SSD_TOP + t:2 * SSD_TOP + t, :] = jnp.zeros((nblk, SSD_TOP, LANES), F32)

    pad_sc[0:SSD_NX, SSD_TOP:SSD_TOP + t, :] = x_ref[...]
    pad_sc[SSD_NX, SSD_TOP:SSD_TOP + t, :] = b_ref[0]
    pad_sc[SSD_NX + 1, SSD_TOP:SSD_TOP + t, :] = c_ref[0]

    def conv_chunk(c):
        r0 = c * CHUNK
        for blk in range(nblk):
            if blk < SSD_NX:
                w_r, bias, widx = wx_ref, bx_ref[blk], blk
            elif blk == SSD_NX:
                w_r, bias, widx = wb_ref, bb_ref[0], 0
            else:
                w_r, bias, widx = wc_ref, bc_ref[0], 0
            acc = jnp.broadcast_to(bias, (CHUNK, LANES))
            for k in range(SSD_CONV):
                acc = acc + w_r[widx, k:k + 1, :] * pad_sc[blk, r0 + SSD_TOP - half + k:r0 + SSD_TOP - half + k + CHUNK, :]
            acc = _silu(acc)
            if blk < SSD_NX:
                xc_sc[blk, r0:r0 + CHUNK, :] = acc
            elif blk == SSD_NX:
                bcv_sc[r0:r0 + CHUNK, :] = acc.astype(BF16)
                bt_sc[c] = acc.T.astype(BF16)
            else:
                ccv_sc[r0:r0 + CHUNK, :] = acc.astype(BF16)

    def gate_chunk(c):
        rows = slice(c * CHUNK, (c + 1) * CHUNK)
        yz = [y_sc[q, rows, :] * _silu(z_ref[q, rows, :]) for q in range(SSD_NX)]
        ms = sum(jnp.sum(a * a, axis=-1, keepdims=True) for a in yz) * (1.0 / SSD_GW)
        rstd = lax.rsqrt(ms + EPS)
        for q in range(SSD_NX):
            sl = slice(q * LANES, (q + 1) * LANES)
            y_ref[rows, sl] = (yz[q] * rstd * gn_ref[0, :, sl]).astype(BF16)

    a_row = -jnp.exp(alog_ref[0]) * _LOG2E
    for c in range(nc):
        v = dt_ref[0, :, c * CHUNK:(c + 1) * CHUNK] + dtb_ref[0]
        dt = jnp.maximum(v, 0.0) + jnp.log1p(jnp.exp(-jnp.abs(v)))
        dtt_sc[c] = dt
        dat_sc[c] = dt * a_row

    row = lax.broadcasted_iota(jnp.int32, (CHUNK, CHUNK), 0)
    colm = lax.broadcasted_iota(jnp.int32, (CHUNK, CHUNK), 1)
    lower = colm <= row
    upper = colm >= row
    tri_l = jnp.where(lower, 1.0, 0.0).astype(BF16)
    tri_u = jnp.where(upper, 1.0, 0.0).astype(BF16)
    lane = lax.broadcasted_iota(jnp.int32, (CHUNK, LANES), 1)
    head_row = lax.broadcasted_iota(jnp.int32, (SSD_DTW, 1), 0)
    zero16 = jnp.zeros((SSD_DTW, CHUNK), F32)

    col0, w0, e0 = 0, SSD_R * LANES, SSD_R * LANES + SSD_GW
    zpad = jnp.zeros((CHUNK - SSD_NPIECE * SSD_DTW, CHUNK), F32)

    def prepare(dr, c):
        lo = dr * SSD_R
        tri_t = tri_l if dr else tri_u
        own = (head_row >= lo) & (head_row < lo + SSD_R)
        rows = slice(c * CHUNK, (c + 1) * CHUNK)
        dtt = dtt_sc[c]
        cum_t = _dot3_right(dat_sc[c], tri_t)
        edge = cum_t[:, 0:1] if dr else cum_t[:, CHUNK - 1:CHUNK]
        w2_t = dtt * jnp.exp2(edge - cum_t)
        ecum_t = jnp.exp2(cum_t)
        pieces = (list(_split3(jnp.where(own, cum_t, zero16)))
                  + list(_split3(jnp.where(own, w2_t, zero16))[:2])
                  + list(_split3(jnp.where(own, ecum_t, zero16))[:2]))
        staged = jnp.concatenate([p.astype(F32) for p in pieces] + [zpad], axis=0)
        spread = jnp.dot(staged.T.astype(BF16), bsel_ref[dr], preferred_element_type=F32)
        cmat = ccv_sc[rows, :]
        if dr == 0:
            cb = lax.dot_general(cmat, bcv_sc[rows, :], _NT, preferred_element_type=F32)
            cb_sc[c] = cb
        else:
            cb = cb_sc[c]
        rowterm = cum_t - jnp.log2(dtt)
        return dict(dr=dr, c=c, rows=rows, lo=lo, rowterm=rowterm, spread=spread, cmat=cmat, cb=cb)

    def scan_chunk(pre):
        dr, c, rows, lo = pre["dr"], pre["c"], pre["rows"], pre["lo"]
        rowterm, spread, cmat, cb = pre["rowterm"], pre["spread"], pre["cmat"], pre["cb"]
        mask = upper if dr else lower
        s_prev = s_sc[...]
        yoff = jnp.dot(cmat, s_prev.astype(BF16), preferred_element_type=F32)
        xd = []
        for q in range(SSD_NX):
            xblk = xc_sc[q, rows, :]
            xb16 = xblk.astype(BF16)
            mh = []
            for r in (2 * q, 2 * q + 1):
                seg = spread[:, col0 + r * LANES:col0 + (r + 1) * LANES] - rowterm[lo + r:lo + r + 1, :]
                lm = jnp.exp2(jnp.where(mask, seg, -jnp.inf))
                mh.append((cb * lm).astype(BF16))
            rhs = jnp.concatenate([jnp.where(lane < SSD_HEADDIM, xb16, jnp.zeros_like(xb16)),
                                   jnp.where(lane >= SSD_HEADDIM, xb16, jnp.zeros_like(xb16))], axis=0)
            y = jnp.dot(jnp.concatenate(mh, axis=1), rhs, preferred_element_type=F32)
            sl = slice(q * LANES, (q + 1) * LANES)
            y = y + yoff[:, sl] * spread[:, e0 + q * LANES:e0 + (q + 1) * LANES]
            if dr == 0:
                y_sc[q, rows, :] = y + xblk * dsum_ref[0, :, sl]
            else:
                y_sc[q, rows, :] = y_sc[q, rows, :] + y
            xd.append((xblk * spread[:, w0 + q * LANES:w0 + (q + 1) * LANES]).astype(BF16))
        st = jnp.dot(bt_sc[c], jnp.concatenate(xd, axis=1), preferred_element_type=F32)
        er = 0 if dr else CHUNK - 1
        cd = spread[er:er + 1, e0:e0 + SSD_GW]
        s_sc[...] = s_prev * cd + st

    order = [(0, c) for c in range(nc)] + [(1, c) for c in range(nc - 1, -1, -1)]
    conv_chunk(0)
    pre = prepare(*order[0])
    for k, (dr, c) in enumerate(order):
        if c == (nc - 1 if dr else 0):
            if has_h0:
                s_sc[...] = h0_ref[0, dr, 0].T
            else:
                s_sc[...] = jnp.zeros_like(s_sc)
        if dr == 0 and c + 1 < nc:
            conv_chunk(c + 1)
        nxt = prepare(*order[k + 1]) if k + 1 < len(order) else None
        scan_chunk(pre)
        pre = nxt
        if dr == 1:
            gate_chunk(c)
        if st_ref is not None and c == (0 if dr else nc - 1):
            st_ref[0, dr, 0] = s_sc[...].T


def _ssd_core(zx, dtc, cw, cbias, dtb, alog, dsum, gn, h0, ybuf, t, row0, nseq):
    has_h0 = h0 is not None
    m = zx.shape[1]
    bsel = _ssd_spread_matrix()
    b0 = row0 // t
    nc = t // CHUNK
    nz = SSD_GROUPS * SSD_NX
    in_specs = [pl.BlockSpec((SSD_NX, t, LANES), lambda b, g: (g, b0 + b, 0)),
                pl.BlockSpec((SSD_NX, t, LANES), lambda b, g: (SSD_GROUPS + g, b0 + b, 0)),
                pl.BlockSpec((1, t, LANES), lambda b, g: (2 * nz + g, b0 + b, 0)),
                pl.BlockSpec((1, t, LANES), lambda b, g: (2 * nz + SSD_GROUPS + g, b0 + b, 0)),
                pl.BlockSpec((1, SSD_DTW, t), lambda b, g: (g, 0, b0 + b)),
                pl.BlockSpec((SSD_NX, 8, LANES), lambda b, g: (g, 0, 0)),
                pl.BlockSpec((1, 8, LANES), lambda b, g: (nz + g, 0, 0)),
                pl.BlockSpec((1, 8, LANES), lambda b, g: (nz + SSD_GROUPS + g, 0, 0)),
                pl.BlockSpec((SSD_NX, 1, LANES), lambda b, g: (g, 0, 0)),
                pl.BlockSpec((1, 1, LANES), lambda b, g: (nz + g, 0, 0)),
                pl.BlockSpec((1, 1, LANES), lambda b, g: (nz + SSD_GROUPS + g, 0, 0)),
                pl.BlockSpec((1, SSD_DTW, CHUNK), lambda b, g: (g, 0, 0)),
                pl.BlockSpec((1, SSD_DTW, CHUNK), lambda b, g: (g, 0, 0)),
                pl.BlockSpec((1, 1, SSD_GW), lambda b, g: (g, 0, 0)),
                pl.BlockSpec((1, 1, SSD_GW), lambda b, g: (g, 0, 0)),
                pl.BlockSpec(bsel.shape, lambda b, g: (0, 0, 0))]
    args = [zx, zx, zx, zx, dtc, cw, cw, cw, cbias, cbias, cbias, dtb, alog, dsum, gn, bsel]
    st_spec = pl.BlockSpec((1, 2, 1, SSD_GW, SSD_STATE), lambda b, g: (b, 0, g, 0, 0))
    y_spec = pl.BlockSpec((t, SSD_GW), lambda b, g: (b0 + b, g))
    y_shape = jax.ShapeDtypeStruct((m, SSD_GROUPS * SSD_GW), BF16)
    aliases = {}
    if has_h0:
        in_specs += [st_spec, pl.BlockSpec(memory_space=pl.ANY)]
        args += [h0, ybuf]
        aliases = {len(args) - 1: 0}
        out_specs, out_shape = y_spec, y_shape
    else:
        out_specs = [y_spec, st_spec]
        out_shape = [y_shape, jax.ShapeDtypeStruct((nseq, 2, SSD_GROUPS, SSD_GW, SSD_STATE), F32)]
    scratch = [pltpu.VMEM((SSD_NX + 2, t + 2 * SSD_TOP, LANES), F32),
               pltpu.VMEM((SSD_NX, t, LANES), F32),
               pltpu.VMEM((t, LANES), BF16),
               pltpu.VMEM((t, LANES), BF16),
               pltpu.VMEM((nc, CHUNK, CHUNK), BF16),
               pltpu.VMEM((nc, CHUNK, CHUNK), F32),
               pltpu.VMEM((SSD_NX, t, LANES), F32),
               pltpu.VMEM((nc, SSD_DTW, CHUNK), F32),
               pltpu.VMEM((nc, SSD_DTW, CHUNK), F32),
               pltpu.VMEM((SSD_STATE, SSD_GW), F32)]
    return pl.pallas_call(
        functools.partial(_ssd_core_kernel, t=t, has_h0=has_h0),
        grid=(nseq, SSD_GROUPS),
        in_specs=in_specs, out_specs=out_specs, out_shape=out_shape,
        input_output_aliases=aliases,
        scratch_shapes=scratch,
        compiler_params=_cparams(("arbitrary", "arbitrary")),
        name="ssd_core_h0" if has_h0 else "ssd_core",
    )(*args)


def _ssd_layer(x, mod4, P, layer, j, st0, npr, seq_p, seq_s):
    m, d = x.shape
    nbp, nbs = npr // seq_p, (m - npr) // seq_s
    heads = SSD_GROUPS * SSD_R
    inner = heads * SSD_HEADDIM
    gn_w = SSD_GROUPS * SSD_STATE
    nzx = 2 * inner + 2 * gn_w
    tm, tn = min(1024, seq_s), 1280
    cidx = _cond_index(tm, npr, seq_s)
    w_in_t = jnp.swapaxes(P["ssd_w_in"], 1, 2)
    nt = m // tm

    def row_tile(n, i):
        return jnp.where(n == 0, i, nt - 1)

    zx, dtc = pl.pallas_call(
        functools.partial(_ssd_in_kernel, d=d, heads=heads),
        grid=(nzx // tn, nt),
        in_specs=[pl.BlockSpec((tm, d), lambda n, i: (row_tile(n, i), 0)),
                  pl.BlockSpec((1, 1, 1, 6 * d), lambda n, i: (layer, cidx(row_tile(n, i)), 0, 0)),
                  pl.BlockSpec((1, 1, d), lambda n, i: (layer, 0, 0)),
                  pl.BlockSpec((1, tn, d), lambda n, i: (j, n, 0)),
                  pl.BlockSpec((1, LANES, d), lambda n, i: (j, nzx // LANES, 0))],
        out_specs=[pl.BlockSpec((tn // LANES, tm, LANES), lambda n, i: (n, i, 0)),
                   pl.BlockSpec((SSD_GROUPS, SSD_DTW, tm), lambda n, i: (0, 0, row_tile(n, i)))],
        out_shape=[jax.ShapeDtypeStruct((nzx // LANES, m, LANES), F32),
                   jax.ShapeDtypeStruct((SSD_GROUPS, SSD_DTW, m), F32)],
        scratch_shapes=[pltpu.VMEM((m, d), BF16)],
        compiler_params=_cparams(("arbitrary", "arbitrary")),
        name="ssd_in",
    )(x, mod4, P["g_norm1"].reshape(-1, 1, d), w_in_t, w_in_t)

    def compact(a):
        a = a.reshape(2, SSD_GROUPS, SSD_R).transpose(1, 0, 2).reshape(SSD_GROUPS, SSD_DTW, 1)
        return jnp.broadcast_to(a, (SSD_GROUPS, SSD_DTW, CHUNK))

    ncb = (inner + 2 * gn_w) // LANES
    cw = jnp.pad(P["ssd_w_conv"][j], ((0, 8 - SSD_CONV), (0, 0))).reshape(8, ncb, LANES).transpose(1, 0, 2)
    cbias = P["ssd_b_conv"][j].reshape(ncb, 1, LANES)
    dsum = jnp.repeat(P["ssd_d"][j][0] + P["ssd_d"][j][1], SSD_HEADDIM).reshape(SSD_GROUPS, 1, SSD_GW)
    gn = P["ssd_g_norm"][j].reshape(SSD_GROUPS, 1, SSD_GW)
    common = (zx, dtc, cw, cbias, compact(P["ssd_dt_bias"][j]), compact(P["ssd_a_log"][j]), dsum, gn)
    yp, st = _ssd_core(*common, None, None, seq_p, 0, nbp)
    h0 = st0.reshape(nbs, 2, SSD_GROUPS, SSD_GW, SSD_STATE)
    y = _ssd_core(*common, h0, yp, seq_s, npr, nbs)
    return y, st.reshape(nbp, 2, heads, SSD_HEADDIM, SSD_STATE)


def kernel(x_prompt, x_sample, cache_ckv, cache_kpe, state_ssm, c, c_ctx, w_ada, b_ada, g_norm1, g_norm2,
           mla_w_dq, mla_g_q, mla_w_uq, mla_w_dkv, mla_g_kv, mla_w_ukv, mla_g_qn, mla_g_kn, mla_w_o,
           cv_w_pw1, cv_b_pw1, cv_w_dw, cv_b_dw, cv_g_ln, cv_b_ln, cv_w_pw2, cv_b_pw2,
           ssd_w_in, ssd_w_conv, ssd_b_conv, ssd_dt_bias, ssd_a_log, ssd_d, ssd_g_norm, ssd_w_out,
           ffn_w_in, ffn_w_out):
    P = dict(g_norm1=g_norm1, g_norm2=g_norm2,
             mla_w_dq=mla_w_dq, mla_g_q=mla_g_q, mla_w_uq=mla_w_uq, mla_w_dkv=mla_w_dkv, mla_g_kv=mla_g_kv,
             mla_w_ukv=mla_w_ukv, mla_g_qn=mla_g_qn, mla_g_kn=mla_g_kn, mla_w_o=mla_w_o,
             cv_w_pw1=cv_w_pw1, cv_b_pw1=cv_b_pw1, cv_w_dw=cv_w_dw, cv_b_dw=cv_b_dw, cv_g_ln=cv_g_ln,
             cv_b_ln=cv_b_ln, cv_w_pw2=cv_w_pw2, cv_b_pw2=cv_b_pw2,
             ssd_w_in=ssd_w_in, ssd_w_conv=ssd_w_conv, ssd_b_conv=ssd_b_conv, ssd_dt_bias=ssd_dt_bias,
             ssd_a_log=ssd_a_log, ssd_d=ssd_d, ssd_g_norm=ssd_g_norm, ssd_w_out=ssd_w_out)
    nbp, seq_p, d = x_prompt.shape
    nbs, seq_s, _ = x_sample.shape
    depth = w_ada.shape[0]
    npr = nbp * seq_p
    x = (x_prompt.reshape(npr, d), x_sample.reshape(nbs * seq_s, d))
    cond8 = jnp.concatenate([c_ctx[None, :], c, jnp.zeros((8 - 1 - nbs, d), F32)], axis=0)
    mod4 = _ada(cond8, w_ada, b_ada).reshape(depth, 8, 1, 6 * d)
    mla_w = _mla_weights(P, seq_s)
    ffn_tm = min(1024, seq_s)
    ffn_cidx = _cond_index(ffn_tm, npr, seq_s)
    ckvs, kpes, ssms = [], [], []
    for i in range(depth):
        kind, j = i % N_MIXERS, i // N_MIXERS
        if kind != 0 and isinstance(x, tuple):
            x = jnp.concatenate(x, axis=0)
        y = w_o = None
        if kind == 0:
            x, ckv, kpe = _mla_layer(x, mod4, P, mla_w, i, j, cache_ckv[:, j], cache_kpe[:, j], npr, seq_p, seq_s)
            ckvs.append(ckv)
            kpes.append(kpe)
        elif kind == 1:
            x = _conv_layer(x, mod4, P, i, j, npr, seq_p, seq_s)
        else:
            y, st = _ssd_layer(x, mod4, P, i, j, state_ssm[:, j], npr, seq_p, seq_s)
            w_o = ssd_w_out[j].astype(BF16)
            ssms.append(st)
        x = _ffn(x, mod4, g_norm2, ffn_w_in, ffn_w_out, i, ffn_cidx, ffn_tm, 256, npr,
                 y=y, w_o=w_o, split=(i == depth - 1))
    return (x[0].reshape(nbp, seq_p, d), x[1].reshape(nbs, seq_s, d),
            jnp.stack(ckvs, axis=1), jnp.stack(kpes, axis=1), jnp.stack(ssms, axis=1))
```
